```python
import math
import jax, jax.numpy as jnp
from jax import lax
import numpy as np

D_MODEL = 1024
BATCH = 8
SEQ = 8192
DEPTH = 2

GRID_W = 64
CTX_LEN = 256
NORM_EPS = 1e-6

RW_HEADS = 4
RW_HD = 64
RW_W = RW_HEADS * RW_HD
RW_DECAY_RANK = 64
RW_ICLR_RANK = 64
RW_GATE_RANK = 128
RW_GN_EPS = 64e-5
RW_SPLITS = [RW_W, 2 * RW_W, 3 * RW_W, 3 * RW_W + RW_DECAY_RANK, 3 * RW_W + RW_DECAY_RANK + RW_ICLR_RANK]
RW_COLS = 3 * RW_W + RW_DECAY_RANK + RW_ICLR_RANK + RW_GATE_RANK

SSM_HEADS = 8
SSM_HD = 64
SSM_W = SSM_HEADS * SSM_HD
SSM_GROUPS = 2
SSM_STATE = 64
SSM_CHUNK = 128
SSM_CONV = 3
SSM_CONV_CH = SSM_W + 2 * SSM_GROUPS * SSM_STATE
SSM_COLS = SSM_W + SSM_CONV_CH + SSM_HEADS

GLA_HEADS = 4
GLA_DK = 32
GLA_DV = 64
GLA_KW = GLA_HEADS * GLA_DK
GLA_VW = GLA_HEADS * GLA_DV
GLA_GATE_RANK = 16
GLA_TAU = 16.0
GLA_CHUNK = 64
GLA_COLS = 2 * GLA_KW + GLA_VW + GLA_GATE_RANK + GLA_VW

IN_COLS = RW_COLS + SSM_COLS + GLA_COLS
MIX_W = RW_W + SSM_W + GLA_VW

N_GROUPS = 4
EXPERTS_PER_GROUP = 4
N_EXPERTS = N_GROUPS * EXPERTS_PER_GROUP
TOP_K = 2
D_EXPERT = 512

kernel_name = 'hybrid_rwkv7_ssd_gla_hmoe_diffusion_block'


def rmsnorm(h, g):
    hf = h.astype(jnp.float32)
    hf = hf * lax.rsqrt(jnp.mean(hf * hf, -1, keepdims=True) + NORM_EPS)
    return (hf * g.astype(jnp.float32)).astype(h.dtype)


def modulate(h, shift, scale):
    return h * (1 + scale[:, None]) + shift[:, None]


def head_layernorm(y, g, b):
    yf = y.astype(jnp.float32)
    mu = jnp.mean(yf, -1, keepdims=True)
    var = jnp.mean(jnp.square(yf - mu), -1, keepdims=True)
    yn = (yf - mu) * lax.rsqrt(var + RW_GN_EPS)
    return yn.reshape(y.shape[:2] + (-1,)) * g + b


def head_rmsnorm(y, g):
    yf = y.astype(jnp.float32)
    yn = yf * lax.rsqrt(jnp.mean(yf * yf, -1, keepdims=True) + NORM_EPS)
    return yn.reshape(y.shape[:2] + (-1,)) * g


def token_shift(u, mu_prev, mu_next):
    pad = jnp.zeros_like(u[:, :1])
    prev = jnp.concatenate([pad, u[:, :-1]], 1)
    nxt = jnp.concatenate([u[:, 1:], pad], 1)
    return u + mu_prev * (prev - u) + mu_next * (nxt - u)


def dwconv_grid(u, w, b, rows):
    bsz, n, ch = u.shape
    img = u.reshape(bsz, rows, n // rows, ch)
    out = lax.conv_general_dilated(img, w[:, :, None, :].astype(u.dtype), (1, 1), 'SAME',
                                   dimension_numbers=('NHWC', 'HWIO', 'NHWC'), feature_group_count=ch)
    return out.reshape(bsz, n, ch) + b


def rwkv7_scan(r, w, k, v, a, b, s0):
    def step(s, inp):
        r_t, w_t, k_t, v_t, a_t, b_t = inp
        sa = jnp.einsum('bhvk,bhk->bhv', s, a_t)
        s = s * w_t[:, :, None, :] + sa[..., None] * b_t[:, :, None, :] + v_t[..., None] * k_t[:, :, None, :]
        return s, jnp.einsum('bhvk,bhk->bhv', s, r_t)
    xs = tuple(jnp.moveaxis(z.astype(jnp.float32), 1, 0) for z in (r, w, k, v, a, b))
    s_fin, ys = lax.scan(step, s0, xs)
    return jnp.moveaxis(ys, 0, 1), s_fin


def ssd_chunked(x, dA, bm, cm, s0):
    bsz, t, h, p = x.shape
    n = bm.shape[-1]
    L = SSM_CHUNK
    nc = t // L
    x = x.astype(jnp.float32).reshape(bsz, nc, L, h, p)
    bm = bm.astype(jnp.float32).reshape(bsz, nc, L, h, n)
    cm = cm.astype(jnp.float32).reshape(bsz, nc, L, h, n)
    a_cs = jnp.cumsum(dA.astype(jnp.float32).reshape(bsz, nc, L, h).transpose(0, 3, 1, 2), -1)
    causal = jnp.tril(jnp.ones((L, L), bool))
    decay_in = jnp.exp(jnp.where(causal, a_cs[..., :, None] - a_cs[..., None, :], -jnp.inf))
    scores = jnp.einsum('bclhn,bcshn->bhcls', cm, bm) * decay_in
    y_diag = jnp.einsum('bhcls,bcshp->bclhp', scores, x)
    decay_to_end = jnp.exp(a_cs[..., -1:] - a_cs)
    states = jnp.einsum('bclhn,bhcl,bclhp->bchpn', bm, decay_to_end, x)
    states = jnp.concatenate([s0.astype(jnp.float32)[:, None], states], 1)
    chunk_cs = jnp.cumsum(jnp.pad(a_cs[..., -1], ((0, 0), (0, 0), (1, 0))), -1)
    causal_c = jnp.tril(jnp.ones((nc + 1, nc + 1), bool))
    decay_chunk = jnp.exp(jnp.where(causal_c, chunk_cs[..., :, None] - chunk_cs[..., None, :], -jnp.inf))
    states = jnp.einsum('bhzc,bchpn->bzhpn', decay_chunk, states)
    y_off = jnp.einsum('bclhn,bchpn,bhcl->bclhp', cm, states[:, :-1], jnp.exp(a_cs))
    return (y_diag + y_off).reshape(bsz, t, h, p), states[:, -1]


def gla_chunked(q, k, v, logg, s0):
    bsz, t, h, dk = q.shape
    dv = v.shape[-1]
    L = GLA_CHUNK
    nc = t // L
    rs = lambda z: z.astype(jnp.float32).reshape(bsz, nc, L, h, z.shape[-1])
    q, k, v, logg = rs(q), rs(k), rs(v), rs(logg)
    b_cs = jnp.cumsum(logg, 2)
    ref = b_cs[:, :, L // 2:L // 2 + 1]
    att = jnp.einsum('bclhd,bcshd->bchls', q * jnp.exp(b_cs - ref), k * jnp.exp(ref - b_cs))
    att = jnp.where(jnp.tril(jnp.ones((L, L), bool)), att, 0.0)
    o_intra = jnp.einsum('bchls,bcshe->bclhe', att, v)
    b_end = b_cs[:, :, -1]
    kv = jnp.einsum('bcshd,bcshe->bchde', k * jnp.exp(b_end[:, :, None] - b_cs), v)
    def step(s, inp):
        kv_c, be_c = inp
        return s * jnp.exp(be_c)[..., None] + kv_c, s
    s_fin, s_start = lax.scan(step, s0.astype(jnp.float32), (jnp.moveaxis(kv, 1, 0), jnp.moveaxis(b_end, 1, 0)))
    o_inter = jnp.einsum('bclhd,bchde->bclhe', q * jnp.exp(b_cs), jnp.moveaxis(s_start, 0, 1))
    return (o_intra + o_inter).reshape(bsz, t, h, dv), s_fin


def bidir(scan_fn, ctx_fwd, lat_fwd, ctx_bwd, lat_bwd, s0, ctx_out):
    flip = lambda zs: tuple(jnp.flip(z, 1) for z in zs)
    yc_f, sc_f = scan_fn(*ctx_fwd, s0)
    yl_f, _ = scan_fn(*lat_fwd, sc_f)
    yc_b, sc_b = scan_fn(*flip(ctx_bwd), s0)
    yl_b, _ = scan_fn(*flip(lat_bwd), sc_b)
    yl = yl_f + jnp.flip(yl_b, 1)
    yc = yc_f + jnp.flip(yc_b, 1) if ctx_out else None
    return yc, yl


def rwkv_group(u_c, u_l, mu_prev, mu_next, w0, w2, a0, a2, g2, k_k, k_a, r_k, gn_g, gn_b, ctx_out):
    def heads(z):
        return z.reshape(z.shape[:2] + (RW_HEADS, RW_HD))

    def prep(u):
        r, k, v, wl, al, gl = jnp.split(token_shift(u, mu_prev, mu_next), RW_SPLITS, -1)
        return r, k, v, jnp.tanh(wl), al, gl

    def scan_inputs(p, d):
        r, k, v, wl, al, _ = p
        w_log = -jax.nn.softplus(-(w0[d] + wl @ w2[d])) - 0.5
        decay = jnp.exp(-jnp.exp(w_log.astype(jnp.float32)))
        a = jax.nn.sigmoid(a0[d] + al @ a2[d])
        kk = heads(k * k_k).astype(jnp.float32)
        kk = kk * lax.rsqrt(jnp.sum(kk * kk, -1, keepdims=True) + 1e-12)
        k_mod = k * (1 + (a - 1) * k_a)
        return (heads(r), heads(decay), heads(k_mod), heads(v), -kk, kk * heads(a))

    def finish(p, y):
        r, k, v, _, _, gl = p
        bonus = jnp.sum(heads(r * k * r_k), -1, keepdims=True) * heads(v)
        out = head_layernorm(y, gn_g, gn_b) + bonus.reshape(r.shape)
        return (out * (jax.nn.sigmoid(gl) @ g2)).astype(r.dtype)

    pc, pl = prep(u_c), prep(u_l)
    s0 = jnp.zeros((u_l.shape[0], RW_HEADS, RW_HD, RW_HD), jnp.float32)
    yc, yl = bidir(rwkv7_scan, scan_inputs(pc, 0), scan_inputs(pl, 0), scan_inputs(pc, 1), scan_inputs(pl, 1), s0, ctx_out)
    return (finish(pc, yc) if ctx_out else None), finish(pl, yl)


def ssm_group(u_c, u_l, rows, conv_w, conv_b, dt_bias, a_log, d_skip, norm_g, ctx_out):
    def prep(u, grid_rows):
        bsz, t = u.shape[:2]
        z, xbc, dt = jnp.split(u, [SSM_W, SSM_W + SSM_CONV_CH], -1)
        xbc = jax.nn.silu(dwconv_grid(xbc, conv_w, conv_b, grid_rows))
        xs, bm, cm = jnp.split(xbc, [SSM_W, SSM_W + SSM_GROUPS * SSM_STATE], -1)
        rep = SSM_HEADS // SSM_GROUPS
        bm = jnp.repeat(bm.reshape(bsz, t, SSM_GROUPS, SSM_STATE), rep, axis=2)
        cm = jnp.repeat(cm.reshape(bsz, t, SSM_GROUPS, SSM_STATE), rep, axis=2)
        return z, xs.reshape(bsz, t, SSM_HEADS, SSM_HD), bm, cm, dt

    def scan_inputs(p, d):
        _, xs, bm, cm, dt = p
        dtp = jax.nn.softplus(dt.astype(jnp.float32) + dt_bias[d])
        dA = -jnp.exp(a_log[d]) * dtp
        return (xs * dtp[..., None], dA, bm, cm)

    def finish(p, y):
        z, xs = p[0], p[1]
        y = (y + d_skip[:, None] * xs).reshape(z.shape[:2] + (SSM_W,))
        return rmsnorm(y * jax.nn.silu(z), norm_g).astype(z.dtype)

    pc, pl = prep(u_c, 1), prep(u_l, rows)
    s0 = jnp.zeros((u_l.shape[0], SSM_HEADS, SSM_HD, SSM_STATE), jnp.float32)
    yc, yl = bidir(ssd_chunked, scan_inputs(pc, 0), scan_inputs(pl, 0), scan_inputs(pc, 1), scan_inputs(pl, 1), s0, ctx_out)
    return (finish(pc, yc) if ctx_out else None), finish(pl, yl)


def gla_group(u_c, u_l, ga2, gb, norm_g, ctx_out):
    def heads(z):
        return z.reshape(z.shape[:2] + (GLA_HEADS, -1))

    def prep(u):
        q, k, v, gl, r = jnp.split(u, [GLA_KW, 2 * GLA_KW, 2 * GLA_KW + GLA_VW, 2 * GLA_KW + GLA_VW + GLA_GATE_RANK], -1)
        return heads(q) * GLA_DK ** -0.5, heads(k), heads(v), gl, r

    def scan_inputs(p, d):
        q, k, v, gl, _ = p
        logg = jax.nn.log_sigmoid((gl @ ga2[d] + gb[d]).astype(jnp.float32)) / GLA_TAU
        return (q, k, v, heads(logg))

    def finish(p, y):
        r = p[4]
        return (head_rmsnorm(y, norm_g) * jax.nn.silu(r)).astype(r.dtype)

    pc, pl = prep(u_c), prep(u_l)
    s0 = jnp.zeros((u_l.shape[0], GLA_HEADS, GLA_DK, GLA_DV), jnp.float32)
    yc, yl = bidir(gla_chunked, scan_inputs(pc, 0), scan_inputs(pl, 0), scan_inputs(pc, 1), scan_inputs(pl, 1), s0, ctx_out)
    return (finish(pc, yc) if ctx_out else None), finish(pl, yl)


def mixer(hc, hl, rows, w_in, w_out, rw_p, ssm_p, gla_p, ctx_out):
    cut = [RW_COLS, RW_COLS + SSM_COLS]
    rc, sc, gc = jnp.split(hc @ w_in, cut, -1)
    rl, sl, gl = jnp.split(hl @ w_in, cut, -1)
    a_c, a_l = rwkv_group(rc, rl, *rw_p, ctx_out)
    b_c, b_l = ssm_group(sc, sl, rows, *ssm_p, ctx_out)
    g_c, g_l = gla_group(gc, gl, *gla_p, ctx_out)
    out_l = jnp.concatenate([a_l, b_l, g_l], -1) @ w_out
    out_c = jnp.concatenate([a_c, b_c, g_c], -1) @ w_out if ctx_out else None
    return out_c, out_l


def hier_moe(h, rg_w, rg_b, re_w, re_b, w1, w3, w2):
    bsz, t, d = h.shape
    tok = h.reshape(bsz * t, d)
    g_logits = (tok @ rg_w + rg_b).astype(jnp.float32)
    g_prob = jax.nn.softmax(g_logits, -1)
    _, g_sel = lax.top_k(g_logits, 1)
    p_group = jnp.take_along_axis(g_prob, g_sel, 1)
    e_logits = (tok @ re_w + re_b).astype(jnp.float32).reshape(-1, N_GROUPS, EXPERTS_PER_GROUP)
    e_logits = jnp.take_along_axis(e_logits, g_sel[:, :, None], 1)[:, 0]
    top_p, top_i = lax.top_k(jax.nn.softmax(e_logits, -1), TOP_K)
    weight = p_group * top_p / jnp.sum(top_p, -1, keepdims=True)
    expert_id = g_sel * EXPERTS_PER_GROUP + top_i
    combine = jnp.sum(jax.nn.one_hot(expert_id, N_EXPERTS, dtype=jnp.float32) * weight[..., None], 1).astype(tok.dtype)
    out = jnp.zeros_like(tok)
    for e in range(N_EXPERTS):
        hid = jax.nn.silu(tok @ w1[e]) * (tok @ w3[e])
        out = out + combine[:, e:e + 1] * (hid @ w2[e])
    return out.reshape(bsz, t, d)


def setup_inputs(seed: int = 0) -> dict:
    key = jax.random.key(seed)
    ks = iter(jax.random.split(key, 64))
    nrm = lambda shape, s: jax.random.normal(next(ks), shape, jnp.float32) * s
    uni = lambda shape, lo, hi: jax.random.uniform(next(ks), shape, jnp.float32, lo, hi)
    L, D = DEPTH, D_MODEL
    dt = jnp.exp(uni((L, 2, SSM_HEADS), math.log(1e-3), math.log(1e-1)))
    return {
        'x': nrm((BATCH, SEQ, D), 1.0),
        'c': nrm((BATCH, D), 1.0),
        'ctx': nrm((BATCH, CTX_LEN, D), 1.0),
        'c_ctx': nrm((D,), 1.0),
        'ada_w': nrm((L, D, 6 * D), 0.5 * D ** -0.5),
        'ada_b': nrm((L, 6 * D), 0.02),
        'norm1_g': 1.0 + nrm((L, D), 0.1),
        'norm2_g': 1.0 + nrm((L, D), 0.1),
        'w_in': nrm((L, D, IN_COLS), D ** -0.5),
        'w_out': nrm((L, MIX_W, D), MIX_W ** -0.5),
        'rw_mu_prev': uni((L, RW_COLS), 0.0, 0.5),
        'rw_mu_next': uni((L, RW_COLS), 0.0, 0.5),
        'rw_w0': uni((L, 2, RW_W), -6.0, 1.0),
        'rw_w2': nrm((L, 2, RW_DECAY_RANK, RW_W), 0.1 * RW_DECAY_RANK ** -0.5),
        'rw_a0': nrm((L, 2, RW_W), 0.1),
        'rw_a2': nrm((L, 2, RW_ICLR_RANK, RW_W), 0.1 * RW_ICLR_RANK ** -0.5),
        'rw_g2': nrm((L, RW_GATE_RANK, RW_W), RW_GATE_RANK ** -0.5),
        'rw_k_k': 0.85 + nrm((L, RW_W), 0.05),
        'rw_k_a': 1.0 + nrm((L, RW_W), 0.05),
        'rw_r_k': nrm((L, RW_W), 0.1),
        'rw_gn_g': 1.0 + nrm((L, RW_W), 0.1),
        'rw_gn_b': nrm((L, RW_W), 0.02),
        'ssm_conv_w': nrm((L, SSM_CONV, SSM_CONV, SSM_CONV_CH), (SSM_CONV * SSM_CONV) ** -0.5),
        'ssm_conv_b': nrm((L, SSM_CONV_CH), 0.02),
        'ssm_dt_bias': dt + jnp.log(-jnp.expm1(-dt)),
        'ssm_a_log': jnp.log(uni((L, 2, SSM_HEADS), 1.0, 16.0)),
        'ssm_d': 1.0 + nrm((L, SSM_HEADS), 0.1),
        'ssm_norm_g': 1.0 + nrm((L, SSM_W), 0.1),
        'gla_ga2': nrm((L, 2, GLA_GATE_RANK, GLA_KW), GLA_GATE_RANK ** -0.5),
        'gla_gb': nrm((L, 2, GLA_KW), 0.5),
        'gla_norm_g': 1.0 + nrm((L, GLA_VW), 0.1),
        'moe_rg_w': nrm((L, D, N_GROUPS), D ** -0.5),
        'moe_rg_b': nrm((L, N_GROUPS), 0.01),
        'moe_re_w': nrm((L, D, N_EXPERTS), D ** -0.5),
        'moe_re_b': nrm((L, N_EXPERTS), 0.01),
        'moe_w1': nrm((L, N_EXPERTS, D, D_EXPERT), D ** -0.5),
        'moe_w3': nrm((L, N_EXPERTS, D, D_EXPERT), D ** -0.5),
        'moe_w2': nrm((L, N_EXPERTS, D_EXPERT, D), D_EXPERT ** -0.5),
        'final_g': 1.0 + nrm((D,), 0.1),
    }


def reference(x, c, ctx, c_ctx, ada_w, ada_b, norm1_g, norm2_g, w_in, w_out,
              rw_mu_prev, rw_mu_next, rw_w0, rw_w2, rw_a0, rw_a2, rw_g2, rw_k_k, rw_k_a, rw_r_k, rw_gn_g, rw_gn_b,
              ssm_conv_w, ssm_conv_b, ssm_dt_bias, ssm_a_log, ssm_d, ssm_norm_g,
              gla_ga2, gla_gb, gla_norm_g,
              moe_rg_w, moe_rg_b, moe_re_w, moe_re_b, moe_w1, moe_w3, moe_w2, final_g):
    rows = x.shape[1] // GRID_W
    cond_l = jax.nn.silu(c)
    cond_c = jax.nn.silu(c_ctx)[None]
    for l in range(DEPTH):
        ctx_out = l < DEPTH - 1
        mod_l = jnp.split(cond_l @ ada_w[l] + ada_b[l], 6, -1)
        mod_c = jnp.split(cond_c @ ada_w[l] + ada_b[l], 6, -1)
        hl = modulate(rmsnorm(x, norm1_g[l]), mod_l[0], mod_l[1])
        hc = modulate(rmsnorm(ctx, norm1_g[l]), mod_c[0], mod_c[1])
        rw_p = (rw_mu_prev[l], rw_mu_next[l], rw_w0[l], rw_w2[l], rw_a0[l], rw_a2[l], rw_g2[l],
                rw_k_k[l], rw_k_a[l], rw_r_k[l], rw_gn_g[l], rw_gn_b[l])
        ssm_p = (ssm_conv_w[l], ssm_conv_b[l], ssm_dt_bias[l], ssm_a_log[l], ssm_d[l], ssm_norm_g[l])
        gla_p = (gla_ga2[l], gla_gb[l], gla_norm_g[l])
        mc, ml = mixer(hc, hl, rows, w_in[l], w_out[l], rw_p, ssm_p, gla_p, ctx_out)
        moe_p = (moe_rg_w[l], moe_rg_b[l], moe_re_w[l], moe_re_b[l], moe_w1[l], moe_w3[l], moe_w2[l])
        x = x + mod_l[2][:, None] * ml
        x = x + mod_l[5][:, None] * hier_moe(modulate(rmsnorm(x, norm2_g[l]), mod_l[3], mod_l[4]), *moe_p)
        if ctx_out:
            ctx = ctx + mod_c[2][:, None] * mc
            ctx = ctx + mod_c[5][:, None] * hier_moe(modulate(rmsnorm(ctx, norm2_g[l]), mod_c[3], mod_c[4]), *moe_p)
    return rmsnorm(x, final_g)
```

```python
import functools

import jax
import jax.numpy as jnp
from jax import lax
from jax.experimental import pallas as pl
from jax.experimental.pallas import tpu as pltpu

F32 = jnp.float32
BF16 = jnp.bfloat16

D_MODEL = 1024
GRID_W = 64
NORM_EPS = 1e-6

RW_HEADS = 4
RW_HD = 64
RW_W = RW_HEADS * RW_HD
RW_DECAY_RANK = 64
RW_ICLR_RANK = 64
RW_GATE_RANK = 128
RW_GN_EPS = 64e-5
RW_COLS = 3 * RW_W + RW_DECAY_RANK + RW_ICLR_RANK + RW_GATE_RANK

SSM_HEADS = 8
SSM_HD = 64
SSM_W = SSM_HEADS * SSM_HD
SSM_GROUPS = 2
SSM_STATE = 64
SSM_CONV_CH = SSM_W + 2 * SSM_GROUPS * SSM_STATE
SSM_COLS = SSM_W + SSM_CONV_CH + SSM_HEADS

GLA_HEADS = 4
GLA_DK = 32
GLA_DV = 64
GLA_KW = GLA_HEADS * GLA_DK
GLA_VW = GLA_HEADS * GLA_DV
GLA_GATE_RANK = 16
GLA_TAU = 16.0
GLA_COLS = 2 * GLA_KW + GLA_VW + GLA_GATE_RANK + GLA_VW

N_GROUPS = 4
EXPERTS_PER_GROUP = 4
N_EXPERTS = N_GROUPS * EXPERTS_PER_GROUP
D_EXPERT = 512

LANES = 128
SMALL_W = LANES
VMEM_LIMIT = 56 * 1024 * 1024

RW_CHUNK = 64
SSD_CHUNK = 128
GLA_CHUNK = 64

_NT = (((1,), (1,)), ((), ()))
_TN = (((0,), (0,)), ((), ()))


def _dot(a, b):
    return jnp.dot(a, b, preferred_element_type=F32)


def _dot_nt(a, b):
    return lax.dot_general(a, b, _NT, preferred_element_type=F32)


def _dot_tn(a, b):
    return lax.dot_general(a, b, _TN, preferred_element_type=F32)


def _split3(x):
    hi = x.astype(BF16)
    r1 = x - hi.astype(F32)
    mid = r1.astype(BF16)
    lo = (r1 - mid.astype(F32)).astype(BF16)
    return hi, mid, lo


def _tri_masks(n):
    row = lax.broadcasted_iota(jnp.int32, (n, n), 0)
    col = lax.broadcasted_iota(jnp.int32, (n, n), 1)
    return col <= row, col < row


def _cumsum_rows(tri, x):
    hi, mid, lo = _split3(x)
    return _dot(tri, hi) + _dot(tri, mid) + _dot(tri, lo)


def _norm_mod(x, g, shift, scale):
    h = x * lax.rsqrt(jnp.mean(x * x, -1, keepdims=True) + NORM_EPS) * g
    return h * (1.0 + scale) + shift


def _in_proj_kernel(x_ref, g_ref, sh_ref, sc_ref, w_ref, *out_refs, widths):
    h = _norm_mod(x_ref[...], g_ref[...], sh_ref[0], sc_ref[0]).astype(BF16)
    off = 0
    for o_ref, wd in zip(out_refs, widths):
        o_ref[...] = _dot(h, w_ref[:, off:off + wd])
        off += wd


def _mod_spec(mod, tiles_per_batch):
    d = mod.shape[-1]
    if mod.shape[0] == 1:
        return pl.BlockSpec((1, 1, d), lambda i: (0, 0, 0))
    return pl.BlockSpec((1, 1, d), lambda i: (i // tiles_per_batch, 0, 0))


def in_proj(x, g, shift, scale, w, widths, *, tm):
    bsz, t, d = x.shape
    n = bsz * t
    tm = min(tm, t)
    assert t % tm == 0
    tpb = t // tm
    x2 = x.reshape(n, d)
    outs = pl.pallas_call(
        functools.partial(_in_proj_kernel, widths=widths),
        grid=(n // tm,),
        in_specs=[
            pl.BlockSpec((tm, d), lambda i: (i, 0)),
            pl.BlockSpec((1, d), lambda i: (0, 0)),
            _mod_spec(shift[:, None], tpb),
            _mod_spec(scale[:, None], tpb),
            pl.BlockSpec(w.shape, lambda i: (0, 0)),
        ],
        out_specs=[pl.BlockSpec((tm, wd), lambda i: (i, 0)) for wd in widths],
        out_shape=[jax.ShapeDtypeStruct((n, wd), F32) for wd in widths],
        compiler_params=pltpu.CompilerParams(dimension_semantics=("parallel",), vmem_limit_bytes=VMEM_LIMIT),
        name="in_proj",
    )(x2, g[None], shift[:, None], scale[:, None], w)
    return [o.reshape(bsz, t, wd) for o, wd in zip(outs, widths)]


def _out_proj_kernel(m_ref, w_ref, x_ref, gate_ref, o_ref):
    acc = _dot(m_ref[...].astype(BF16), w_ref[...])
    o_ref[...] = x_ref[...] + gate_ref[0] * acc


def out_proj(m, w, x, gate, *, tm):
    bsz, t, d = x.shape
    n = bsz * t
    tm = min(tm, t)
    assert t % tm == 0
    tpb = t // tm
    kdim = m.shape[-1]
    out = pl.pallas_call(
        _out_proj_kernel,
        grid=(n // tm,),
        in_specs=[
            pl.BlockSpec((tm, kdim), lambda i: (i, 0)),
            pl.BlockSpec(w.shape, lambda i: (0, 0)),
            pl.BlockSpec((tm, d), lambda i: (i, 0)),
            _mod_spec(gate[:, None], tpb),
        ],
        out_specs=pl.BlockSpec((tm, d), lambda i: (i, 0)),
        out_shape=jax.ShapeDtypeStruct((n, d), F32),
        compiler_params=pltpu.CompilerParams(dimension_semantics=("parallel",), vmem_limit_bytes=VMEM_LIMIT),
        name="out_proj",
    )(m.reshape(n, kdim), w, x.reshape(n, d), gate[:, None])
    return out.reshape(bsz, t, d)


def _route(h, rgw_ref, rgb_ref, rew_ref, reb_ref):
    hp = lax.Precision.HIGHEST
    gl = jnp.dot(h, rgw_ref[...], precision=hp, preferred_element_type=F32) + rgb_ref[...]
    el = jnp.dot(h, rew_ref[...], precision=hp, preferred_element_type=F32) + reb_ref[...]
    lane = lax.broadcasted_iota(jnp.int32, gl.shape, 1)
    valid = lane < N_EXPERTS
    neg = -jnp.inf
    big = jnp.int32(1 << 20)
    glm = jnp.where(valid, gl, neg)
    gmax = jnp.max(glm, -1, keepdims=True)
    g_sel_lane = jnp.min(jnp.where(glm == gmax, lane, big), -1, keepdims=True)
    in_group = (lane // EXPERTS_PER_GROUP) == (g_sel_lane // EXPERTS_PER_GROUP)
    gsum = jnp.sum(jnp.where(valid, jnp.exp(glm - gmax), 0.0), -1, keepdims=True) / EXPERTS_PER_GROUP
    p_group = 1.0 / gsum
    elm = jnp.where(in_group & valid, el, neg)
    m1 = jnp.max(elm, -1, keepdims=True)
    i1 = jnp.min(jnp.where(elm == m1, lane, big), -1, keepdims=True)
    elm2 = jnp.where(lane == i1, neg, elm)
    m2 = jnp.max(elm2, -1, keepdims=True)
    i2 = jnp.min(jnp.where(elm2 == m2, lane, big), -1, keepdims=True)
    p2 = jnp.exp(m2 - m1)
    wa = p_group / (1.0 + p2)
    wb = p_group * p2 / (1.0 + p2)
    return jnp.where(lane == i1, wa, 0.0) + jnp.where(lane == i2, wb, 0.0)


def _moe_kernel(x_ref, g_ref, sh_ref, sc_ref, gate_ref, rgw_ref, rgb_ref, rew_ref, reb_ref,
                w1_ref, w3_ref, w2_ref, o_ref, h_ref, comb_ref, acc_ref):
    e = pl.program_id(1)

    @pl.when(e == 0)
    def _():
        h = _norm_mod(x_ref[...], g_ref[...], sh_ref[0], sc_ref[0])
        h_ref[...] = h.astype(BF16)
        comb_ref[...] = _route(h, rgw_ref, rgb_ref, rew_ref, reb_ref)
        acc_ref[...] = jnp.zeros_like(acc_ref)

    hb = h_ref[...]
    a = _dot(hb, w1_ref[0])
    b = _dot(hb, w3_ref[0])
    hid = (a * jax.nn.sigmoid(a) * b).astype(BF16)
    y = _dot(hid, w2_ref[0])
    comb = comb_ref[...]
    lane = lax.broadcasted_iota(jnp.int32, comb.shape, 1)
    col = jnp.sum(jnp.where(lane == e, comb, 0.0), -1, keepdims=True)
    acc_ref[...] += col * y

    @pl.when(e == N_EXPERTS - 1)
    def _():
        o_ref[...] = x_ref[...] + gate_ref[0] * acc_ref[...]


def moe_block(x, g, shift, scale, gate, rgw, rgb, rew, reb, w1, w3, w2, *, tm):
    bsz, t, d = x.shape
    n = bsz * t
    tm = min(tm, t)
    assert t % tm == 0
    tpb = t // tm

    def mod_spec(mod):
        if mod.shape[0] == 1:
            return pl.BlockSpec((1, 1, d), lambda i, e: (0, 0, 0))
        return pl.BlockSpec((1, 1, d), lambda i, e: (i // tpb, 0, 0))

    const = lambda shape: pl.BlockSpec(shape, lambda i, e: (0,) * len(shape))
    out = pl.pallas_call(
        _moe_kernel,
        grid=(n // tm, N_EXPERTS),
        in_specs=[
            pl.BlockSpec((tm, d), lambda i, e: (i, 0)),
            const((1, d)),
            mod_spec(shift[:, None]), mod_spec(scale[:, None]), mod_spec(gate[:, None]),
            const(rgw.shape), const(rgb.shape), const(rew.shape), const(reb.shape),
            pl.BlockSpec((1, d, D_EXPERT), lambda i, e: (e, 0, 0)),
            pl.BlockSpec((1, d, D_EXPERT), lambda i, e: (e, 0, 0)),
            pl.BlockSpec((1, D_EXPERT, d), lambda i, e: (e, 0, 0)),
        ],
        out_specs=pl.BlockSpec((tm, d), lambda i, e: (i, 0)),
        out_shape=jax.ShapeDtypeStruct((n, d), F32),
        scratch_shapes=[pltpu.VMEM((tm, d), BF16), pltpu.VMEM((tm, LANES), F32), pltpu.VMEM((tm, d), F32)],
        compiler_params=pltpu.CompilerParams(dimension_semantics=("parallel", "arbitrary"),
                                             vmem_limit_bytes=VMEM_LIMIT),
        name="moe",
    )(x.reshape(n, d), g[None], shift[:, None], scale[:, None], gate[:, None], rgw, rgb, rew, reb, w1, w3, w2)
    return out.reshape(bsz, t, d)


def _rwkv_kernel(r_ref, lw_ref, k_ref, v_ref, a_ref, b_ref, y_ref, st_ref, *, nsub):
    L = RW_CHUNK
    hd = RW_HD

    @pl.when(pl.program_id(1) == 0)
    def _():
        st_ref[...] = jnp.zeros_like(st_ref)

    incl, strict = _tri_masks(L)
    tri = jnp.where(incl, 1.0, 0.0).astype(BF16)
    zeros = jnp.zeros((L, hd), F32)
    for s in range(nsub):
        rows = pl.ds(s * L, L)
        lw = lw_ref[0, rows, :]
        c = _cumsum_rows(tri, lw)
        c_end = c[L - 1:L, :]
        e_inv = jnp.exp(-c)
        e_end = jnp.exp(c_end - c)
        bv = b_ref[0, rows, :]
        kv = k_ref[0, rows, :]
        at_all = a_ref[0, rows, :] * jnp.exp(c - lw)
        rt_all = r_ref[0, rows, :] * jnp.exp(c)
        bt_all = bv * e_inv
        kt_all = kv * e_inv
        bh_all = bv * e_end
        kh_all = kv * e_end
        p_end = jnp.exp(c_end)
        v_all = v_ref[0, rows, :]
        ys = []
        for h in range(RW_HEADS):
            hs = slice(h * hd, (h + 1) * hd)
            at, rt, vh = at_all[:, hs], rt_all[:, hs], v_all[:, hs]
            bh, kh = bh_all[:, hs], kh_all[:, hs]
            lhs = jnp.concatenate([at, rt], 0).astype(BF16)
            rhs = jnp.concatenate([bt_all[:, hs], kt_all[:, hs]], 0).astype(BF16)
            sc = _dot_nt(lhs, rhs)
            mab = jnp.where(strict, sc[:L, :L], 0.0)
            mak = jnp.where(strict, sc[:L, L:], 0.0)
            mrb = jnp.where(incl, sc[L:, :L], 0.0)
            mrk = jnp.where(incl, sc[L:, L:], 0.0)
            vb = vh.astype(BF16)
            x = jnp.concatenate([at, _dot(mak.astype(BF16), vb)], 1)
            nmat = mab
            steps = L.bit_length() - 1
            for i in range(steps):
                nb = nmat.astype(BF16)
                x = x + _dot(nb, x.astype(BF16))
                if i + 1 < steps:
                    nmat = _dot(nb, nb)
            w, u0 = x[:, :hd], x[:, hd:]
            z = jnp.concatenate([x, jnp.concatenate([zeros, vh], 1)], 0).astype(BF16)
            ftop = _dot(jnp.concatenate([mrb, mrk], 1).astype(BF16), z)
            q = rt + ftop[:, :hd]
            y0 = ftop[:, hd:]
            bhw = _dot_tn(bh.astype(BF16), w.astype(BF16))
            gt = _dot_tn(jnp.concatenate([u0, vh], 0).astype(BF16),
                         jnp.concatenate([bh, kh], 0).astype(BF16))
            st = st_ref[h]
            stb = st.astype(BF16)
            ys.append(_dot_nt(q.astype(BF16), stb) + y0)
            st_ref[h] = st * p_end[:, hs] + _dot_nt(stb, bhw.astype(BF16)) + gt
        y_ref[0, rows, :] = jnp.concatenate(ys, 1)


def rwkv_scan(r, lw, k, v, a, b, *, nsub=2):
    s, t, w = r.shape
    blk = RW_CHUNK * nsub
    assert t % blk == 0
    spec = pl.BlockSpec((1, blk, w), lambda i, c: (i, c, 0))
    return pl.pallas_call(
        functools.partial(_rwkv_kernel, nsub=nsub),
        grid=(s, t // blk),
        in_specs=[spec] * 6,
        out_specs=spec,
        out_shape=jax.ShapeDtypeStruct((s, t, w), F32),
        scratch_shapes=[pltpu.VMEM((RW_HEADS, RW_HD, RW_HD), F32)],
        compiler_params=pltpu.CompilerParams(dimension_semantics=("parallel", "arbitrary"),
                                             vmem_limit_bytes=VMEM_LIMIT),
        name="rwkv_scan",
    )(r, lw, k, v, a, b)


def _ssd_kernel(x_ref, da_ref, b_ref, c_ref, y_ref, st_ref):
    L = SSD_CHUNK
    hd = SSM_HD
    ns = SSM_STATE
    per_group = SSM_HEADS // SSM_GROUPS

    @pl.when(pl.program_id(1) == 0)
    def _():
        st_ref[...] = jnp.zeros_like(st_ref)

    incl, _ = _tri_masks(L)
    tri = jnp.where(incl, 1.0, 0.0).astype(BF16)
    parts = _split3(da_ref[0])
    acs_row = sum(_dot_nt(p, tri) for p in parts)
    acs_col = sum(_dot_nt(tri, p) for p in parts)
    xs = x_ref[0]
    ys = []
    for g in range(SSM_GROUPS):
        gs = slice(g * ns, (g + 1) * ns)
        bg = b_ref[0, :, gs]
        cg = c_ref[0, :, gs]
        cb = _dot_nt(cg.astype(BF16), bg.astype(BF16))
        for hh in range(per_group):
            h = g * per_group + hh
            col = acs_col[:, h:h + 1]
            row = acs_row[h:h + 1, :]
            a_end = acs_col[L - 1:L, h:h + 1]
            dec = jnp.exp(jnp.where(incl, col - row, -jnp.inf))
            xh = xs[:, h * hd:(h + 1) * hd].astype(BF16)
            st = st_ref[h]
            y_diag = _dot((cb * dec).astype(BF16), xh)
            y_off = _dot((cg * jnp.exp(col)).astype(BF16), st.astype(BF16))
            new = _dot_tn((bg * jnp.exp(a_end - col)).astype(BF16), xh)
            st_ref[h] = jnp.exp(a_end) * st + new
            ys.append(y_diag + y_off)
    y_ref[0] = jnp.concatenate(ys, 1)


def ssd_scan(x, da_t, bm, cm):
    s, t, w = x.shape
    L = SSD_CHUNK
    assert t % L == 0
    gw = bm.shape[-1]
    return pl.pallas_call(
        _ssd_kernel,
        grid=(s, t // L),
        in_specs=[
            pl.BlockSpec((1, L, w), lambda i, c: (i, c, 0)),
            pl.BlockSpec((1, SSM_HEADS, L), lambda i, c: (i, 0, c)),
            pl.BlockSpec((1, L, gw), lambda i, c: (i, c, 0)),
            pl.BlockSpec((1, L, gw), lambda i, c: (i, c, 0)),
        ],
        out_specs=pl.BlockSpec((1, L, w), lambda i, c: (i, c, 0)),
        out_shape=jax.ShapeDtypeStruct((s, t, w), F32),
        scratch_shapes=[pltpu.VMEM((SSM_HEADS, SSM_STATE, SSM_HD), F32)],
        compiler_params=pltpu.CompilerParams(dimension_semantics=("parallel", "arbitrary"),
                                             vmem_limit_bytes=VMEM_LIMIT),
        name="ssd_scan",
    )(x, da_t, bm, cm)


def _gla_kernel(q_ref, k_ref, v_ref, lg_ref, y_ref, st_ref):
    L = GLA_CHUNK

    @pl.when(pl.program_id(1) == 0)
    def _():
        st_ref[...] = jnp.zeros_like(st_ref)

    incl, _ = _tri_masks(L)
    tri = jnp.where(incl, 1.0, 0.0).astype(BF16)
    bcs = _cumsum_rows(tri, lg_ref[0])
    mid = bcs[L // 2:L // 2 + 1, :]
    bend = bcs[L - 1:L, :]
    q = q_ref[0]
    k = k_ref[0]
    v = v_ref[0]
    qe = (q * jnp.exp(bcs - mid)).astype(BF16)
    ke = (k * jnp.exp(mid - bcs)).astype(BF16)
    qd = (q * jnp.exp(bcs)).astype(BF16)
    kd = (k * jnp.exp(bend - bcs)).astype(BF16)
    eb = jnp.exp(bend)
    ys = []
    for h in range(GLA_HEADS):
        ks = slice(h * GLA_DK, (h + 1) * GLA_DK)
        vh = v[:, h * GLA_DV:(h + 1) * GLA_DV].astype(BF16)
        att = jnp.where(incl, _dot_nt(qe[:, ks], ke[:, ks]), 0.0)
        o_intra = _dot(att.astype(BF16), vh)
        st = st_ref[h]
        o_inter = _dot_nt(qd[:, ks], st.astype(BF16))
        st_ref[h] = st * eb[:, ks] + _dot_tn(vh, kd[:, ks])
        ys.append(o_intra + o_inter)
    y_ref[0] = jnp.concatenate(ys, 1)


def gla_scan(q, k, v, lg):
    s, t, kw = q.shape
    vw = v.shape[-1]
    L = GLA_CHUNK
    assert t % L == 0
    kspec = pl.BlockSpec((1, L, kw), lambda i, c: (i, c, 0))
    vspec = pl.BlockSpec((1, L, vw), lambda i, c: (i, c, 0))
    return pl.pallas_call(
        _gla_kernel,
        grid=(s, t // L),
        in_specs=[kspec, kspec, vspec, kspec],
        out_specs=vspec,
        out_shape=jax.ShapeDtypeStruct((s, t, vw), F32),
        scratch_shapes=[pltpu.VMEM((GLA_HEADS, GLA_DV, GLA_DK), F32)],
        compiler_params=pltpu.CompilerParams(dimension_semantics=("parallel", "arbitrary"),
                                             vmem_limit_bytes=VMEM_LIMIT),
        name="gla_scan",
    )(q, k, v, lg)


_HP = lax.Precision.HIGHEST


def _scan_order(zc, zl, d):
    if d == 1:
        zc, zl = jnp.flip(zc, 1), jnp.flip(zl, 1)
    return jnp.concatenate([zc, zl], 1)


def _bidir(scan_fn, inputs_c, inputs_l):
    tc = inputs_c[0][0].shape[1]
    nb = inputs_l[0][0].shape[0]
    stacked = []
    for j in range(len(inputs_l[0])):
        stacked.append(jnp.concatenate([_scan_order(inputs_c[d][j], inputs_l[d][j], d) for d in (0, 1)], 0))
    y = scan_fn(*stacked)
    yf, yb = y[:nb], y[nb:]
    yl = yf[:, tc:] + jnp.flip(yb[:, tc:], 1)
    yc = yf[:, :tc] + jnp.flip(yb[:, :tc], 1)
    return yc, yl


def _token_shift(u, mu_prev, mu_next):
    pad = jnp.zeros_like(u[:, :1])
    prev = jnp.concatenate([pad, u[:, :-1]], 1)
    nxt = jnp.concatenate([u[:, 1:], pad], 1)
    return u + mu_prev * (prev - u) + mu_next * (nxt - u)


def _rw_heads(z):
    return z.reshape(z.shape[:2] + (RW_HEADS, RW_HD))


def _rwkv_group(u_c, u_l, mu_prev, mu_next, w0, w2, a0, a2, g2, k_k, k_a, r_k, gn_g, gn_b, ctx_out):
    splits = [RW_W, 2 * RW_W, 3 * RW_W, 3 * RW_W + RW_DECAY_RANK, 3 * RW_W + RW_DECAY_RANK + RW_ICLR_RANK]

    def prep(u):
        r, k, v, wl, al, gl = jnp.split(_token_shift(u, mu_prev, mu_next), splits, -1)
        return r, k, v, jnp.tanh(wl), al, gl

    def scan_inputs(p, d):
        r, k, v, wl, al, _ = p
        w_log = -jax.nn.softplus(-(w0[d] + jnp.dot(wl, w2[d], precision=_HP))) - 0.5
        log_decay = -jnp.exp(w_log)
        a = jax.nn.sigmoid(a0[d] + jnp.dot(al, a2[d], precision=_HP))
        kk = _rw_heads(k * k_k)
        kk = (kk * lax.rsqrt(jnp.sum(kk * kk, -1, keepdims=True) + 1e-12)).reshape(k.shape)
        k_mod = k * (1 + (a - 1) * k_a)
        return (r, log_decay, k_mod, v, -kk, kk * a)

    def finish(p, y):
        r, k, v, _, _, gl = p
        bonus = jnp.sum(_rw_heads(r * k * r_k), -1, keepdims=True) * _rw_heads(v)
        yh = _rw_heads(y)
        mu = jnp.mean(yh, -1, keepdims=True)
        var = jnp.mean(jnp.square(yh - mu), -1, keepdims=True)
        yn = ((yh - mu) * lax.rsqrt(var + RW_GN_EPS)).reshape(y.shape) * gn_g + gn_b
        out = yn + bonus.reshape(r.shape)
        return out * jnp.dot(jax.nn.sigmoid(gl), g2, precision=_HP)

    pc, pl_ = prep(u_c), prep(u_l)
    yc, yl = _bidir(rwkv_scan, [scan_inputs(pc, d) for d in (0, 1)], [scan_inputs(pl_, d) for d in (0, 1)])
    return (finish(pc, yc) if ctx_out else None), finish(pl_, yl)


def _dwconv_grid(u, w, b, rows):
    bsz, n, ch = u.shape
    img = u.reshape(bsz, rows, n // rows, ch)
    out = lax.conv_general_dilated(img, w[:, :, None, :], (1, 1), 'SAME',
                                   dimension_numbers=('NHWC', 'HWIO', 'NHWC'), feature_group_count=ch,
                                   precision=_HP)
    return out.reshape(bsz, n, ch) + b


def _ssm_group(zc, xbc_c, dt_c, zl, xbc_l, dt_l, rows, conv_w, conv_b, dt_bias, a_log, d_skip, norm_g, ctx_out):
    def prep(z, xbc, dt, grid_rows):
        xbc = jax.nn.silu(_dwconv_grid(xbc, conv_w, conv_b, grid_rows))
        xs, bm, cm = jnp.split(xbc, [SSM_W, SSM_W + SSM_GROUPS * SSM_STATE], -1)
        return z, xs, bm, cm, dt

    def scan_inputs(p, d):
        _, xs, bm, cm, dt = p
        dtp = jax.nn.softplus(dt + dt_bias[d])
        da = -jnp.exp(a_log[d]) * dtp
        xdt = (xs.reshape(xs.shape[:2] + (SSM_HEADS, SSM_HD)) * dtp[..., None]).reshape(xs.shape)
        return (xdt, da, bm, cm)

    def scan_fn(x, da, bm, cm):
        return ssd_scan(x, jnp.swapaxes(da, 1, 2), bm, cm)

    def finish(p, y):
        z, xs = p[0], p[1]
        y = y + jnp.repeat(d_skip, SSM_HD) * xs
        return rmsnorm_f32(y * jax.nn.silu(z), norm_g)

    pc, pl_ = prep(zc, xbc_c, dt_c, 1), prep(zl, xbc_l, dt_l, rows)
    yc, yl = _bidir(scan_fn, [scan_inputs(pc, d) for d in (0, 1)], [scan_inputs(pl_, d) for d in (0, 1)])
    return (finish(pc, yc) if ctx_out else None), finish(pl_, yl)


def rmsnorm_f32(h, g):
    return h * lax.rsqrt(jnp.mean(h * h, -1, keepdims=True) + NORM_EPS) * g


def _gla_group(u_c, gl_c, u_l, gl_l, ga2, gb, norm_g, ctx_out):
    def prep(u, gl):
        q, k, v, r = jnp.split(u, [GLA_KW, 2 * GLA_KW, 2 * GLA_KW + GLA_VW], -1)
        return q * GLA_DK ** -0.5, k, v, gl, r

    def scan_inputs(p, d):
        q, k, v, gl, _ = p
        logg = jax.nn.log_sigmoid(jnp.dot(gl, ga2[d], precision=_HP) + gb[d]) / GLA_TAU
        return (q, k, v, logg)

    def finish(p, y):
        r = p[4]
        yh = y.reshape(y.shape[:2] + (GLA_HEADS, GLA_DV))
        yn = (yh * lax.rsqrt(jnp.mean(yh * yh, -1, keepdims=True) + NORM_EPS)).reshape(y.shape) * norm_g
        return yn * jax.nn.silu(r)

    pc, pl_ = prep(u_c, gl_c), prep(u_l, gl_l)
    yc, yl = _bidir(gla_scan, [scan_inputs(pc, d) for d in (0, 1)], [scan_inputs(pl_, d) for d in (0, 1)])
    return (finish(pc, yc) if ctx_out else None), finish(pl_, yl)


_IN_WIDTHS = (RW_COLS, SSM_W, SSM_CONV_CH, 2 * GLA_KW + 2 * GLA_VW, SMALL_W)


def _arrange_w_in(w_in):
    rw, ssm, gla = jnp.split(w_in, [RW_COLS, RW_COLS + SSM_COLS], -1)
    z, xbc, dt = jnp.split(ssm, [SSM_W, SSM_W + SSM_CONV_CH], -1)
    qkv, gl, r = jnp.split(gla, [2 * GLA_KW + GLA_VW, 2 * GLA_KW + GLA_VW + GLA_GATE_RANK], -1)
    pad = jnp.zeros((w_in.shape[0], SMALL_W - SSM_HEADS - GLA_GATE_RANK), w_in.dtype)
    return jnp.concatenate([rw, z, xbc, qkv, r, dt, gl, pad], -1).astype(BF16)


def _pad_lanes(w, reps=1):
    w = jnp.repeat(w, reps, axis=-1) if reps > 1 else w
    return jnp.pad(w, [(0, 0)] * (w.ndim - 1) + [(0, LANES - w.shape[-1])])


def kernel(x, c, ctx, c_ctx, ada_w, ada_b, norm1_g, norm2_g, w_in, w_out, rw_mu_prev, rw_mu_next, rw_w0, rw_w2, rw_a0, rw_a2, rw_g2, rw_k_k, rw_k_a, rw_r_k, rw_gn_g, rw_gn_b, ssm_conv_w, ssm_conv_b, ssm_dt_bias, ssm_a_log, ssm_d, ssm_norm_g, gla_ga2, gla_gb, gla_norm_g, moe_rg_w, moe_rg_b, moe_re_w, moe_re_b, moe_w1, moe_w3, moe_w2, final_g):
    depth = ada_w.shape[0]
    rows = x.shape[1] // GRID_W
    cond_l = jax.nn.silu(c)
    cond_c = jax.nn.silu(c_ctx)[None]
    tm = 512
    for l in range(depth):
        ctx_out = l < depth - 1
        mod_l = jnp.split(jnp.dot(cond_l, ada_w[l], precision=_HP) + ada_b[l], 6, -1)
        mod_c = jnp.split(jnp.dot(cond_c, ada_w[l], precision=_HP) + ada_b[l], 6, -1)
        w_in_l = _arrange_w_in(w_in[l])
        rl, zl, xbcl, gql, sml = in_proj(x, norm1_g[l], mod_l[0], mod_l[1], w_in_l, _IN_WIDTHS, tm=tm)
        rc, zc, xbcc, gqc, smc = in_proj(ctx, norm1_g[l], mod_c[0], mod_c[1], w_in_l, _IN_WIDTHS, tm=tm)
        dtl, gll = sml[..., :SSM_HEADS], sml[..., SSM_HEADS:SSM_HEADS + GLA_GATE_RANK]
        dtc, glc = smc[..., :SSM_HEADS], smc[..., SSM_HEADS:SSM_HEADS + GLA_GATE_RANK]
        a_c, a_l = _rwkv_group(rc, rl, rw_mu_prev[l], rw_mu_next[l], rw_w0[l], rw_w2[l], rw_a0[l], rw_a2[l],
                               rw_g2[l], rw_k_k[l], rw_k_a[l], rw_r_k[l], rw_gn_g[l], rw_gn_b[l], ctx_out)
        b_c, b_l = _ssm_group(zc, xbcc, dtc, zl, xbcl, dtl, rows, ssm_conv_w[l], ssm_conv_b[l], ssm_dt_bias[l],
                              ssm_a_log[l], ssm_d[l], ssm_norm_g[l], ctx_out)
        g_c, g_l = _gla_group(gqc, glc, gql, gll, gla_ga2[l], gla_gb[l], gla_norm_g[l], ctx_out)
        w_out_l = w_out[l].astype(BF16)
        rgw = _pad_lanes(moe_rg_w[l], EXPERTS_PER_GROUP)
        rgb = _pad_lanes(moe_rg_b[l][None], EXPERTS_PER_GROUP)
        rew = _pad_lanes(moe_re_w[l])
        reb = _pad_lanes(moe_re_b[l][None])
        w1, w3, w2 = moe_w1[l].astype(BF16), moe_w3[l].astype(BF16), moe_w2[l].astype(BF16)
        x = out_proj(jnp.concatenate([a_l, b_l, g_l], -1), w_out_l, x, mod_l[2], tm=tm)
        x = moe_block(x, norm2_g[l], mod_l[3], mod_l[4], mod_l[5], rgw, rgb, rew, reb, w1, w3, w2, tm=tm)
        if ctx_out:
            ctx = out_proj(jnp.concatenate([a_c, b_c, g_c], -1), w_out_l, ctx, mod_c[2], tm=tm)
            ctx = moe_block(ctx, norm2_g[l], mod_c[3], mod_c[4], mod_c[5], rgw, rgb, rew, reb, w1, w3, w2, tm=tm)
    return rmsnorm_f32(x, final_g)
```

```python
import functools

import jax
import jax.numpy as jnp
from jax import lax
from jax.experimental import pallas as pl
from jax.experimental.pallas import tpu as pltpu

F32 = jnp.float32
BF16 = jnp.bfloat16

D_MODEL = 1024
GRID_W = 64
NORM_EPS = 1e-6

RW_HEADS = 4
RW_HD = 64
RW_W = RW_HEADS * RW_HD
RW_DECAY_RANK = 64
RW_ICLR_RANK = 64
RW_GATE_RANK = 128
RW_GN_EPS = 64e-5
RW_COLS = 3 * RW_W + RW_DECAY_RANK + RW_ICLR_RANK + RW_GATE_RANK

SSM_HEADS = 8
SSM_HD = 64
SSM_W = SSM_HEADS * SSM_HD
SSM_GROUPS = 2
SSM_STATE = 64
SSM_CONV_CH = SSM_W + 2 * SSM_GROUPS * SSM_STATE
SSM_COLS = SSM_W + SSM_CONV_CH + SSM_HEADS

GLA_HEADS = 4
GLA_DK = 32
GLA_DV = 64
GLA_KW = GLA_HEADS * GLA_DK
GLA_VW = GLA_HEADS * GLA_DV
GLA_GATE_RANK = 16
GLA_TAU = 16.0
GLA_COLS = 2 * GLA_KW + GLA_VW + GLA_GATE_RANK + GLA_VW

N_GROUPS = 4
EXPERTS_PER_GROUP = 4
N_EXPERTS = N_GROUPS * EXPERTS_PER_GROUP
D_EXPERT = 512

LANES = 128
SMALL_W = LANES
VMEM_LIMIT = 56 * 1024 * 1024

RW_CHUNK = 64
SSD_CHUNK = 128
GLA_CHUNK = 64

_NT = (((1,), (1,)), ((), ()))
_TN = (((0,), (0,)), ((), ()))


def _dot(a, b):
    return jnp.dot(a, b, preferred_element_type=F32)


def _dot_nt(a, b):
    return lax.dot_general(a, b, _NT, preferred_element_type=F32)


def _dot_tn(a, b):
    return lax.dot_general(a, b, _TN, preferred_element_type=F32)


def _split3(x):
    hi = x.astype(BF16)
    r1 = x - hi.astype(F32)
    mid = r1.astype(BF16)
    lo = (r1 - mid.astype(F32)).astype(BF16)
    return hi, mid, lo


def _cumsum_rows(tri, x):
    hi, mid, lo = _split3(x)
    return _dot(tri, hi) + _dot(tri, mid) + _dot(tri, lo)


def _norm_mod(x, g, shift, scale):
    h = x * lax.rsqrt(jnp.mean(x * x, -1, keepdims=True) + NORM_EPS) * g
    return h * (1.0 + scale) + shift


def _in_proj_kernel(x_ref, g_ref, sh_ref, sc_ref, w_ref, *out_refs, widths):
    h = _norm_mod(x_ref[...], g_ref[...], sh_ref[0], sc_ref[0]).astype(BF16)
    off = 0
    for o_ref, wd in zip(out_refs, widths):
        o_ref[...] = _dot(h, w_ref[:, off:off + wd])
        off += wd


def _mod_spec(mod, tiles_per_batch):
    d = mod.shape[-1]
    if mod.shape[0] == 1:
        return pl.BlockSpec((1, 1, d), lambda i: (0, 0, 0))
    return pl.BlockSpec((1, 1, d), lambda i: (i // tiles_per_batch, 0, 0))


def in_proj(x, g, shift, scale, w, widths, *, tm):
    bsz, t, d = x.shape
    n = bsz * t
    tm = min(tm, t)
    assert t % tm == 0
    tpb = t // tm
    x2 = x.reshape(n, d)
    outs = pl.pallas_call(
        functools.partial(_in_proj_kernel, widths=widths),
        grid=(n // tm,),
        in_specs=[
            pl.BlockSpec((tm, d), lambda i: (i, 0)),
            pl.BlockSpec((1, d), lambda i: (0, 0)),
            _mod_spec(shift[:, None], tpb),
            _mod_spec(scale[:, None], tpb),
            pl.BlockSpec(w.shape, lambda i: (0, 0)),
        ],
        out_specs=[pl.BlockSpec((tm, wd), lambda i: (i, 0)) for wd in widths],
        out_shape=[jax.ShapeDtypeStruct((n, wd), F32) for wd in widths],
        compiler_params=pltpu.CompilerParams(dimension_semantics=("parallel",), vmem_limit_bytes=VMEM_LIMIT),
        name="in_proj",
    )(x2, g[None], shift[:, None], scale[:, None], w)
    return [o.reshape(bsz, t, wd) for o, wd in zip(outs, widths)]


def _out_proj_kernel(m_ref, w_ref, x_ref, gate_ref, o_ref):
    acc = _dot(m_ref[...].astype(BF16), w_ref[...])
    o_ref[...] = x_ref[...] + gate_ref[0] * acc


def out_proj(m, w, x, gate, *, tm):
    bsz, t, d = x.shape
    n = bsz * t
    tm = min(tm, t)
    assert t % tm == 0
    tpb = t // tm
    kdim = m.shape[-1]
    out = pl.pallas_call(
        _out_proj_kernel,
        grid=(n // tm,),
        in_specs=[
            pl.BlockSpec((tm, kdim), lambda i: (i, 0)),
            pl.BlockSpec(w.shape, lambda i: (0, 0)),
            pl.BlockSpec((tm, d), lambda i: (i, 0)),
            _mod_spec(gate[:, None], tpb),
        ],
        out_specs=pl.BlockSpec((tm, d), lambda i: (i, 0)),
        out_shape=jax.ShapeDtypeStruct((n, d), F32),
        compiler_params=pltpu.CompilerParams(dimension_semantics=("parallel",), vmem_limit_bytes=VMEM_LIMIT),
        name="out_proj",
    )(m.reshape(n, kdim), w, x.reshape(n, d), gate[:, None])
    return out.reshape(bsz, t, d)


def _route(h, rgw_ref, rgb_ref, rew_ref, reb_ref):
    hp = lax.Precision.HIGHEST
    gl = jnp.dot(h, rgw_ref[...], precision=hp, preferred_element_type=F32) + rgb_ref[...]
    el = jnp.dot(h, rew_ref[...], precision=hp, preferred_element_type=F32) + reb_ref[...]
    lane = lax.broadcasted_iota(jnp.int32, gl.shape, 1)
    valid = lane < N_EXPERTS
    neg = -jnp.inf
    big = jnp.int32(1 << 20)
    glm = jnp.where(valid, gl, neg)
    gmax = jnp.max(glm, -1, keepdims=True)
    g_sel_lane = jnp.min(jnp.where(glm == gmax, lane, big), -1, keepdims=True)
    in_group = (lane // EXPERTS_PER_GROUP) == (g_sel_lane // EXPERTS_PER_GROUP)
    gsum = jnp.sum(jnp.where(valid, jnp.exp(glm - gmax), 0.0), -1, keepdims=True) / EXPERTS_PER_GROUP
    p_group = 1.0 / gsum
    elm = jnp.where(in_group & valid, el, neg)
    m1 = jnp.max(elm, -1, keepdims=True)
    i1 = jnp.min(jnp.where(elm == m1, lane, big), -1, keepdims=True)
    elm2 = jnp.where(lane == i1, neg, elm)
    m2 = jnp.max(elm2, -1, keepdims=True)
    i2 = jnp.min(jnp.where(elm2 == m2, lane, big), -1, keepdims=True)
    p2 = jnp.exp(m2 - m1)
    wa = p_group / (1.0 + p2)
    wb = p_group * p2 / (1.0 + p2)
    return jnp.where(lane == i1, wa, 0.0) + jnp.where(lane == i2, wb, 0.0)


def _moe_kernel(x_ref, g_ref, sh_ref, sc_ref, gate_ref, rgw_ref, rgb_ref, rew_ref, reb_ref,
                w1_ref, w3_ref, w2_ref, o_ref, h_ref, comb_ref, acc_ref):
    e = pl.program_id(1)

    @pl.when(e == 0)
    def _():
        h = _norm_mod(x_ref[...], g_ref[...], sh_ref[0], sc_ref[0])
        h_ref[...] = h.astype(BF16)
        comb_ref[...] = _route(h, rgw_ref, rgb_ref, rew_ref, reb_ref)
        acc_ref[...] = jnp.zeros_like(acc_ref)

    hb = h_ref[...]
    a = _dot(hb, w1_ref[0])
    b = _dot(hb, w3_ref[0])
    hid = (a * jax.nn.sigmoid(a) * b).astype(BF16)
    y = _dot(hid, w2_ref[0])
    comb = comb_ref[...]
    lane = lax.broadcasted_iota(jnp.int32, comb.shape, 1)
    col = jnp.sum(jnp.where(lane == e, comb, 0.0), -1, keepdims=True)
    acc_ref[...] += col * y

    @pl.when(e == N_EXPERTS - 1)
    def _():
        o_ref[...] = x_ref[...] + gate_ref[0] * acc_ref[...]


def moe_block(x, g, shift, scale, gate, rgw, rgb, rew, reb, w1, w3, w2, *, tm):
    bsz, t, d = x.shape
    n = bsz * t
    tm = min(tm, t)
    assert t % tm == 0
    tpb = t // tm

    def mod_spec(mod):
        if mod.shape[0] == 1:
            return pl.BlockSpec((1, 1, d), lambda i, e: (0, 0, 0))
        return pl.BlockSpec((1, 1, d), lambda i, e: (i // tpb, 0, 0))

    const = lambda shape: pl.BlockSpec(shape, lambda i, e: (0,) * len(shape))
    out = pl.pallas_call(
        _moe_kernel,
        grid=(n // tm, N_EXPERTS),
        in_specs=[
            pl.BlockSpec((tm, d), lambda i, e: (i, 0)),
            const((1, d)),
            mod_spec(shift[:, None]), mod_spec(scale[:, None]), mod_spec(gate[:, None]),
            const(rgw.shape), const(rgb.shape), const(rew.shape), const(reb.shape),
            pl.BlockSpec((1, d, D_EXPERT), lambda i, e: (e, 0, 0)),
            pl.BlockSpec((1, d, D_EXPERT), lambda i, e: (e, 0, 0)),
            pl.BlockSpec((1, D_EXPERT, d), lambda i, e: (e, 0, 0)),
        ],
        out_specs=pl.BlockSpec((tm, d), lambda i, e: (i, 0)),
        out_shape=jax.ShapeDtypeStruct((n, d), F32),
        scratch_shapes=[pltpu.VMEM((tm, d), BF16), pltpu.VMEM((tm, LANES), F32), pltpu.VMEM((tm, d), F32)],
        compiler_params=pltpu.CompilerParams(dimension_semantics=("parallel", "arbitrary"),
                                             vmem_limit_bytes=VMEM_LIMIT),
        name="moe",
    )(x.reshape(n, d), g[None], shift[:, None], scale[:, None], gate[:, None], rgw, rgb, rew, reb, w1, w3, w2)
    return out.reshape(bsz, t, d)


def _scan_masks(n, rev):
    row = lax.broadcasted_iota(jnp.int32, (n, n), 0)
    col = lax.broadcasted_iota(jnp.int32, (n, n), 1)
    d = (row - col) * jnp.where(rev, -1, 1)
    return d >= 0, d > 0


def _chunk_rows(s, nsub, size, rev):
    return pl.ds(pl.multiple_of(jnp.where(rev, (nsub - 1 - s) * size, s * size), size), size)


def _last_row(x, rev):
    n = x.shape[0]
    return jnp.where(rev, x[0:1], x[n - 1:n])


def _scan_specs(nseq, blk, n_ctx_blk, n_blk, widths_shared, widths_dir):
    def tblock(i, c):
        back = jnp.where(c < n_ctx_blk, n_ctx_blk - 1 - c, n_ctx_blk + n_blk - 1 - c)
        return jnp.where(i >= nseq, back, c)
    shared = [pl.BlockSpec((1, blk, w), lambda i, c: (i % nseq, tblock(i, c), 0)) for w in widths_shared]
    per_dir = [pl.BlockSpec((1, blk, w), lambda i, c: (i, tblock(i, c), 0)) for w in widths_dir]
    return shared, per_dir, tblock


def _rwkv_kernel(r_ref, k_ref, v_ref, kk_ref, ka_ref, lw_ref, as_ref, y_ref, st_ref, *, nsub, nseq):
    L = RW_CHUNK
    hd = RW_HD
    rev = pl.program_id(0) >= nseq

    @pl.when(pl.program_id(1) == 0)
    def _():
        st_ref[...] = jnp.zeros_like(st_ref)

    incl, strict = _scan_masks(L, rev)
    tri = jnp.where(incl, 1.0, 0.0).astype(BF16)
    zeros = jnp.zeros((L, hd), BF16)
    chains = [(s, h) for s in range(nsub) for h in range(RW_HEADS)]
    rows = [_chunk_rows(s, nsub, L, rev) for s in range(nsub)]

    prep = []
    for s in range(nsub):
        lw = lw_ref[0, rows[s], :]
        c = _cumsum_rows(tri, lw)
        c_end = _last_row(c, rev)
        e_inv = jnp.exp(-c)
        e_end = jnp.exp(c_end - c)
        a_sig = as_ref[0, rows[s], :]
        kk = kk_ref[0, rows[s], :]
        k = k_ref[0, rows[s], :]
        kmod = k * (1.0 + (a_sig - 1.0) * ka_ref[...])
        bv = kk * a_sig
        prep.append(dict(
            at=-kk * jnp.exp(c - lw), rt=r_ref[0, rows[s], :] * jnp.exp(c),
            bt=(bv * e_inv).astype(BF16), kt=(kmod * e_inv).astype(BF16),
            bh=(bv * e_end).astype(BF16), kh=(kmod * e_end).astype(BF16),
            p_end=jnp.exp(c_end), v=v_ref[0, rows[s], :].astype(BF16)))

    def head(name, s, h):
        return prep[s][name][:, h * hd:(h + 1) * hd]

    sc = {}
    for s, h in chains:
        lhs = jnp.concatenate([head("at", s, h), head("rt", s, h)], 0).astype(BF16)
        rhs = jnp.concatenate([head("bt", s, h), head("kt", s, h)], 0)
        sc[s, h] = _dot_nt(lhs, rhs)
    nmat, x, mr = {}, {}, {}
    for s, h in chains:
        m = sc[s, h]
        nmat[s, h] = jnp.where(strict, m[:L, :L], 0.0)
        mak = jnp.where(strict, m[:L, L:], 0.0).astype(BF16)
        mr[s, h] = jnp.concatenate([jnp.where(incl, m[L:, :L], 0.0), jnp.where(incl, m[L:, L:], 0.0)], 1).astype(BF16)
        x[s, h] = jnp.concatenate([head("at", s, h), _dot(mak, head("v", s, h))], 1)
    steps = L.bit_length() - 1
    for i in range(steps):
        for s, h in chains:
            nb = nmat[s, h].astype(BF16)
            xb = x[s, h].astype(BF16)
            if i + 1 < steps:
                prod = _dot(nb, jnp.concatenate([xb, nb], 1))
                x[s, h] = x[s, h] + prod[:, :2 * hd]
                nmat[s, h] = prod[:, 2 * hd:]
            else:
                x[s, h] = x[s, h] + _dot(nb, xb)
    ftop, bhw_t, gt = {}, {}, {}
    for s, h in chains:
        xb = x[s, h].astype(BF16)
        vb = head("v", s, h)
        z = jnp.concatenate([xb, jnp.concatenate([zeros, vb], 1)], 0)
        ftop[s, h] = _dot(mr[s, h], z)
        t1 = _dot_tn(xb, head("bh", s, h))
        bhw_t[s, h] = t1[:hd].astype(BF16)
        gt[s, h] = t1[hd:] + _dot_tn(vb, head("kh", s, h))
    st = [st_ref[h] for h in range(RW_HEADS)]
    for s in range(nsub):
        ys = []
        for h in range(RW_HEADS):
            stb = st[h].astype(BF16)
            q = (head("rt", s, h) + ftop[s, h][:, :hd]).astype(BF16)
            ys.append(_dot_nt(q, stb) + ftop[s, h][:, hd:])
            st[h] = st[h] * head("p_end", s, h) + _dot(stb, bhw_t[s, h]) + gt[s, h]
        y_ref[0, rows[s], :] = jnp.concatenate(ys, 1)
    for h in range(RW_HEADS):
        st_ref[h] = st[h]


def rwkv_scan(r, k, v, kk, k_a, lw, a_sig, *, t_ctx, nsub=4):
    nseq, t, w = r.shape
    blk = RW_CHUNK * nsub
    assert t % blk == 0 and t_ctx % blk == 0
    shared, per_dir, _ = _scan_specs(nseq, blk, t_ctx // blk, t // blk, [w] * 4, [w] * 3)
    return pl.pallas_call(
        functools.partial(_rwkv_kernel, nsub=nsub, nseq=nseq),
        grid=(2 * nseq, t // blk),
        in_specs=shared + [pl.BlockSpec((1, w), lambda i, c: (0, 0))] + per_dir[:2],
        out_specs=per_dir[2],
        out_shape=jax.ShapeDtypeStruct((2 * nseq, t, w), F32),
        scratch_shapes=[pltpu.VMEM((RW_HEADS, RW_HD, RW_HD), F32)],
        compiler_params=pltpu.CompilerParams(dimension_semantics=("parallel", "arbitrary"),
                                             vmem_limit_bytes=VMEM_LIMIT),
        name="rwkv_scan",
    )(r, k, v, kk, k_a[None], lw, a_sig)


def _ssd_kernel(x_ref, b_ref, c_ref, dd_ref, y_ref, st_ref, *, nseq):
    L = SSD_CHUNK
    hd = SSM_HD
    ns = SSM_STATE
    nh = SSM_HEADS
    per_group = nh // SSM_GROUPS
    rev = pl.program_id(0) >= nseq

    @pl.when(pl.program_id(1) == 0)
    def _():
        st_ref[...] = jnp.zeros_like(st_ref)

    incl, _ = _scan_masks(L, rev)
    tri = jnp.where(incl, 1.0, 0.0).astype(BF16)
    row_i = lax.broadcasted_iota(jnp.int32, (L, L), 0)
    col_i = lax.broadcasted_iota(jnp.int32, (L, L), 1)
    eye = jnp.where(row_i == col_i, 1.0, 0.0).astype(BF16)
    dd = dd_ref[0]
    da_parts = _split3(dd[:nh])
    dt_parts = _split3(dd[nh:])
    acs_row = sum(_dot_nt(p, tri) for p in da_parts)
    acs_col = sum(_dot_nt(tri, p) for p in da_parts)
    dt_col = sum(_dot_nt(eye, p) for p in dt_parts)
    a_end_all = _last_row(acs_col, rev)
    xs = x_ref[0]
    ys = []
    for g in range(SSM_GROUPS):
        gs = slice(g * ns, (g + 1) * ns)
        bg = b_ref[0, :, gs]
        cg = c_ref[0, :, gs]
        cb = _dot_nt(cg.astype(BF16), bg.astype(BF16))
        for hh in range(per_group):
            h = g * per_group + hh
            col = acs_col[:, h:h + 1]
            row = acs_row[h:h + 1, :]
            a_end = a_end_all[:, h:h + 1]
            dec = jnp.exp(jnp.where(incl, col - row, -jnp.inf))
            xh = (xs[:, h * hd:(h + 1) * hd] * dt_col[:, h:h + 1]).astype(BF16)
            st = st_ref[h]
            y_diag = _dot((cb * dec).astype(BF16), xh)
            y_off = _dot((cg * jnp.exp(col)).astype(BF16), st.astype(BF16))
            new = _dot_tn((bg * jnp.exp(a_end - col)).astype(BF16), xh)
            st_ref[h] = jnp.exp(a_end) * st + new
            ys.append(y_diag + y_off)
    y_ref[0] = jnp.concatenate(ys, 1)


def ssd_scan(xs, bm, cm, dd, *, t_ctx):
    nseq, t, w = xs.shape
    L = SSD_CHUNK
    assert t % L == 0 and t_ctx % L == 0
    gw = bm.shape[-1]
    shared, per_dir, tblock = _scan_specs(nseq, L, t_ctx // L, t // L, [w, gw, gw], [w])
    dd_spec = pl.BlockSpec((1, 2 * SSM_HEADS, L), lambda i, c: (i, 0, tblock(i, c)))
    return pl.pallas_call(
        functools.partial(_ssd_kernel, nseq=nseq),
        grid=(2 * nseq, t // L),
        in_specs=shared + [dd_spec],
        out_specs=per_dir[0],
        out_shape=jax.ShapeDtypeStruct((2 * nseq, t, w), F32),
        scratch_shapes=[pltpu.VMEM((SSM_HEADS, SSM_STATE, SSM_HD), F32)],
        compiler_params=pltpu.CompilerParams(dimension_semantics=("parallel", "arbitrary"),
                                             vmem_limit_bytes=VMEM_LIMIT),
        name="ssd_scan",
    )(xs, bm, cm, dd)


def _gla_kernel(q_ref, k_ref, v_ref, lg_ref, y_ref, st_ref, *, nseq):
    L = GLA_CHUNK
    rev = pl.program_id(0) >= nseq

    @pl.when(pl.program_id(1) == 0)
    def _():
        st_ref[...] = jnp.zeros_like(st_ref)

    incl, _ = _scan_masks(L, rev)
    tri = jnp.where(incl, 1.0, 0.0).astype(BF16)
    bcs = _cumsum_rows(tri, lg_ref[0])
    mid = jnp.where(rev, bcs[L - 1 - L // 2:L - L // 2], bcs[L // 2:L // 2 + 1])
    bend = _last_row(bcs, rev)
    q = q_ref[0] * GLA_DK ** -0.5
    k = k_ref[0]
    v = v_ref[0]
    qe = (q * jnp.exp(bcs - mid)).astype(BF16)
    ke = (k * jnp.exp(mid - bcs)).astype(BF16)
    qd = (q * jnp.exp(bcs)).astype(BF16)
    kd = (k * jnp.exp(bend - bcs)).astype(BF16)
    eb = jnp.exp(bend)
    ys = []
    for h in range(GLA_HEADS):
        ks = slice(h * GLA_DK, (h + 1) * GLA_DK)
        vh = v[:, h * GLA_DV:(h + 1) * GLA_DV].astype(BF16)
        att = jnp.where(incl, _dot_nt(qe[:, ks], ke[:, ks]), 0.0)
        o_intra = _dot(att.astype(BF16), vh)
        st = st_ref[h]
        o_inter = _dot_nt(qd[:, ks], st.astype(BF16))
        st_ref[h] = st * eb[:, ks] + _dot_tn(vh, kd[:, ks])
        ys.append(o_intra + o_inter)
    y_ref[0] = jnp.concatenate(ys, 1)


def gla_scan(q, k, v, lg, *, t_ctx):
    nseq, t, kw = q.shape
    vw = v.shape[-1]
    L = GLA_CHUNK
    assert t % L == 0 and t_ctx % L == 0
    shared, per_dir, _ = _scan_specs(nseq, L, t_ctx // L, t // L, [kw, kw, vw], [kw, vw])
    return pl.pallas_call(
        functools.partial(_gla_kernel, nseq=nseq),
        grid=(2 * nseq, t // L),
        in_specs=shared + per_dir[:1],
        out_specs=per_dir[1],
        out_shape=jax.ShapeDtypeStruct((2 * nseq, t, vw), F32),
        scratch_shapes=[pltpu.VMEM((GLA_HEADS, GLA_DV, GLA_DK), F32)],
        compiler_params=pltpu.CompilerParams(dimension_semantics=("parallel", "arbitrary"),
                                             vmem_limit_bytes=VMEM_LIMIT),
        name="gla_scan",
    )(q, k, v, lg)


_HP = lax.Precision.HIGHEST


def _cat(zc, zl):
    return jnp.concatenate([zc, zl], 1)


def _sum_dirs(y, t_ctx):
    nb = y.shape[0] // 2
    y = y[:nb] + y[nb:]
    return y[:, :t_ctx], y[:, t_ctx:]


def _token_shift(u, mu_prev, mu_next):
    pad = jnp.zeros_like(u[:, :1])
    prev = jnp.concatenate([pad, u[:, :-1]], 1)
    nxt = jnp.concatenate([u[:, 1:], pad], 1)
    return u + mu_prev * (prev - u) + mu_next * (nxt - u)


def _rw_heads(z):
    return z.reshape(z.shape[:2] + (RW_HEADS, RW_HD))


def _rwkv_group(u_c, u_l, mu_prev, mu_next, w0, w2, a0, a2, g2, k_k, k_a, r_k, gn_g, gn_b, ctx_out):
    splits = [RW_W, 2 * RW_W, 3 * RW_W, 3 * RW_W + RW_DECAY_RANK, 3 * RW_W + RW_DECAY_RANK + RW_ICLR_RANK]
    t_ctx = u_c.shape[1]

    def prep(u):
        r, k, v, wl, al, gl = jnp.split(_token_shift(u, mu_prev, mu_next), splits, -1)
        return r, k, v, jnp.tanh(wl), al, gl

    def per_dir(p, d):
        _, _, _, wl, al, _ = p
        w_log = -jax.nn.softplus(-(w0[d] + jnp.dot(wl, w2[d], precision=_HP))) - 0.5
        return -jnp.exp(w_log), jax.nn.sigmoid(a0[d] + jnp.dot(al, a2[d], precision=_HP))

    def finish(p, y):
        r, k, v, _, _, gl = p
        bonus = jnp.sum(_rw_heads(r * k * r_k), -1, keepdims=True) * _rw_heads(v)
        yh = _rw_heads(y)
        mu = jnp.mean(yh, -1, keepdims=True)
        var = jnp.mean(jnp.square(yh - mu), -1, keepdims=True)
        yn = ((yh - mu) * lax.rsqrt(var + RW_GN_EPS)).reshape(y.shape) * gn_g + gn_b
        out = yn + bonus.reshape(r.shape)
        return out * jnp.dot(jax.nn.sigmoid(gl), g2, precision=_HP)

    pc, pl_ = prep(u_c), prep(u_l)
    r, k, v = (_cat(pc[j], pl_[j]) for j in range(3))
    kk = _rw_heads(k * k_k)
    kk = (kk * lax.rsqrt(jnp.sum(kk * kk, -1, keepdims=True) + 1e-12)).reshape(k.shape)
    dirs = [[_cat(a, b) for a, b in zip(per_dir(pc, d), per_dir(pl_, d))] for d in (0, 1)]
    lw = jnp.concatenate([dirs[0][0], dirs[1][0]], 0)
    a_sig = jnp.concatenate([dirs[0][1], dirs[1][1]], 0)
    yc, yl = _sum_dirs(rwkv_scan(r, k, v, kk, k_a, lw, a_sig, t_ctx=t_ctx), t_ctx)
    return (finish(pc, yc) if ctx_out else None), finish(pl_, yl)


def _dwconv_grid(u, w, b, rows):
    bsz, n, ch = u.shape
    img = u.reshape(bsz, rows, n // rows, ch)
    out = lax.conv_general_dilated(img, w[:, :, None, :], (1, 1), 'SAME',
                                   dimension_numbers=('NHWC', 'HWIO', 'NHWC'), feature_group_count=ch,
                                   precision=_HP)
    return out.reshape(bsz, n, ch) + b


def rmsnorm_f32(h, g):
    return h * lax.rsqrt(jnp.mean(h * h, -1, keepdims=True) + NORM_EPS) * g


def _ssm_group(zc, xbc_c, dt_c, zl, xbc_l, dt_l, rows, conv_w, conv_b, dt_bias, a_log, d_skip, norm_g, ctx_out):
    t_ctx = zc.shape[1]

    def prep(xbc, grid_rows):
        xbc = jax.nn.silu(_dwconv_grid(xbc, conv_w, conv_b, grid_rows))
        return jnp.split(xbc, [SSM_W, SSM_W + SSM_GROUPS * SSM_STATE], -1)

    def per_dir(dt, d):
        dtp = jax.nn.softplus(dt + dt_bias[d])
        return jnp.swapaxes(jnp.concatenate([-jnp.exp(a_log[d]) * dtp, dtp], -1), 1, 2)

    def finish(z, xs, y):
        y = y + jnp.repeat(d_skip, SSM_HD) * xs
        return rmsnorm_f32(y * jax.nn.silu(z), norm_g)

    xs_c, bm_c, cm_c = prep(xbc_c, 1)
    xs_l, bm_l, cm_l = prep(xbc_l, rows)
    dt = _cat(dt_c, dt_l)
    dd = jnp.concatenate([per_dir(dt, 0), per_dir(dt, 1)], 0)
    y = ssd_scan(_cat(xs_c, xs_l), _cat(bm_c, bm_l), _cat(cm_c, cm_l), dd, t_ctx=t_ctx)
    yc, yl = _sum_dirs(y, t_ctx)
    return (finish(zc, xs_c, yc) if ctx_out else None), finish(zl, xs_l, yl)


def _gla_group(u_c, gl_c, u_l, gl_l, ga2, gb, norm_g, ctx_out):
    t_ctx = u_c.shape[1]
    u = _cat(u_c, u_l)
    gl = _cat(gl_c, gl_l)
    q, k, v, r = jnp.split(u, [GLA_KW, 2 * GLA_KW, 2 * GLA_KW + GLA_VW], -1)
    lg = jnp.concatenate([jax.nn.log_sigmoid(jnp.dot(gl, ga2[d], precision=_HP) + gb[d]) / GLA_TAU for d in (0, 1)], 0)

    def finish(r, y):
        yh = y.reshape(y.shape[:2] + (GLA_HEADS, GLA_DV))
        yn = (yh * lax.rsqrt(jnp.mean(yh * yh, -1, keepdims=True) + NORM_EPS)).reshape(y.shape) * norm_g
        return yn * jax.nn.silu(r)

    yc, yl = _sum_dirs(gla_scan(q, k, v, lg, t_ctx=t_ctx), t_ctx)
    return (finish(r[:, :t_ctx], yc) if ctx_out else None), finish(r[:, t_ctx:], yl)


_IN_WIDTHS = (RW_COLS, SSM_W, SSM_CONV_CH, 2 * GLA_KW + 2 * GLA_VW, SMALL_W)


def _arrange_w_in(w_in):
    rw, ssm, gla = jnp.split(w_in, [RW_COLS, RW_COLS + SSM_COLS], -1)
    z, xbc, dt = jnp.split(ssm, [SSM_W, SSM_W + SSM_CONV_CH], -1)
    qkv, gl, r = jnp.split(gla, [2 * GLA_KW + GLA_VW, 2 * GLA_KW + GLA_VW + GLA_GATE_RANK], -1)
    pad = jnp.zeros((w_in.shape[0], SMALL_W - SSM_HEADS - GLA_GATE_RANK), w_in.dtype)
    return jnp.concatenate([rw, z, xbc, qkv, r, dt, gl, pad], -1).astype(BF16)


def _pad_lanes(w, reps=1):
    w = jnp.repeat(w, reps, axis=-1) if reps > 1 else w
    return jnp.pad(w, [(0, 0)] * (w.ndim - 1) + [(0, LANES - w.shape[-1])])


def kernel(x, c, ctx, c_ctx, ada_w, ada_b, norm1_g, norm2_g, w_in, w_out, rw_mu_prev, rw_mu_next, rw_w0, rw_w2, rw_a0, rw_a2, rw_g2, rw_k_k, rw_k_a, rw_r_k, rw_gn_g, rw_gn_b, ssm_conv_w, ssm_conv_b, ssm_dt_bias, ssm_a_log, ssm_d, ssm_norm_g, gla_ga2, gla_gb, gla_norm_g, moe_rg_w, moe_rg_b, moe_re_w, moe_re_b, moe_w1, moe_w3, moe_w2, final_g):
    depth = ada_w.shape[0]
    rows = x.shape[1] // GRID_W
    cond_l = jax.nn.silu(c)
    cond_c = jax.nn.silu(c_ctx)[None]
    tm = 512
    for l in range(depth):
        ctx_out = l < depth - 1
        mod_l = jnp.split(jnp.dot(cond_l, ada_w[l], precision=_HP) + ada_b[l], 6, -1)
        mod_c = jnp.split(jnp.dot(cond_c, ada_w[l], precision=_HP) + ada_b[l], 6, -1)
        w_in_l = _arrange_w_in(w_in[l])
        rl, zl, xbcl, gql, sml = in_proj(x, norm1_g[l], mod_l[0], mod_l[1], w_in_l, _IN_WIDTHS, tm=tm)
        rc, zc, xbcc, gqc, smc = in_proj(ctx, norm1_g[l], mod_c[0], mod_c[1], w_in_l, _IN_WIDTHS, tm=tm)
        dtl, gll = sml[..., :SSM_HEADS], sml[..., SSM_HEADS:SSM_HEADS + GLA_GATE_RANK]
        dtc, glc = smc[..., :SSM_HEADS], smc[..., SSM_HEADS:SSM_HEADS + GLA_GATE_RANK]
        a_c, a_l = _rwkv_group(rc, rl, rw_mu_prev[l], rw_mu_next[l], rw_w0[l], rw_w2[l], rw_a0[l], rw_a2[l],
                               rw_g2[l], rw_k_k[l], rw_k_a[l], rw_r_k[l], rw_gn_g[l], rw_gn_b[l], ctx_out)
        b_c, b_l = _ssm_group(zc, xbcc, dtc, zl, xbcl, dtl, rows, ssm_conv_w[l], ssm_conv_b[l], ssm_dt_bias[l],
                              ssm_a_log[l], ssm_d[l], ssm_norm_g[l], ctx_out)
        g_c, g_l = _gla_group(gqc, glc, gql, gll, gla_ga2[l], gla_gb[l], gla_norm_g[l], ctx_out)
        w_out_l = w_out[l].astype(BF16)
        rgw = _pad_lanes(moe_rg_w[l], EXPERTS_PER_GROUP)
        rgb = _pad_lanes(moe_rg_b[l][None], EXPERTS_PER_GROUP)
        rew = _pad_lanes(moe_re_w[l])
        reb = _pad_lanes(moe_re_b[l][None])
        w1, w3, w2 = moe_w1[l].astype(BF16), moe_w3[l].astype(BF16), moe_w2[l].astype(BF16)
        x = out_proj(jnp.concatenate([a_l, b_l, g_l], -1), w_out_l, x, mod_l[2], tm=tm)
        x = moe_block(x, norm2_g[l], mod_l[3], mod_l[4], mod_l[5], rgw, rgb, rew, reb, w1, w3, w2, tm=tm)
        if ctx_out:
            ctx = out_proj(jnp.concatenate([a_c, b_c, g_c], -1), w_out_l, ctx, mod_c[2], tm=tm)
            ctx = moe_block(ctx, norm2_g[l], mod_c[3], mod_c[4], mod_c[5], rgw, rgb, rew, reb, w1, w3, w2, tm=tm)
    return rmsnorm_f32(x, final_g)
```

```python
import functools

import jax
import jax.numpy as jnp
from jax import lax
from jax.experimental import pallas as pl
from jax.experimental.pallas import tpu as pltpu

F32 = jnp.float32
BF16 = jnp.bfloat16

D_MODEL = 1024
GRID_W = 64
NORM_EPS = 1e-6

RW_HEADS = 4
RW_HD = 64
RW_W = RW_HEADS * RW_HD
RW_DECAY_RANK = 64
RW_ICLR_RANK = 64
RW_GATE_RANK = 128
RW_GN_EPS = 64e-5
RW_COLS = 3 * RW_W + RW_DECAY_RANK + RW_ICLR_RANK + RW_GATE_RANK

SSM_HEADS = 8
SSM_HD = 64
SSM_W = SSM_HEADS * SSM_HD
SSM_GROUPS = 2
SSM_STATE = 64
SSM_CONV_CH = SSM_W + 2 * SSM_GROUPS * SSM_STATE
SSM_COLS = SSM_W + SSM_CONV_CH + SSM_HEADS

GLA_HEADS = 4
GLA_DK = 32
GLA_DV = 64
GLA_KW = GLA_HEADS * GLA_DK
GLA_VW = GLA_HEADS * GLA_DV
GLA_GATE_RANK = 16
GLA_TAU = 16.0
GLA_COLS = 2 * GLA_KW + GLA_VW + GLA_GATE_RANK + GLA_VW

N_GROUPS = 4
EXPERTS_PER_GROUP = 4
N_EXPERTS = N_GROUPS * EXPERTS_PER_GROUP
D_EXPERT = 512

LANES = 128
SMALL_W = LANES
VMEM_LIMIT = 56 * 1024 * 1024

PROJ_TM = 512
MOE_TM = 1024

RW_CHUNK = 64
SSD_CHUNK = 128
GLA_CHUNK = 64

_NT = (((1,), (1,)), ((), ()))
_TN = (((0,), (0,)), ((), ()))


def _dot(a, b):
    return jnp.dot(a, b, preferred_element_type=F32)


def _dot_nt(a, b):
    return lax.dot_general(a, b, _NT, preferred_element_type=F32)


def _dot_tn(a, b):
    return lax.dot_general(a, b, _TN, preferred_element_type=F32)


def _split3(x):
    hi = x.astype(BF16)
    r1 = x - hi.astype(F32)
    mid = r1.astype(BF16)
    lo = (r1 - mid.astype(F32)).astype(BF16)
    return hi, mid, lo


def _cumsum_rows(tri, x):
    hi, mid, lo = _split3(x)
    return _dot(tri, hi) + _dot(tri, mid) + _dot(tri, lo)


def _norm_mod(x, g, shift, scale):
    h = x * lax.rsqrt(jnp.mean(x * x, -1, keepdims=True) + NORM_EPS) * g
    return h * (1.0 + scale) + shift


def _in_proj_kernel(x_ref, g_ref, sh_ref, sc_ref, w_ref, *out_refs, widths):
    h = _norm_mod(x_ref[...], g_ref[...], sh_ref[0], sc_ref[0]).astype(BF16)
    off = 0
    for o_ref, wd in zip(out_refs, widths):
        o_ref[...] = _dot(h, w_ref[:, off:off + wd])
        off += wd


def _mod_spec(mod, tiles_per_batch):
    d = mod.shape[-1]
    if mod.shape[0] == 1:
        return pl.BlockSpec((1, 1, d), lambda i: (0, 0, 0))
    return pl.BlockSpec((1, 1, d), lambda i: (i // tiles_per_batch, 0, 0))


def in_proj(x, g, shift, scale, w, widths, *, tm):
    bsz, t, d = x.shape
    n = bsz * t
    tm = min(tm, t)
    assert t % tm == 0
    tpb = t // tm
    x2 = x.reshape(n, d)
    outs = pl.pallas_call(
        functools.partial(_in_proj_kernel, widths=widths),
        grid=(n // tm,),
        in_specs=[
            pl.BlockSpec((tm, d), lambda i: (i, 0)),
            pl.BlockSpec((1, d), lambda i: (0, 0)),
            _mod_spec(shift[:, None], tpb),
            _mod_spec(scale[:, None], tpb),
            pl.BlockSpec(w.shape, lambda i: (0, 0)),
        ],
        out_specs=[pl.BlockSpec((tm, wd), lambda i: (i, 0)) for wd in widths],
        out_shape=[jax.ShapeDtypeStruct((n, wd), F32) for wd in widths],
        compiler_params=pltpu.CompilerParams(dimension_semantics=("parallel",), vmem_limit_bytes=VMEM_LIMIT),
        name="in_proj",
    )(x2, g[None], shift[:, None], scale[:, None], w)
    return [o.reshape(bsz, t, wd) for o, wd in zip(outs, widths)]


def _out_proj_kernel(m_ref, w_ref, x_ref, gate_ref, o_ref):
    acc = _dot(m_ref[...].astype(BF16), w_ref[...])
    o_ref[...] = x_ref[...] + gate_ref[0] * acc


def out_proj(m, w, x, gate, *, tm):
    bsz, t, d = x.shape
    n = bsz * t
    tm = min(tm, t)
    assert t % tm == 0
    tpb = t // tm
    kdim = m.shape[-1]
    out = pl.pallas_call(
        _out_proj_kernel,
        grid=(n // tm,),
        in_specs=[
            pl.BlockSpec((tm, kdim), lambda i: (i, 0)),
            pl.BlockSpec(w.shape, lambda i: (0, 0)),
            pl.BlockSpec((tm, d), lambda i: (i, 0)),
            _mod_spec(gate[:, None], tpb),
        ],
        out_specs=pl.BlockSpec((tm, d), lambda i: (i, 0)),
        out_shape=jax.ShapeDtypeStruct((n, d), F32),
        compiler_params=pltpu.CompilerParams(dimension_semantics=("parallel",), vmem_limit_bytes=VMEM_LIMIT),
        name="out_proj",
    )(m.reshape(n, kdim), w, x.reshape(n, d), gate[:, None])
    return out.reshape(bsz, t, d)


def _route(h, rgw_ref, rgb_ref, rew_ref, reb_ref):
    hp = lax.Precision.HIGHEST
    gl = jnp.dot(h, rgw_ref[...], precision=hp, preferred_element_type=F32) + rgb_ref[...]
    el = jnp.dot(h, rew_ref[...], precision=hp, preferred_element_type=F32) + reb_ref[...]
    lane = lax.broadcasted_iota(jnp.int32, gl.shape, 1)
    valid = lane < N_EXPERTS
    neg = -jnp.inf
    big = jnp.int32(1 << 20)
    glm = jnp.where(valid, gl, neg)
    gmax = jnp.max(glm, -1, keepdims=True)
    g_sel_lane = jnp.min(jnp.where(glm == gmax, lane, big), -1, keepdims=True)
    in_group = (lane // EXPERTS_PER_GROUP) == (g_sel_lane // EXPERTS_PER_GROUP)
    gsum = jnp.sum(jnp.where(valid, jnp.exp(glm - gmax), 0.0), -1, keepdims=True) / EXPERTS_PER_GROUP
    p_group = 1.0 / gsum
    elm = jnp.where(in_group & valid, el, neg)
    m1 = jnp.max(elm, -1, keepdims=True)
    i1 = jnp.min(jnp.where(elm == m1, lane, big), -1, keepdims=True)
    elm2 = jnp.where(lane == i1, neg, elm)
    m2 = jnp.max(elm2, -1, keepdims=True)
    i2 = jnp.min(jnp.where(elm2 == m2, lane, big), -1, keepdims=True)
    p2 = jnp.exp(m2 - m1)
    wa = p_group / (1.0 + p2)
    wb = p_group * p2 / (1.0 + p2)
    return jnp.where(lane == i1, wa, 0.0) + jnp.where(lane == i2, wb, 0.0)


def _moe_kernel(x_ref, g_ref, sh_ref, sc_ref, gate_ref, rgw_ref, rgb_ref, rew_ref, reb_ref,
                w1_ref, w3_ref, w2_ref, o_ref, h_ref, comb_ref, acc_ref):
    e = pl.program_id(1)

    @pl.when(e == 0)
    def _():
        h = _norm_mod(x_ref[...], g_ref[...], sh_ref[0], sc_ref[0])
        h_ref[...] = h.astype(BF16)
        comb_ref[...] = _route(h, rgw_ref, rgb_ref, rew_ref, reb_ref)
        acc_ref[...] = jnp.zeros_like(acc_ref)

    hb = h_ref[...]
    a = _dot(hb, w1_ref[0])
    b = _dot(hb, w3_ref[0])
    hid = (a * jax.nn.sigmoid(a) * b).astype(BF16)
    y = _dot(hid, w2_ref[0])
    comb = comb_ref[...]
    lane = lax.broadcasted_iota(jnp.int32, comb.shape, 1)
    col = jnp.sum(jnp.where(lane == e, comb, 0.0), -1, keepdims=True)
    acc_ref[...] += col * y

    @pl.when(e == N_EXPERTS - 1)
    def _():
        o_ref[...] = x_ref[...] + gate_ref[0] * acc_ref[...]


def moe_block(x, g, shift, scale, gate, rgw, rgb, rew, reb, w1, w3, w2, *, tm):
    bsz, t, d = x.shape
    n = bsz * t
    tm = min(tm, t)
    assert t % tm == 0
    tpb = t // tm

    def mod_spec(mod):
        if mod.shape[0] == 1:
            return pl.BlockSpec((1, 1, d), lambda i, e: (0, 0, 0))
        return pl.BlockSpec((1, 1, d), lambda i, e: (i // tpb, 0, 0))

    const = lambda shape: pl.BlockSpec(shape, lambda i, e: (0,) * len(shape))
    out = pl.pallas_call(
        _moe_kernel,
        grid=(n // tm, N_EXPERTS),
        in_specs=[
            pl.BlockSpec((tm, d), lambda i, e: (i, 0)),
            const((1, d)),
            mod_spec(shift[:, None]), mod_spec(scale[:, None]), mod_spec(gate[:, None]),
            const(rgw.shape), const(rgb.shape), const(rew.shape), const(reb.shape),
            pl.BlockSpec((1, d, D_EXPERT), lambda i, e: (e, 0, 0)),
            pl.BlockSpec((1, d, D_EXPERT), lambda i, e: (e, 0, 0)),
            pl.BlockSpec((1, D_EXPERT, d), lambda i, e: (e, 0, 0)),
        ],
        out_specs=pl.BlockSpec((tm, d), lambda i, e: (i, 0)),
        out_shape=jax.ShapeDtypeStruct((n, d), F32),
        scratch_shapes=[pltpu.VMEM((tm, d), BF16), pltpu.VMEM((tm, LANES), F32), pltpu.VMEM((tm, d), F32)],
        compiler_params=pltpu.CompilerParams(dimension_semantics=("parallel", "arbitrary"),
                                             vmem_limit_bytes=VMEM_LIMIT),
        name="moe",
    )(x.reshape(n, d), g[None], shift[:, None], scale[:, None], gate[:, None], rgw, rgb, rew, reb, w1, w3, w2)
    return out.reshape(bsz, t, d)


def _scan_masks(n, rev):
    row = lax.broadcasted_iota(jnp.int32, (n, n), 0)
    col = lax.broadcasted_iota(jnp.int32, (n, n), 1)
    d = (row - col) * jnp.where(rev, -1, 1)
    return d >= 0, d > 0


def _chunk_rows(s, nsub, size, rev):
    return pl.ds(pl.multiple_of(jnp.where(rev, (nsub - 1 - s) * size, s * size), size), size)


def _last_row(x, rev):
    n = x.shape[0]
    return jnp.where(rev, x[0:1], x[n - 1:n])


def _scan_specs(nseq, blk, n_ctx_blk, n_blk, widths_shared, widths_dir):
    def tblock(i, c):
        back = jnp.where(c < n_ctx_blk, n_ctx_blk - 1 - c, n_ctx_blk + n_blk - 1 - c)
        return jnp.where(i >= nseq, back, c)
    shared = [pl.BlockSpec((1, blk, w), lambda i, c: (i % nseq, tblock(i, c), 0)) for w in widths_shared]
    per_dir = [pl.BlockSpec((1, blk, w), lambda i, c: (i, tblock(i, c), 0)) for w in widths_dir]
    return shared, per_dir, tblock


def _rwkv_kernel(r_ref, k_ref, v_ref, kk_ref, ka_ref, lw_ref, as_ref, y_ref, st_ref, *, nsub, nseq):
    L = RW_CHUNK
    hd = RW_HD
    rev = pl.program_id(0) >= nseq

    @pl.when(pl.program_id(1) == 0)
    def _():
        st_ref[...] = jnp.zeros_like(st_ref)

    incl, strict = _scan_masks(L, rev)
    tri = jnp.where(incl, 1.0, 0.0).astype(BF16)
    zeros = jnp.zeros((L, hd), BF16)
    chains = [(s, h) for s in range(nsub) for h in range(RW_HEADS)]
    rows = [_chunk_rows(s, nsub, L, rev) for s in range(nsub)]

    prep = []
    for s in range(nsub):
        lw = lw_ref[0, rows[s], :]
        c = _cumsum_rows(tri, lw)
        c_end = _last_row(c, rev)
        e_inv = jnp.exp(-c)
        e_end = jnp.exp(c_end - c)
        a_sig = as_ref[0, rows[s], :]
        kk = kk_ref[0, rows[s], :]
        k = k_ref[0, rows[s], :]
        kmod = k * (1.0 + (a_sig - 1.0) * ka_ref[...])
        bv = kk * a_sig
        prep.append(dict(
            at=-kk * jnp.exp(c - lw), rt=r_ref[0, rows[s], :] * jnp.exp(c),
            bt=(bv * e_inv).astype(BF16), kt=(kmod * e_inv).astype(BF16),
            bh=(bv * e_end).astype(BF16), kh=(kmod * e_end).astype(BF16),
            p_end=jnp.exp(c_end), v=v_ref[0, rows[s], :].astype(BF16)))

    def head(name, s, h):
        return prep[s][name][:, h * hd:(h + 1) * hd]

    sc = {}
    for s, h in chains:
        lhs = jnp.concatenate([head("at", s, h), head("rt", s, h)], 0).astype(BF16)
        rhs = jnp.concatenate([head("bt", s, h), head("kt", s, h)], 0)
        sc[s, h] = _dot_nt(lhs, rhs)
    nmat, x, mr = {}, {}, {}
    for s, h in chains:
        m = sc[s, h]
        nmat[s, h] = jnp.where(strict, m[:L, :L], 0.0)
        mak = jnp.where(strict, m[:L, L:], 0.0).astype(BF16)
        mr[s, h] = jnp.concatenate([jnp.where(incl, m[L:, :L], 0.0), jnp.where(incl, m[L:, L:], 0.0)], 1).astype(BF16)
        x[s, h] = jnp.concatenate([head("at", s, h), _dot(mak, head("v", s, h))], 1)
    steps = L.bit_length() - 1
    for i in range(steps):
        for s, h in chains:
            nb = nmat[s, h].astype(BF16)
            xb = x[s, h].astype(BF16)
            if i + 1 < steps:
                prod = _dot(nb, jnp.concatenate([xb, nb], 1))
                x[s, h] = x[s, h] + prod[:, :2 * hd]
                nmat[s, h] = prod[:, 2 * hd:]
            else:
                x[s, h] = x[s, h] + _dot(nb, xb)
    ftop, bhw_t, gt = {}, {}, {}
    for s, h in chains:
        xb = x[s, h].astype(BF16)
        vb = head("v", s, h)
        z = jnp.concatenate([xb, jnp.concatenate([zeros, vb], 1)], 0)
        ftop[s, h] = _dot(mr[s, h], z)
        t1 = _dot_tn(xb, head("bh", s, h))
        bhw_t[s, h] = t1[:hd].astype(BF16)
        gt[s, h] = t1[hd:] + _dot_tn(vb, head("kh", s, h))
    st = [st_ref[h] for h in range(RW_HEADS)]
    for s in range(nsub):
        ys = []
        for h in range(RW_HEADS):
            stb = st[h].astype(BF16)
            q = (head("rt", s, h) + ftop[s, h][:, :hd]).astype(BF16)
            ys.append(_dot_nt(q, stb) + ftop[s, h][:, hd:])
            st[h] = st[h] * head("p_end", s, h) + _dot(stb, bhw_t[s, h]) + gt[s, h]
        y_ref[0, rows[s], :] = jnp.concatenate(ys, 1)
    for h in range(RW_HEADS):
        st_ref[h] = st[h]


def rwkv_scan(r, k, v, kk, k_a, lw, a_sig, *, t_ctx, nsub=4):
    nseq, t, w = r.shape
    blk = RW_CHUNK * nsub
    assert t % blk == 0 and t_ctx % blk == 0
    shared, per_dir, _ = _scan_specs(nseq, blk, t_ctx // blk, t // blk, [w] * 4, [w] * 3)
    return pl.pallas_call(
        functools.partial(_rwkv_kernel, nsub=nsub, nseq=nseq),
        grid=(2 * nseq, t // blk),
        in_specs=shared + [pl.BlockSpec((1, w), lambda i, c: (0, 0))] + per_dir[:2],
        out_specs=per_dir[2],
        out_shape=jax.ShapeDtypeStruct((2 * nseq, t, w), F32),
        scratch_shapes=[pltpu.VMEM((RW_HEADS, RW_HD, RW_HD), F32)],
        compiler_params=pltpu.CompilerParams(dimension_semantics=("parallel", "arbitrary"),
                                             vmem_limit_bytes=VMEM_LIMIT),
        name="rwkv_scan",
    )(r, k, v, kk, k_a[None], lw, a_sig)


def _expand_cols(x, e):
    return sum(_dot(p, e) for p in _split3(x))


def _ssd_kernel(x_ref, b_ref, c_ref, dd_ref, y_ref, st_ref, *, nsub, nseq):
    L = SSD_CHUNK
    nh = SSM_HEADS
    pw = 2 * SSM_HD
    npairs = nh // 2
    pairs_per_group = npairs // SSM_GROUPS
    rev = pl.program_id(0) >= nseq

    @pl.when(pl.program_id(1) == 0)
    def _():
        st_ref[...] = jnp.zeros_like(st_ref)

    incl, _ = _scan_masks(L, rev)
    tri = jnp.where(incl, 1.0, 0.0).astype(BF16)
    h_i = lax.broadcasted_iota(jnp.int32, (nh, nh * SSM_HD), 0)
    c_i = lax.broadcasted_iota(jnp.int32, (nh, nh * SSM_HD), 1)
    e_head = jnp.where(c_i // SSM_HD == h_i, 1.0, 0.0).astype(BF16)
    h_j = lax.broadcasted_iota(jnp.int32, (nh, nh * L), 0)
    c_j = lax.broadcasted_iota(jnp.int32, (nh, nh * L), 1)
    e_wide = jnp.where(c_j // L == h_j, 1.0, 0.0).astype(BF16)
    lane_p = lax.broadcasted_iota(jnp.int32, (L, pw), 1)
    first_half = lane_p < SSM_HD
    r_bd = lax.broadcasted_iota(jnp.int32, (pw, pw), 0)
    c_bd = lax.broadcasted_iota(jnp.int32, (pw, pw), 1)
    bd_mask = (r_bd < SSM_HD) == (c_bd < SSM_HD)

    rows = [_chunk_rows(s, nsub, L, rev) for s in range(nsub)]
    work = []
    for s in range(nsub):
        dd = dd_ref[0, jnp.where(rev, nsub - 1 - s, s)]
        da_parts = _split3(dd[:nh])
        acs_row = sum(_dot_nt(p, tri) for p in da_parts)
        acs_col = sum(_dot_nt(tri, p) for p in da_parts)
        colx = _expand_cols(acs_col, e_wide)
        dtx = sum(_dot_tn(p, e_head) for p in _split3(dd[nh:]))
        xdt = (x_ref[0, rows[s], :] * dtx).astype(BF16)
        bm = b_ref[0, rows[s], :]
        cm = c_ref[0, rows[s], :]
        bm_sw = pltpu.roll(bm, SSM_STATE, 1)
        cm_sw = pltpu.roll(cm, SSM_STATE, 1)
        b2 = [jnp.where(first_half, bm, bm_sw), jnp.where(first_half, bm_sw, bm)]
        c2 = [jnp.where(first_half, cm, cm_sw), jnp.where(first_half, cm_sw, cm)]
        cb = [_dot_nt(cm[:, g * SSM_STATE:(g + 1) * SSM_STATE].astype(BF16),
                      bm[:, g * SSM_STATE:(g + 1) * SSM_STATE].astype(BF16)) for g in range(SSM_GROUPS)]
        work.append(dict(acs_row=acs_row, colx=colx, xdt=xdt, b2=b2, c2=c2, cb=cb))

    y_diag, ce, new, ea = {}, {}, {}, {}
    for s in range(nsub):
        w = work[s]
        for p in range(npairs):
            g = p // pairs_per_group
            h0, h1 = 2 * p, 2 * p + 1
            cx0 = w["colx"][:, h0 * L:(h0 + 1) * L]
            cx1 = w["colx"][:, h1 * L:(h1 + 1) * L]
            s0 = w["cb"][g] * jnp.exp(jnp.where(incl, cx0 - w["acs_row"][h0:h0 + 1, :], -jnp.inf))
            s1 = w["cb"][g] * jnp.exp(jnp.where(incl, cx1 - w["acs_row"][h1:h1 + 1, :], -jnp.inf))
            xp = w["xdt"][:, p * pw:(p + 1) * pw]
            zero = jnp.zeros_like(xp)
            x_bd = jnp.concatenate([jnp.where(first_half, xp, zero), jnp.where(first_half, zero, xp)], 0)
            y_diag[s, p] = _dot(jnp.concatenate([s0, s1], 1).astype(BF16), x_bd)
            col = jnp.where(first_half, cx0[:, :pw], cx1[:, :pw])
            a_end = _last_row(col, rev)
            ce[s, p] = (w["c2"][g] * jnp.exp(col)).astype(BF16)
            be = (w["b2"][g] * jnp.exp(a_end - col)).astype(BF16)
            new[s, p] = jnp.where(bd_mask, _dot_tn(be, xp), 0.0)
            ea[s, p] = jnp.exp(a_end)
    st = [st_ref[p] for p in range(npairs)]
    for s in range(nsub):
        ys = []
        for p in range(npairs):
            ys.append(y_diag[s, p] + _dot(ce[s, p], st[p].astype(BF16)))
            st[p] = st[p] * ea[s, p] + new[s, p]
        y_ref[0, rows[s], :] = jnp.concatenate(ys, 1)
    for p in range(npairs):
        st_ref[p] = st[p]


def ssd_scan(xs, bm, cm, dd, *, t_ctx, nsub=2):
    nseq, t, w = xs.shape
    blk = SSD_CHUNK * nsub
    assert t % blk == 0 and t_ctx % blk == 0
    assert SSM_STATE == SSM_HD and SSD_CHUNK == 2 * SSM_HD
    gw = bm.shape[-1]
    shared, per_dir, tblock = _scan_specs(nseq, blk, t_ctx // blk, t // blk, [w, gw, gw], [w])
    dd_spec = pl.BlockSpec((1, nsub, 2 * SSM_HEADS, SSD_CHUNK), lambda i, c: (i, tblock(i, c), 0, 0))
    return pl.pallas_call(
        functools.partial(_ssd_kernel, nsub=nsub, nseq=nseq),
        grid=(2 * nseq, t // blk),
        in_specs=shared + [dd_spec],
        out_specs=per_dir[0],
        out_shape=jax.ShapeDtypeStruct((2 * nseq, t, w), F32),
        scratch_shapes=[pltpu.VMEM((SSM_HEADS // 2, 2 * SSM_STATE, 2 * SSM_HD), F32)],
        compiler_params=pltpu.CompilerParams(dimension_semantics=("parallel", "arbitrary"),
                                             vmem_limit_bytes=VMEM_LIMIT),
        name="ssd_scan",
    )(xs, bm, cm, dd)


def _gla_kernel(q_ref, k_ref, v_ref, lg_ref, y_ref, st_ref, *, nsub, nseq):
    L = GLA_CHUNK
    rev = pl.program_id(0) >= nseq

    @pl.when(pl.program_id(1) == 0)
    def _():
        st_ref[...] = jnp.zeros_like(st_ref)

    nh = GLA_HEADS
    incl, _ = _scan_masks(L, rev)
    tri = jnp.where(incl, 1.0, 0.0).astype(BF16)
    incl_h = jnp.concatenate([incl] * nh, 0)
    r_q = lax.broadcasted_iota(jnp.int32, (nh * L, GLA_KW), 0)
    c_q = lax.broadcasted_iota(jnp.int32, (nh * L, GLA_KW), 1)
    q_mask = r_q // L == c_q // GLA_DK
    r_s = lax.broadcasted_iota(jnp.int32, (GLA_VW, GLA_KW), 0)
    c_s = lax.broadcasted_iota(jnp.int32, (GLA_VW, GLA_KW), 1)
    st_mask = r_s // GLA_DV == c_s // GLA_DK
    lane_v = lax.broadcasted_iota(jnp.int32, (L, GLA_VW), 1) // GLA_DV
    rows = [_chunk_rows(s, nsub, L, rev) for s in range(nsub)]

    o_intra, qd, kv_t, eb = [], [], [], []
    for s in range(nsub):
        bcs = _cumsum_rows(tri, lg_ref[0, rows[s], :])
        mid = jnp.where(rev, bcs[L - 1 - L // 2:L - L // 2], bcs[L // 2:L // 2 + 1])
        bend = _last_row(bcs, rev)
        q = q_ref[0, rows[s], :] * GLA_DK ** -0.5
        k = k_ref[0, rows[s], :]
        vb = v_ref[0, rows[s], :].astype(BF16)
        qe = (q * jnp.exp(bcs - mid)).astype(BF16)
        ke = (k * jnp.exp(mid - bcs)).astype(BF16)
        q_st = jnp.where(q_mask, jnp.concatenate([qe] * nh, 0), jnp.zeros((), BF16))
        att = jnp.where(incl_h, _dot_nt(q_st, ke), 0.0).astype(BF16)
        full = _dot(att, vb)
        o_intra.append(sum(jnp.where(lane_v == h, full[h * L:(h + 1) * L], 0.0) for h in range(nh)))
        qd.append((q * jnp.exp(bcs)).astype(BF16))
        kd = (k * jnp.exp(bend - bcs)).astype(BF16)
        kv_t.append(jnp.where(st_mask, _dot_tn(vb, kd), 0.0))
        eb.append(jnp.exp(bend))
    st = st_ref[...]
    for s in range(nsub):
        y_ref[0, rows[s], :] = o_intra[s] + _dot_nt(qd[s], st.astype(BF16))
        st = st * eb[s] + kv_t[s]
    st_ref[...] = st


def gla_scan(q, k, v, lg, *, t_ctx, nsub=4):
    nseq, t, kw = q.shape
    vw = v.shape[-1]
    blk = GLA_CHUNK * nsub
    assert t % blk == 0 and t_ctx % blk == 0
    shared, per_dir, _ = _scan_specs(nseq, blk, t_ctx // blk, t // blk, [kw, kw, vw], [kw, vw])
    return pl.pallas_call(
        functools.partial(_gla_kernel, nsub=nsub, nseq=nseq),
        grid=(2 * nseq, t // blk),
        in_specs=shared + per_dir[:1],
        out_specs=per_dir[1],
        out_shape=jax.ShapeDtypeStruct((2 * nseq, t, vw), F32),
        scratch_shapes=[pltpu.VMEM((GLA_VW, GLA_KW), F32)],
        compiler_params=pltpu.CompilerParams(dimension_semantics=("parallel", "arbitrary"),
                                             vmem_limit_bytes=VMEM_LIMIT),
        name="gla_scan",
    )(q, k, v, lg)


_HP = lax.Precision.HIGHEST


def _cat(zc, zl):
    return jnp.concatenate([zc, zl], 1)


def _sum_dirs(y, t_ctx):
    nb = y.shape[0] // 2
    y = y[:nb] + y[nb:]
    return y[:, :t_ctx], y[:, t_ctx:]


def _token_shift(u, mu_prev, mu_next):
    pad = jnp.zeros_like(u[:, :1])
    prev = jnp.concatenate([pad, u[:, :-1]], 1)
    nxt = jnp.concatenate([u[:, 1:], pad], 1)
    return u + mu_prev * (prev - u) + mu_next * (nxt - u)


def _rw_heads(z):
    return z.reshape(z.shape[:2] + (RW_HEADS, RW_HD))


def _rwkv_group(u_c, u_l, mu_prev, mu_next, w0, w2, a0, a2, g2, k_k, k_a, r_k, gn_g, gn_b, ctx_out):
    splits = [RW_W, 2 * RW_W, 3 * RW_W, 3 * RW_W + RW_DECAY_RANK, 3 * RW_W + RW_DECAY_RANK + RW_ICLR_RANK]
    t_ctx = u_c.shape[1]

    def prep(u):
        r, k, v, wl, al, gl = jnp.split(_token_shift(u, mu_prev, mu_next), splits, -1)
        return r, k, v, jnp.tanh(wl), al, gl

    def per_dir(p, d):
        _, _, _, wl, al, _ = p
        w_log = -jax.nn.softplus(-(w0[d] + jnp.dot(wl, w2[d], precision=_HP))) - 0.5
        return -jnp.exp(w_log), jax.nn.sigmoid(a0[d] + jnp.dot(al, a2[d], precision=_HP))

    def finish(p, y):
        r, k, v, _, _, gl = p
        bonus = jnp.sum(_rw_heads(r * k * r_k), -1, keepdims=True) * _rw_heads(v)
        yh = _rw_heads(y)
        mu = jnp.mean(yh, -1, keepdims=True)
        var = jnp.mean(jnp.square(yh - mu), -1, keepdims=True)
        yn = ((yh - mu) * lax.rsqrt(var + RW_GN_EPS)).reshape(y.shape) * gn_g + gn_b
        out = yn + bonus.reshape(r.shape)
        return out * jnp.dot(jax.nn.sigmoid(gl), g2, precision=_HP)

    pc, pl_ = prep(u_c), prep(u_l)
    r, k, v = (_cat(pc[j], pl_[j]) for j in range(3))
    kk = _rw_heads(k * k_k)
    kk = (kk * lax.rsqrt(jnp.sum(kk * kk, -1, keepdims=True) + 1e-12)).reshape(k.shape)
    dirs = [[_cat(a, b) for a, b in zip(per_dir(pc, d), per_dir(pl_, d))] for d in (0, 1)]
    lw = jnp.concatenate([dirs[0][0], dirs[1][0]], 0)
    a_sig = jnp.concatenate([dirs[0][1], dirs[1][1]], 0)
    yc, yl = _sum_dirs(rwkv_scan(r, k, v, kk, k_a, lw, a_sig, t_ctx=t_ctx), t_ctx)
    return (finish(pc, yc) if ctx_out else None), finish(pl_, yl)


def _dwconv_grid(u, w, b, rows):
    bsz, n, ch = u.shape
    img = u.reshape(bsz, rows, n // rows, ch)
    out = lax.conv_general_dilated(img, w[:, :, None, :], (1, 1), 'SAME',
                                   dimension_numbers=('NHWC', 'HWIO', 'NHWC'), feature_group_count=ch,
                                   precision=_HP)
    return out.reshape(bsz, n, ch) + b


def rmsnorm_f32(h, g):
    return h * lax.rsqrt(jnp.mean(h * h, -1, keepdims=True) + NORM_EPS) * g


def _ssm_group(zc, xbc_c, dt_c, zl, xbc_l, dt_l, rows, conv_w, conv_b, dt_bias, a_log, d_skip, norm_g, ctx_out):
    t_ctx = zc.shape[1]

    def prep(xbc, grid_rows):
        xbc = jax.nn.silu(_dwconv_grid(xbc, conv_w, conv_b, grid_rows))
        return jnp.split(xbc, [SSM_W, SSM_W + SSM_GROUPS * SSM_STATE], -1)

    def per_dir(dt, d):
        dtp = jax.nn.softplus(dt + dt_bias[d])
        both = jnp.concatenate([-jnp.exp(a_log[d]) * dtp, dtp], -1)
        nb, t = both.shape[:2]
        return jnp.swapaxes(both.reshape(nb, t // SSD_CHUNK, SSD_CHUNK, 2 * SSM_HEADS), 2, 3)

    def finish(z, xs, y):
        y = y + jnp.repeat(d_skip, SSM_HD) * xs
        return rmsnorm_f32(y * jax.nn.silu(z), norm_g)

    xs_c, bm_c, cm_c = prep(xbc_c, 1)
    xs_l, bm_l, cm_l = prep(xbc_l, rows)
    dt = _cat(dt_c, dt_l)
    dd = jnp.concatenate([per_dir(dt, 0), per_dir(dt, 1)], 0)
    y = ssd_scan(_cat(xs_c, xs_l), _cat(bm_c, bm_l), _cat(cm_c, cm_l), dd, t_ctx=t_ctx)
    yc, yl = _sum_dirs(y, t_ctx)
    return (finish(zc, xs_c, yc) if ctx_out else None), finish(zl, xs_l, yl)


def _gla_group(u_c, gl_c, u_l, gl_l, ga2, gb, norm_g, ctx_out):
    t_ctx = u_c.shape[1]
    u = _cat(u_c, u_l)
    gl = _cat(gl_c, gl_l)
    q, k, v, r = jnp.split(u, [GLA_KW, 2 * GLA_KW, 2 * GLA_KW + GLA_VW], -1)
    lg = jnp.concatenate([jax.nn.log_sigmoid(jnp.dot(gl, ga2[d], precision=_HP) + gb[d]) / GLA_TAU for d in (0, 1)], 0)

    def finish(r, y):
        yh = y.reshape(y.shape[:2] + (GLA_HEADS, GLA_DV))
        yn = (yh * lax.rsqrt(jnp.mean(yh * yh, -1, keepdims=True) + NORM_EPS)).reshape(y.shape) * norm_g
        return yn * jax.nn.silu(r)

    yc, yl = _sum_dirs(gla_scan(q, k, v, lg, t_ctx=t_ctx), t_ctx)
    return (finish(r[:, :t_ctx], yc) if ctx_out else None), finish(r[:, t_ctx:], yl)


_IN_WIDTHS = (RW_COLS, SSM_W, SSM_CONV_CH, 2 * GLA_KW + 2 * GLA_VW, SMALL_W)


def _arrange_w_in(w_in):
    rw, ssm, gla = jnp.split(w_in, [RW_COLS, RW_COLS + SSM_COLS], -1)
    z, xbc, dt = jnp.split(ssm, [SSM_W, SSM_W + SSM_CONV_CH], -1)
    qkv, gl, r = jnp.split(gla, [2 * GLA_KW + GLA_VW, 2 * GLA_KW + GLA_VW + GLA_GATE_RANK], -1)
    pad = jnp.zeros((w_in.shape[0], SMALL_W - SSM_HEADS - GLA_GATE_RANK), w_in.dtype)
    return jnp.concatenate([rw, z, xbc, qkv, r, dt, gl, pad], -1).astype(BF16)


def _pad_lanes(w, reps=1):
    w = jnp.repeat(w, reps, axis=-1) if reps > 1 else w
    return jnp.pad(w, [(0, 0)] * (w.ndim - 1) + [(0, LANES - w.shape[-1])])


def kernel(x, c, ctx, c_ctx, ada_w, ada_b, norm1_g, norm2_g, w_in, w_out, rw_mu_prev, rw_mu_next, rw_w0, rw_w2, rw_a0, rw_a2, rw_g2, rw_k_k, rw_k_a, rw_r_k, rw_gn_g, rw_gn_b, ssm_conv_w, ssm_conv_b, ssm_dt_bias, ssm_a_log, ssm_d, ssm_norm_g, gla_ga2, gla_gb, gla_norm_g, moe_rg_w, moe_rg_b, moe_re_w, moe_re_b, moe_w1, moe_w3, moe_w2, final_g):
    depth = ada_w.shape[0]
    rows = x.shape[1] // GRID_W
    cond_l = jax.nn.silu(c)
    cond_c = jax.nn.silu(c_ctx)[None]
    tm = PROJ_TM
    for l in range(depth):
        ctx_out = l < depth - 1
        mod_l = jnp.split(jnp.dot(cond_l, ada_w[l], precision=_HP) + ada_b[l], 6, -1)
        mod_c = jnp.split(jnp.dot(cond_c, ada_w[l], precision=_HP) + ada_b[l], 6, -1)
        w_in_l = _arrange_w_in(w_in[l])
        rl, zl, xbcl, gql, sml = in_proj(x, norm1_g[l], mod_l[0], mod_l[1], w_in_l, _IN_WIDTHS, tm=tm)
        rc, zc, xbcc, gqc, smc = in_proj(ctx, norm1_g[l], mod_c[0], mod_c[1], w_in_l, _IN_WIDTHS, tm=tm)
        dtl, gll = sml[..., :SSM_HEADS], sml[..., SSM_HEADS:SSM_HEADS + GLA_GATE_RANK]
        dtc, glc = smc[..., :SSM_HEADS], smc[..., SSM_HEADS:SSM_HEADS + GLA_GATE_RANK]
        a_c, a_l = _rwkv_group(rc, rl, rw_mu_prev[l], rw_mu_next[l], rw_w0[l], rw_w2[l], rw_a0[l], rw_a2[l],
                               rw_g2[l], rw_k_k[l], rw_k_a[l], rw_r_k[l], rw_gn_g[l], rw_gn_b[l], ctx_out)
        b_c, b_l = _ssm_group(zc, xbcc, dtc, zl, xbcl, dtl, rows, ssm_conv_w[l], ssm_conv_b[l], ssm_dt_bias[l],
                              ssm_a_log[l], ssm_d[l], ssm_norm_g[l], ctx_out)
        g_c, g_l = _gla_group(gqc, glc, gql, gll, gla_ga2[l], gla_gb[l], gla_norm_g[l], ctx_out)
        w_out_l = w_out[l].astype(BF16)
        rgw = _pad_lanes(moe_rg_w[l], EXPERTS_PER_GROUP)
        rgb = _pad_lanes(moe_rg_b[l][None], EXPERTS_PER_GROUP)
        rew = _pad_lanes(moe_re_w[l])
        reb = _pad_lanes(moe_re_b[l][None])
        w1, w3, w2 = moe_w1[l].astype(BF16), moe_w3[l].astype(BF16), moe_w2[l].astype(BF16)
        x = out_proj(jnp.concatenate([a_l, b_l, g_l], -1), w_out_l, x, mod_l[2], tm=tm)
        x = moe_block(x, norm2_g[l], mod_l[3], mod_l[4], mod_l[5], rgw, rgb, rew, reb, w1, w3, w2, tm=MOE_TM)
        if ctx_out:
            ctx = out_proj(jnp.concatenate([a_c, b_c, g_c], -1), w_out_l, ctx, mod_c[2], tm=tm)
            ctx = moe_block(ctx, norm2_g[l], mod_c[3], mod_c[4], mod_c[5], rgw, rgb, rew, reb, w1, w3, w2, tm=MOE_TM)
    return rmsnorm_f32(x, final_g)
```

```python
import functools

import jax
import jax.numpy as jnp
from jax import lax
from jax.experimental import pallas as pl
from jax.experimental.pallas import tpu as pltpu

F32 = jnp.float32
BF16 = jnp.bfloat16

D_MODEL = 1024
GRID_W = 64
NORM_EPS = 1e-6

RW_HEADS = 4
RW_HD = 64
RW_W = RW_HEADS * RW_HD
RW_DECAY_RANK = 64
RW_ICLR_RANK = 64
RW_GATE_RANK = 128
RW_GN_EPS = 64e-5
RW_COLS = 3 * RW_W + RW_DECAY_RANK + RW_ICLR_RANK + RW_GATE_RANK

SSM_HEADS = 8
SSM_HD = 64
SSM_W = SSM_HEADS * SSM_HD
SSM_GROUPS = 2
SSM_STATE = 64
SSM_CONV_CH = SSM_W + 2 * SSM_GROUPS * SSM_STATE
SSM_COLS = SSM_W + SSM_CONV_CH + SSM_HEADS

GLA_HEADS = 4
GLA_DK = 32
GLA_DV = 64
GLA_KW = GLA_HEADS * GLA_DK
GLA_VW = GLA_HEADS * GLA_DV
GLA_GATE_RANK = 16
GLA_TAU = 16.0
GLA_COLS = 2 * GLA_KW + GLA_VW + GLA_GATE_RANK + GLA_VW

N_GROUPS = 4
EXPERTS_PER_GROUP = 4
N_EXPERTS = N_GROUPS * EXPERTS_PER_GROUP
D_EXPERT = 512

LANES = 128
SMALL_W = LANES
VMEM_LIMIT = 56 * 1024 * 1024

PROJ_TM = 512
MOE_TM = 1024
MOE_SUB = 128
MOE_SCATTER_COLS = 256

RW_CHUNK = 64
SSD_CHUNK = 128
GLA_CHUNK = 64

_NT = (((1,), (1,)), ((), ()))
_TN = (((0,), (0,)), ((), ()))


def _dot(a, b):
    return jnp.dot(a, b, preferred_element_type=F32)


def _dot_nt(a, b):
    return lax.dot_general(a, b, _NT, preferred_element_type=F32)


def _dot_tn(a, b):
    return lax.dot_general(a, b, _TN, preferred_element_type=F32)


def _split3(x):
    hi = x.astype(BF16)
    r1 = x - hi.astype(F32)
    mid = r1.astype(BF16)
    lo = (r1 - mid.astype(F32)).astype(BF16)
    return hi, mid, lo


def _cumsum_rows(tri, x):
    hi, mid, lo = _split3(x)
    return _dot(tri, hi) + _dot(tri, mid) + _dot(tri, lo)


def _norm_mod(x, g, shift, scale):
    h = x * lax.rsqrt(jnp.mean(x * x, -1, keepdims=True) + NORM_EPS) * g
    return h * (1.0 + scale) + shift


def _in_proj_kernel(x_ref, g_ref, sh_ref, sc_ref, w_ref, *out_refs, widths):
    h = _norm_mod(x_ref[...], g_ref[...], sh_ref[0], sc_ref[0]).astype(BF16)
    off = 0
    for o_ref, wd in zip(out_refs, widths):
        o_ref[...] = _dot(h, w_ref[:, off:off + wd])
        off += wd


def _mod_spec(mod, tiles_per_batch):
    d = mod.shape[-1]
    if mod.shape[0] == 1:
        return pl.BlockSpec((1, 1, d), lambda i: (0, 0, 0))
    return pl.BlockSpec((1, 1, d), lambda i: (i // tiles_per_batch, 0, 0))


def in_proj(x, g, shift, scale, w, widths, *, tm):
    bsz, t, d = x.shape
    n = bsz * t
    tm = min(tm, t)
    assert t % tm == 0
    tpb = t // tm
    x2 = x.reshape(n, d)
    outs = pl.pallas_call(
        functools.partial(_in_proj_kernel, widths=widths),
        grid=(n // tm,),
        in_specs=[
            pl.BlockSpec((tm, d), lambda i: (i, 0)),
            pl.BlockSpec((1, d), lambda i: (0, 0)),
            _mod_spec(shift[:, None], tpb),
            _mod_spec(scale[:, None], tpb),
            pl.BlockSpec(w.shape, lambda i: (0, 0)),
        ],
        out_specs=[pl.BlockSpec((tm, wd), lambda i: (i, 0)) for wd in widths],
        out_shape=[jax.ShapeDtypeStruct((n, wd), F32) for wd in widths],
        compiler_params=pltpu.CompilerParams(dimension_semantics=("parallel",), vmem_limit_bytes=VMEM_LIMIT),
        name="in_proj",
    )(x2, g[None], shift[:, None], scale[:, None], w)
    return [o.reshape(bsz, t, wd) for o, wd in zip(outs, widths)]


def _out_proj_kernel(m_ref, w_ref, x_ref, gate_ref, o_ref):
    acc = _dot(m_ref[...].astype(BF16), w_ref[...])
    o_ref[...] = x_ref[...] + gate_ref[0] * acc


def out_proj(m, w, x, gate, *, tm):
    bsz, t, d = x.shape
    n = bsz * t
    tm = min(tm, t)
    assert t % tm == 0
    tpb = t // tm
    kdim = m.shape[-1]
    out = pl.pallas_call(
        _out_proj_kernel,
        grid=(n // tm,),
        in_specs=[
            pl.BlockSpec((tm, kdim), lambda i: (i, 0)),
            pl.BlockSpec(w.shape, lambda i: (0, 0)),
            pl.BlockSpec((tm, d), lambda i: (i, 0)),
            _mod_spec(gate[:, None], tpb),
        ],
        out_specs=pl.BlockSpec((tm, d), lambda i: (i, 0)),
        out_shape=jax.ShapeDtypeStruct((n, d), F32),
        compiler_params=pltpu.CompilerParams(dimension_semantics=("parallel",), vmem_limit_bytes=VMEM_LIMIT),
        name="out_proj",
    )(m.reshape(n, kdim), w, x.reshape(n, d), gate[:, None])
    return out.reshape(bsz, t, d)


def _route(logits):
    lane = lax.broadcasted_iota(jnp.int32, logits.shape, 1)
    valid = lane < N_EXPERTS
    neg = -jnp.inf
    big = jnp.int32(1 << 20)
    el = pltpu.roll(logits, LANES - N_EXPERTS, 1)
    glm = jnp.where(valid, logits, neg)
    gmax = jnp.max(glm, -1, keepdims=True)
    g_sel_lane = jnp.min(jnp.where(glm == gmax, lane, big), -1, keepdims=True)
    g_sel = g_sel_lane // EXPERTS_PER_GROUP
    in_group = (lane // EXPERTS_PER_GROUP) == g_sel
    gsum = jnp.sum(jnp.where(valid, jnp.exp(glm - gmax), 0.0), -1, keepdims=True) / EXPERTS_PER_GROUP
    p_group = 1.0 / gsum
    elm = jnp.where(in_group & valid, el, neg)
    m1 = jnp.max(elm, -1, keepdims=True)
    i1 = jnp.min(jnp.where(elm == m1, lane, big), -1, keepdims=True)
    elm2 = jnp.where(lane == i1, neg, elm)
    m2 = jnp.max(elm2, -1, keepdims=True)
    i2 = jnp.min(jnp.where(elm2 == m2, lane, big), -1, keepdims=True)
    p2 = jnp.exp(m2 - m1)
    wa = p_group / (1.0 + p2)
    wb = p_group * p2 / (1.0 + p2)
    return jnp.where(lane == i1, wa, 0.0) + jnp.where(lane == i2, wb, 0.0), g_sel


def _moe_kernel(x_ref, g_ref, sh_ref, sc_ref, gate_ref, rw_ref, rb_ref, w1_ref, w3_ref, w2_ref,
                o_ref, h_ref, comb_ref, code_ref):
    grp = pl.program_id(1)
    tm, d = x_ref.shape

    @pl.when(grp == 0)
    def _():
        h = _norm_mod(x_ref[...], g_ref[...], sh_ref[0], sc_ref[0])
        h_ref[...] = h.astype(BF16)
        h_hi, h_mid, _ = _split3(h)
        w_hi, w_mid, _ = _split3(rw_ref[...])
        logits = _dot(h_hi, w_hi) + _dot(h_hi, w_mid) + _dot(h_mid, w_hi) + rb_ref[...]
        comb, g_sel = _route(logits)
        comb_ref[...] = comb
        lane = lax.broadcasted_iota(jnp.int32, (tm, LANES), 1)
        member = jnp.where(lane == g_sel, 1.0, 0.0)
        r_t = lax.broadcasted_iota(jnp.int32, (tm, tm), 0)
        c_t = lax.broadcasted_iota(jnp.int32, (tm, tm), 1)
        before = jnp.where(c_t < r_t, 1.0, 0.0).astype(BF16)
        rank = _dot(before, member.astype(BF16))
        code = jnp.where(member > 0.0, rank, -1.0)
        code_ref[...] = code.T[:code_ref.shape[0]]
        o_ref[...] = jnp.zeros_like(o_ref)

    crow = code_ref[pl.ds(grp, 1), :]
    count = jnp.max(crow).astype(jnp.int32) + 1
    sub_i = lax.broadcasted_iota(jnp.int32, (MOE_SUB, tm), 0).astype(F32)
    comb_parts = _split3(comb_ref[...])
    lane_c = lax.broadcasted_iota(jnp.int32, (MOE_SUB, LANES), 1)

    def body(j, carry):
        base = (j * MOE_SUB).astype(F32)
        sel = jnp.where(crow - base == sub_i, 1.0, 0.0).astype(BF16)
        hg = _dot(sel, h_ref[...]).astype(BF16)
        cg = sum(_dot(sel, p) for p in comb_parts)
        ysum = jnp.zeros((MOE_SUB, d), F32)
        for e in range(EXPERTS_PER_GROUP):
            a = _dot(hg, w1_ref[0, e])
            b = _dot(hg, w3_ref[0, e])
            hid = (a * jax.nn.sigmoid(a) * b).astype(BF16)
            col = jnp.sum(jnp.where(lane_c == grp * EXPERTS_PER_GROUP + e, cg, 0.0), -1, keepdims=True)
            ysum = ysum + col * _dot(hid, w2_ref[0, e])
        y_hi = ysum.astype(BF16)
        y_lo = (ysum - y_hi.astype(F32)).astype(BF16)
        for c0 in range(0, d, MOE_SCATTER_COLS):
            cs = slice(c0, c0 + MOE_SCATTER_COLS)
            o_ref[:, cs] += _dot_tn(sel, y_hi[:, cs]) + _dot_tn(sel, y_lo[:, cs])
        return carry

    lax.fori_loop(0, (count + MOE_SUB - 1) // MOE_SUB, body, 0)

    @pl.when(grp == N_GROUPS - 1)
    def _():
        o_ref[...] = x_ref[...] + gate_ref[0] * o_ref[...]


def moe_block(x, g, shift, scale, gate, rw, rb, w1, w3, w2, *, tm):
    bsz, t, d = x.shape
    n = bsz * t
    tm = min(tm, t)
    assert t % tm == 0 and tm % MOE_SUB == 0 and d % MOE_SCATTER_COLS == 0
    tpb = t // tm
    epg = EXPERTS_PER_GROUP

    def mod_spec(mod):
        if mod.shape[0] == 1:
            return pl.BlockSpec((1, 1, d), lambda i, e: (0, 0, 0))
        return pl.BlockSpec((1, 1, d), lambda i, e: (i // tpb, 0, 0))

    const = lambda shape: pl.BlockSpec(shape, lambda i, e: (0,) * len(shape))
    out = pl.pallas_call(
        _moe_kernel,
        grid=(n // tm, N_GROUPS),
        in_specs=[
            pl.BlockSpec((tm, d), lambda i, e: (i, 0)),
            const((1, d)),
            mod_spec(shift[:, None]), mod_spec(scale[:, None]), mod_spec(gate[:, None]),
            const(rw.shape), const(rb.shape),
            pl.BlockSpec((1, epg, d, D_EXPERT), lambda i, e: (e, 0, 0, 0)),
            pl.BlockSpec((1, epg, d, D_EXPERT), lambda i, e: (e, 0, 0, 0)),
            pl.BlockSpec((1, epg, D_EXPERT, d), lambda i, e: (e, 0, 0, 0)),
        ],
        out_specs=pl.BlockSpec((tm, d), lambda i, e: (i, 0)),
        out_shape=jax.ShapeDtypeStruct((n, d), F32),
        scratch_shapes=[pltpu.VMEM((tm, d), BF16), pltpu.VMEM((tm, LANES), F32), pltpu.VMEM((8, tm), F32)],
        compiler_params=pltpu.CompilerParams(dimension_semantics=("parallel", "arbitrary"),
                                             vmem_limit_bytes=VMEM_LIMIT),
        name="moe",
    )(x.reshape(n, d), g[None], shift[:, None], scale[:, None], gate[:, None], rw, rb, w1, w3, w2)
    return out.reshape(bsz, t, d)


def _scan_masks(n, rev):
    row = lax.broadcasted_iota(jnp.int32, (n, n), 0)
    col = lax.broadcasted_iota(jnp.int32, (n, n), 1)
    d = (row - col) * jnp.where(rev, -1, 1)
    return d >= 0, d > 0


def _chunk_rows(s, nsub, size, rev):
    return pl.ds(pl.multiple_of(jnp.where(rev, (nsub - 1 - s) * size, s * size), size), size)


def _last_row(x, rev):
    n = x.shape[0]
    return jnp.where(rev, x[0:1], x[n - 1:n])


def _scan_specs(nseq, blk, n_ctx_blk, n_blk, widths_shared, widths_dir):
    def tblock(i, c):
        back = jnp.where(c < n_ctx_blk, n_ctx_blk - 1 - c, n_ctx_blk + n_blk - 1 - c)
        return jnp.where(i >= nseq, back, c)
    shared = [pl.BlockSpec((1, blk, w), lambda i, c: (i % nseq, tblock(i, c), 0)) for w in widths_shared]
    per_dir = [pl.BlockSpec((1, blk, w), lambda i, c: (i, tblock(i, c), 0)) for w in widths_dir]
    return shared, per_dir, tblock


def _rwkv_kernel(r_ref, k_ref, v_ref, kk_ref, ka_ref, lw_ref, as_ref, y_ref, st_ref, *, nsub, nseq):
    L = RW_CHUNK
    hd = RW_HD
    rev = pl.program_id(0) >= nseq

    @pl.when(pl.program_id(1) == 0)
    def _():
        st_ref[...] = jnp.zeros_like(st_ref)

    incl, strict = _scan_masks(L, rev)
    tri = jnp.where(incl, 1.0, 0.0).astype(BF16)
    zeros = jnp.zeros((L, hd), BF16)
    chains = [(s, h) for s in range(nsub) for h in range(RW_HEADS)]
    rows = [_chunk_rows(s, nsub, L, rev) for s in range(nsub)]

    prep = []
    for s in range(nsub):
        lw = lw_ref[0, rows[s], :]
        c = _cumsum_rows(tri, lw)
        c_end = _last_row(c, rev)
        e_inv = jnp.exp(-c)
        e_end = jnp.exp(c_end - c)
        a_sig = as_ref[0, rows[s], :]
        kk = kk_ref[0, rows[s], :]
        k = k_ref[0, rows[s], :]
        kmod = k * (1.0 + (a_sig - 1.0) * ka_ref[...])
        bv = kk * a_sig
        prep.append(dict(
            at=-kk * jnp.exp(c - lw), rt=r_ref[0, rows[s], :] * jnp.exp(c),
            bt=(bv * e_inv).astype(BF16), kt=(kmod * e_inv).astype(BF16),
            bh=(bv * e_end).astype(BF16), kh=(kmod * e_end).astype(BF16),
            p_end=jnp.exp(c_end), v=v_ref[0, rows[s], :].astype(BF16)))

    def head(name, s, h):
        return prep[s][name][:, h * hd:(h + 1) * hd]

    sc = {}
    for s, h in chains:
        lhs = jnp.concatenate([head("at", s, h), head("rt", s, h)], 0).astype(BF16)
        rhs = jnp.concatenate([head("bt", s, h), head("kt", s, h)], 0)
        sc[s, h] = _dot_nt(lhs, rhs)
    nmat, x, mr = {}, {}, {}
    for s, h in chains:
        m = sc[s, h]
        nmat[s, h] = jnp.where(strict, m[:L, :L], 0.0)
        mak = jnp.where(strict, m[:L, L:], 0.0).astype(BF16)
        mr[s, h] = jnp.concatenate([jnp.where(incl, m[L:, :L], 0.0), jnp.where(incl, m[L:, L:], 0.0)], 1).astype(BF16)
        x[s, h] = jnp.concatenate([head("at", s, h), _dot(mak, head("v", s, h))], 1)
    steps = L.bit_length() - 1
    for i in range(steps):
        for s, h in chains:
            nb = nmat[s, h].astype(BF16)
            xb = x[s, h].astype(BF16)
            if i + 1 < steps:
                prod = _dot(nb, jnp.concatenate([xb, nb], 1))
                x[s, h] = x[s, h] + prod[:, :2 * hd]
                nmat[s, h] = prod[:, 2 * hd:]
            else:
                x[s, h] = x[s, h] + _dot(nb, xb)
    ftop, bhw_t, gt = {}, {}, {}
    for s, h in chains:
        xb = x[s, h].astype(BF16)
        vb = head("v", s, h)
        z = jnp.concatenate([xb, jnp.concatenate([zeros, vb], 1)], 0)
        ftop[s, h] = _dot(mr[s, h], z)
        t1 = _dot_tn(xb, head("bh", s, h))
        bhw_t[s, h] = t1[:hd].astype(BF16)
        gt[s, h] = t1[hd:] + _dot_tn(vb, head("kh", s, h))
    st = [st_ref[h] for h in range(RW_HEADS)]
    for s in range(nsub):
        ys = []
        for h in range(RW_HEADS):
            stb = st[h].astype(BF16)
            q = (head("rt", s, h) + ftop[s, h][:, :hd]).astype(BF16)
            ys.append(_dot_nt(q, stb) + ftop[s, h][:, hd:])
            st[h] = st[h] * head("p_end", s, h) + _dot(stb, bhw_t[s, h]) + gt[s, h]
        y_ref[0, rows[s], :] = jnp.concatenate(ys, 1)
    for h in range(RW_HEADS):
        st_ref[h] = st[h]


def rwkv_scan(r, k, v, kk, k_a, lw, a_sig, *, t_ctx, nsub=4):
    nseq, t, w = r.shape
    blk = RW_CHUNK * nsub
    assert t % blk == 0 and t_ctx % blk == 0
    shared, per_dir, _ = _scan_specs(nseq, blk, t_ctx // blk, t // blk, [w] * 4, [w] * 3)
    return pl.pallas_call(
        functools.partial(_rwkv_kernel, nsub=nsub, nseq=nseq),
        grid=(2 * nseq, t // blk),
        in_specs=shared + [pl.BlockSpec((1, w), lambda i, c: (0, 0))] + per_dir[:2],
        out_specs=per_dir[2],
        out_shape=jax.ShapeDtypeStruct((2 * nseq, t, w), F32),
        scratch_shapes=[pltpu.VMEM((RW_HEADS, RW_HD, RW_HD), F32)],
        compiler_params=pltpu.CompilerParams(dimension_semantics=("parallel", "arbitrary"),
                                             vmem_limit_bytes=VMEM_LIMIT),
        name="rwkv_scan",
    )(r, k, v, kk, k_a[None], lw, a_sig)


def _expand_cols(x, e):
    return sum(_dot(p, e) for p in _split3(x))


def _ssd_kernel(x_ref, b_ref, c_ref, dd_ref, y_ref, st_ref, *, nsub, nseq):
    L = SSD_CHUNK
    nh = SSM_HEADS
    pw = 2 * SSM_HD
    npairs = nh // 2
    pairs_per_group = npairs // SSM_GROUPS
    rev = pl.program_id(0) >= nseq

    @pl.when(pl.program_id(1) == 0)
    def _():
        st_ref[...] = jnp.zeros_like(st_ref)

    incl, _ = _scan_masks(L, rev)
    tri = jnp.where(incl, 1.0, 0.0).astype(BF16)
    h_i = lax.broadcasted_iota(jnp.int32, (nh, nh * SSM_HD), 0)
    c_i = lax.broadcasted_iota(jnp.int32, (nh, nh * SSM_HD), 1)
    e_head = jnp.where(c_i // SSM_HD == h_i, 1.0, 0.0).astype(BF16)
    h_j = lax.broadcasted_iota(jnp.int32, (nh, nh * L), 0)
    c_j = lax.broadcasted_iota(jnp.int32, (nh, nh * L), 1)
    e_wide = jnp.where(c_j // L == h_j, 1.0, 0.0).astype(BF16)
    lane_p = lax.broadcasted_iota(jnp.int32, (L, pw), 1)
    first_half = lane_p < SSM_HD
    r_bd = lax.broadcasted_iota(jnp.int32, (pw, pw), 0)
    c_bd = lax.broadcasted_iota(jnp.int32, (pw, pw), 1)
    bd_mask = (r_bd < SSM_HD) == (c_bd < SSM_HD)

    rows = [_chunk_rows(s, nsub, L, rev) for s in range(nsub)]
    work = []
    for s in range(nsub):
        dd = dd_ref[0, jnp.where(rev, nsub - 1 - s, s)]
        da_parts = _split3(dd[:nh])
        acs_row = sum(_dot_nt(p, tri) for p in da_parts)
        acs_col = sum(_dot_nt(tri, p) for p in da_parts)
        colx = _expand_cols(acs_col, e_wide)
        dtx = sum(_dot_tn(p, e_head) for p in _split3(dd[nh:]))
        xdt = (x_ref[0, rows[s], :] * dtx).astype(BF16)
        bm = b_ref[0, rows[s], :]
        cm = c_ref[0, rows[s], :]
        bm_sw = pltpu.roll(bm, SSM_STATE, 1)
        cm_sw = pltpu.roll(cm, SSM_STATE, 1)
        b2 = [jnp.where(first_half, bm, bm_sw), jnp.where(first_half, bm_sw, bm)]
        c2 = [jnp.where(first_half, cm, cm_sw), jnp.where(first_half, cm_sw, cm)]
        cb = [_dot_nt(cm[:, g * SSM_STATE:(g + 1) * SSM_STATE].astype(BF16),
                      bm[:, g * SSM_STATE:(g + 1) * SSM_STATE].astype(BF16)) for g in range(SSM_GROUPS)]
        work.append(dict(acs_row=acs_row, colx=colx, xdt=xdt, b2=b2, c2=c2, cb=cb))

    y_diag, ce, new, ea = {}, {}, {}, {}
    for s in range(nsub):
        w = work[s]
        for p in range(npairs):
            g = p // pairs_per_group
            h0, h1 = 2 * p, 2 * p + 1
            cx0 = w["colx"][:, h0 * L:(h0 + 1) * L]
            cx1 = w["colx"][:, h1 * L:(h1 + 1) * L]
            s0 = w["cb"][g] * jnp.exp(jnp.where(incl, cx0 - w["acs_row"][h0:h0 + 1, :], -jnp.inf))
            s1 = w["cb"][g] * jnp.exp(jnp.where(incl, cx1 - w["acs_row"][h1:h1 + 1, :], -jnp.inf))
            xp = w["xdt"][:, p * pw:(p + 1) * pw]
            zero = jnp.zeros_like(xp)
            x_bd = jnp.concatenate([jnp.where(first_half, xp, zero), jnp.where(first_half, zero, xp)], 0)
            y_diag[s, p] = _dot(jnp.concatenate([s0, s1], 1).astype(BF16), x_bd)
            col = jnp.where(first_half, cx0[:, :pw], cx1[:, :pw])
            a_end = _last_row(col, rev)
            ce[s, p] = (w["c2"][g] * jnp.exp(col)).astype(BF16)
            be = (w["b2"][g] * jnp.exp(a_end - col)).astype(BF16)
            new[s, p] = jnp.where(bd_mask, _dot_tn(be, xp), 0.0)
            ea[s, p] = jnp.exp(a_end)
    st = [st_ref[p] for p in range(npairs)]
    for s in range(nsub):
        ys = []
        for p in range(npairs):
            ys.append(y_diag[s, p] + _dot(ce[s, p], st[p].astype(BF16)))
            st[p] = st[p] * ea[s, p] + new[s, p]
        y_ref[0, rows[s], :] = jnp.concatenate(ys, 1)
    for p in range(npairs):
        st_ref[p] = st[p]


def ssd_scan(xs, bm, cm, dd, *, t_ctx, nsub=2):
    nseq, t, w = xs.shape
    blk = SSD_CHUNK * nsub
    assert t % blk == 0 and t_ctx % blk == 0
    assert SSM_STATE == SSM_HD and SSD_CHUNK == 2 * SSM_HD
    gw = bm.shape[-1]
    shared, per_dir, tblock = _scan_specs(nseq, blk, t_ctx // blk, t // blk, [w, gw, gw], [w])
    dd_spec = pl.BlockSpec((1, nsub, 2 * SSM_HEADS, SSD_CHUNK), lambda i, c: (i, tblock(i, c), 0, 0))
    return pl.pallas_call(
        functools.partial(_ssd_kernel, nsub=nsub, nseq=nseq),
        grid=(2 * nseq, t // blk),
        in_specs=shared + [dd_spec],
        out_specs=per_dir[0],
        out_shape=jax.ShapeDtypeStruct((2 * nseq, t, w), F32),
        scratch_shapes=[pltpu.VMEM((SSM_HEADS // 2, 2 * SSM_STATE, 2 * SSM_HD), F32)],
        compiler_params=pltpu.CompilerParams(dimension_semantics=("parallel", "arbitrary"),
                                             vmem_limit_bytes=VMEM_LIMIT),
        name="ssd_scan",
    )(xs, bm, cm, dd)


def _gla_kernel(q_ref, k_ref, v_ref, lg_ref, y_ref, st_ref, *, nsub, nseq):
    L = GLA_CHUNK
    rev = pl.program_id(0) >= nseq

    @pl.when(pl.program_id(1) == 0)
    def _():
        st_ref[...] = jnp.zeros_like(st_ref)

    nh = GLA_HEADS
    incl, _ = _scan_masks(L, rev)
    tri = jnp.where(incl, 1.0, 0.0).astype(BF16)
    incl_h = jnp.concatenate([incl] * nh, 0)
    r_q = lax.broadcasted_iota(jnp.int32, (nh * L, GLA_KW), 0)
    c_q = lax.broadcasted_iota(jnp.int32, (nh * L, GLA_KW), 1)
    q_mask = r_q // L == c_q // GLA_DK
    r_s = lax.broadcasted_iota(jnp.int32, (GLA_VW, GLA_KW), 0)
    c_s = lax.broadcasted_iota(jnp.int32, (GLA_VW, GLA_KW), 1)
    st_mask = r_s // GLA_DV == c_s // GLA_DK
    lane_v = lax.broadcasted_iota(jnp.int32, (L, GLA_VW), 1) // GLA_DV
    rows = [_chunk_rows(s, nsub, L, rev) for s in range(nsub)]

    o_intra, qd, kv_t, eb = [], [], [], []
    for s in range(nsub):
        bcs = _cumsum_rows(tri, lg_ref[0, rows[s], :])
        mid = jnp.where(rev, bcs[L - 1 - L // 2:L - L // 2], bcs[L // 2:L // 2 + 1])
        bend = _last_row(bcs, rev)
        q = q_ref[0, rows[s], :] * GLA_DK ** -0.5
        k = k_ref[0, rows[s], :]
        vb = v_ref[0, rows[s], :].astype(BF16)
        qe = (q * jnp.exp(bcs - mid)).astype(BF16)
        ke = (k * jnp.exp(mid - bcs)).astype(BF16)
        q_st = jnp.where(q_mask, jnp.concatenate([qe] * nh, 0), jnp.zeros((), BF16))
        att = jnp.where(incl_h, _dot_nt(q_st, ke), 0.0).astype(BF16)
        full = _dot(att, vb)
        o_intra.append(sum(jnp.where(lane_v == h, full[h * L:(h + 1) * L], 0.0) for h in range(nh)))
        qd.append((q * jnp.exp(bcs)).astype(BF16))
        kd = (k * jnp.exp(bend - bcs)).astype(BF16)
        kv_t.append(jnp.where(st_mask, _dot_tn(vb, kd), 0.0))
        eb.append(jnp.exp(bend))
    st = st_ref[...]
    for s in range(nsub):
        y_ref[0, rows[s], :] = o_intra[s] + _dot_nt(qd[s], st.astype(BF16))
        st = st * eb[s] + kv_t[s]
    st_ref[...] = st


def gla_scan(q, k, v, lg, *, t_ctx, nsub=4):
    nseq, t, kw = q.shape
    vw = v.shape[-1]
    blk = GLA_CHUNK * nsub
    assert t % blk == 0 and t_ctx % blk == 0
    shared, per_dir, _ = _scan_specs(nseq, blk, t_ctx // blk, t // blk, [kw, kw, vw], [kw, vw])
    return pl.pallas_call(
        functools.partial(_gla_kernel, nsub=nsub, nseq=nseq),
        grid=(2 * nseq, t // blk),
        in_specs=shared + per_dir[:1],
        out_specs=per_dir[1],
        out_shape=jax.ShapeDtypeStruct((2 * nseq, t, vw), F32),
        scratch_shapes=[pltpu.VMEM((GLA_VW, GLA_KW), F32)],
        compiler_params=pltpu.CompilerParams(dimension_semantics=("parallel", "arbitrary"),
                                             vmem_limit_bytes=VMEM_LIMIT),
        name="gla_scan",
    )(q, k, v, lg)


_HP = lax.Precision.HIGHEST


def _cat(zc, zl):
    return jnp.concatenate([zc, zl], 1)


def _sum_dirs(y, t_ctx):
    nb = y.shape[0] // 2
    y = y[:nb] + y[nb:]
    return y[:, :t_ctx], y[:, t_ctx:]


def _token_shift(u, mu_prev, mu_next):
    pad = jnp.zeros_like(u[:, :1])
    prev = jnp.concatenate([pad, u[:, :-1]], 1)
    nxt = jnp.concatenate([u[:, 1:], pad], 1)
    return u + mu_prev * (prev - u) + mu_next * (nxt - u)


def _rw_heads(z):
    return z.reshape(z.shape[:2] + (RW_HEADS, RW_HD))


def _rwkv_group(u_c, u_l, mu_prev, mu_next, w0, w2, a0, a2, g2, k_k, k_a, r_k, gn_g, gn_b, ctx_out):
    splits = [RW_W, 2 * RW_W, 3 * RW_W, 3 * RW_W + RW_DECAY_RANK, 3 * RW_W + RW_DECAY_RANK + RW_ICLR_RANK]
    t_ctx = u_c.shape[1]

    def prep(u):
        r, k, v, wl, al, gl = jnp.split(_token_shift(u, mu_prev, mu_next), splits, -1)
        return r, k, v, jnp.tanh(wl), al, gl

    def per_dir(p, d):
        _, _, _, wl, al, _ = p
        w_log = -jax.nn.softplus(-(w0[d] + jnp.dot(wl, w2[d], precision=_HP))) - 0.5
        return -jnp.exp(w_log), jax.nn.sigmoid(a0[d] + jnp.dot(al, a2[d], precision=_HP))

    def finish(p, y):
        r, k, v, _, _, gl = p
        bonus = jnp.sum(_rw_heads(r * k * r_k), -1, keepdims=True) * _rw_heads(v)
        yh = _rw_heads(y)
        mu = jnp.mean(yh, -1, keepdims=True)
        var = jnp.mean(jnp.square(yh - mu), -1, keepdims=True)
        yn = ((yh - mu) * lax.rsqrt(var + RW_GN_EPS)).reshape(y.shape) * gn_g + gn_b
        out = yn + bonus.reshape(r.shape)
        return out * jnp.dot(jax.nn.sigmoid(gl), g2, precision=_HP)

    pc, pl_ = prep(u_c), prep(u_l)
    r, k, v = (_cat(pc[j], pl_[j]) for j in range(3))
    kk = _rw_heads(k * k_k)
    kk = (kk * lax.rsqrt(jnp.sum(kk * kk, -1, keepdims=True) + 1e-12)).reshape(k.shape)
    dirs = [[_cat(a, b) for a, b in zip(per_dir(pc, d), per_dir(pl_, d))] for d in (0, 1)]
    lw = jnp.concatenate([dirs[0][0], dirs[1][0]], 0)
    a_sig = jnp.concatenate([dirs[0][1], dirs[1][1]], 0)
    yc, yl = _sum_dirs(rwkv_scan(r, k, v, kk, k_a, lw, a_sig, t_ctx=t_ctx), t_ctx)
    return (finish(pc, yc) if ctx_out else None), finish(pl_, yl)


def _dwconv_grid(u, w, b, rows):
    bsz, n, ch = u.shape
    img = u.reshape(bsz, rows, n // rows, ch)
    out = lax.conv_general_dilated(img, w[:, :, None, :], (1, 1), 'SAME',
                                   dimension_numbers=('NHWC', 'HWIO', 'NHWC'), feature_group_count=ch,
                                   precision=_HP)
    return out.reshape(bsz, n, ch) + b


def rmsnorm_f32(h, g):
    return h * lax.rsqrt(jnp.mean(h * h, -1, keepdims=True) + NORM_EPS) * g


def _ssm_group(zc, xbc_c, dt_c, zl, xbc_l, dt_l, rows, conv_w, conv_b, dt_bias, a_log, d_skip, norm_g, ctx_out):
    t_ctx = zc.shape[1]

    def prep(xbc, grid_rows):
        xbc = jax.nn.silu(_dwconv_grid(xbc, conv_w, conv_b, grid_rows))
        return jnp.split(xbc, [SSM_W, SSM_W + SSM_GROUPS * SSM_STATE], -1)

    def per_dir(dt, d):
        dtp = jax.nn.softplus(dt + dt_bias[d])
        both = jnp.concatenate([-jnp.exp(a_log[d]) * dtp, dtp], -1)
        nb, t = both.shape[:2]
        return jnp.swapaxes(both.reshape(nb, t // SSD_CHUNK, SSD_CHUNK, 2 * SSM_HEADS), 2, 3)

    def finish(z, xs, y):
        y = y + jnp.repeat(d_skip, SSM_HD) * xs
        return rmsnorm_f32(y * jax.nn.silu(z), norm_g)

    xs_c, bm_c, cm_c = prep(xbc_c, 1)
    xs_l, bm_l, cm_l = prep(xbc_l, rows)
    dt = _cat(dt_c, dt_l)
    dd = jnp.concatenate([per_dir(dt, 0), per_dir(dt, 1)], 0)
    y = ssd_scan(_cat(xs_c, xs_l), _cat(bm_c, bm_l), _cat(cm_c, cm_l), dd, t_ctx=t_ctx)
    yc, yl = _sum_dirs(y, t_ctx)
    return (finish(zc, xs_c, yc) if ctx_out else None), finish(zl, xs_l, yl)


def _gla_group(u_c, gl_c, u_l, gl_l, ga2, gb, norm_g, ctx_out):
    t_ctx = u_c.shape[1]
    u = _cat(u_c, u_l)
    gl = _cat(gl_c, gl_l)
    q, k, v, r = jnp.split(u, [GLA_KW, 2 * GLA_KW, 2 * GLA_KW + GLA_VW], -1)
    lg = jnp.concatenate([jax.nn.log_sigmoid(jnp.dot(gl, ga2[d], precision=_HP) + gb[d]) / GLA_TAU for d in (0, 1)], 0)

    def finish(r, y):
        yh = y.reshape(y.shape[:2] + (GLA_HEADS, GLA_DV))
        yn = (yh * lax.rsqrt(jnp.mean(yh * yh, -1, keepdims=True) + NORM_EPS)).reshape(y.shape) * norm_g
        return yn * jax.nn.silu(r)

    yc, yl = _sum_dirs(gla_scan(q, k, v, lg, t_ctx=t_ctx), t_ctx)
    return (finish(r[:, :t_ctx], yc) if ctx_out else None), finish(r[:, t_ctx:], yl)


_IN_WIDTHS = (RW_COLS, SSM_W, SSM_CONV_CH, 2 * GLA_KW + 2 * GLA_VW, SMALL_W)


def _arrange_w_in(w_in):
    rw, ssm, gla = jnp.split(w_in, [RW_COLS, RW_COLS + SSM_COLS], -1)
    z, xbc, dt = jnp.split(ssm, [SSM_W, SSM_W + SSM_CONV_CH], -1)
    qkv, gl, r = jnp.split(gla, [2 * GLA_KW + GLA_VW, 2 * GLA_KW + GLA_VW + GLA_GATE_RANK], -1)
    pad = jnp.zeros((w_in.shape[0], SMALL_W - SSM_HEADS - GLA_GATE_RANK), w_in.dtype)
    return jnp.concatenate([rw, z, xbc, qkv, r, dt, gl, pad], -1).astype(BF16)


def _pad_lanes(w, reps=1):
    w = jnp.repeat(w, reps, axis=-1) if reps > 1 else w
    return jnp.pad(w, [(0, 0)] * (w.ndim - 1) + [(0, LANES - w.shape[-1])])


def kernel(x, c, ctx, c_ctx, ada_w, ada_b, norm1_g, norm2_g, w_in, w_out, rw_mu_prev, rw_mu_next, rw_w0, rw_w2, rw_a0, rw_a2, rw_g2, rw_k_k, rw_k_a, rw_r_k, rw_gn_g, rw_gn_b, ssm_conv_w, ssm_conv_b, ssm_dt_bias, ssm_a_log, ssm_d, ssm_norm_g, gla_ga2, gla_gb, gla_norm_g, moe_rg_w, moe_rg_b, moe_re_w, moe_re_b, moe_w1, moe_w3, moe_w2, final_g):
    depth = ada_w.shape[0]
    rows = x.shape[1] // GRID_W
    cond_l = jax.nn.silu(c)
    cond_c = jax.nn.silu(c_ctx)[None]
    tm = PROJ_TM
    for l in range(depth):
        ctx_out = l < depth - 1
        mod_l = jnp.split(jnp.dot(cond_l, ada_w[l], precision=_HP) + ada_b[l], 6, -1)
        mod_c = jnp.split(jnp.dot(cond_c, ada_w[l], precision=_HP) + ada_b[l], 6, -1)
        w_in_l = _arrange_w_in(w_in[l])
        rl, zl, xbcl, gql, sml = in_proj(x, norm1_g[l], mod_l[0], mod_l[1], w_in_l, _IN_WIDTHS, tm=tm)
        rc, zc, xbcc, gqc, smc = in_proj(ctx, norm1_g[l], mod_c[0], mod_c[1], w_in_l, _IN_WIDTHS, tm=tm)
        dtl, gll = sml[..., :SSM_HEADS], sml[..., SSM_HEADS:SSM_HEADS + GLA_GATE_RANK]
        dtc, glc = smc[..., :SSM_HEADS], smc[..., SSM_HEADS:SSM_HEADS + GLA_GATE_RANK]
        a_c, a_l = _rwkv_group(rc, rl, rw_mu_prev[l], rw_mu_next[l], rw_w0[l], rw_w2[l], rw_a0[l], rw_a2[l],
                               rw_g2[l], rw_k_k[l], rw_k_a[l], rw_r_k[l], rw_gn_g[l], rw_gn_b[l], ctx_out)
        b_c, b_l = _ssm_group(zc, xbcc, dtc, zl, xbcl, dtl, rows, ssm_conv_w[l], ssm_conv_b[l], ssm_dt_bias[l],
                              ssm_a_log[l], ssm_d[l], ssm_norm_g[l], ctx_out)
        g_c, g_l = _gla_group(gqc, glc, gql, gll, gla_ga2[l], gla_gb[l], gla_norm_g[l], ctx_out)
        w_out_l = w_out[l].astype(BF16)
        epg = EXPERTS_PER_GROUP
        rw = _pad_lanes(jnp.concatenate([jnp.repeat(moe_rg_w[l], epg, -1), moe_re_w[l]], -1))
        rb = _pad_lanes(jnp.concatenate([jnp.repeat(moe_rg_b[l], epg, -1), moe_re_b[l]], -1)[None])
        by_group = lambda w: w.astype(BF16).reshape((N_GROUPS, epg) + w.shape[1:])
        w1, w3, w2 = by_group(moe_w1[l]), by_group(moe_w3[l]), by_group(moe_w2[l])
        x = out_proj(jnp.concatenate([a_l, b_l, g_l], -1), w_out_l, x, mod_l[2], tm=tm)
        x = moe_block(x, norm2_g[l], mod_l[3], mod_l[4], mod_l[5], rw, rb, w1, w3, w2, tm=MOE_TM)
        if ctx_out:
            ctx = out_proj(jnp.concatenate([a_c, b_c, g_c], -1), w_out_l, ctx, mod_c[2], tm=tm)
            ctx = moe_block(ctx, norm2_g[l], mod_c[3], mod_c[4], mod_c[5], rw, rb, w1, w3, w2, tm=MOE_TM)
    return rmsnorm_f32(x, final_g)
```

```python
import functools

import jax
import jax.numpy as jnp
from jax import lax
from jax.experimental import pallas as pl
from jax.experimental.pallas import tpu as pltpu

F32 = jnp.float32
BF16 = jnp.bfloat16

D_MODEL = 1024
GRID_W = 64
NORM_EPS = 1e-6

RW_HEADS = 4
RW_HD = 64
RW_W = RW_HEADS * RW_HD
RW_DECAY_RANK = 64
RW_ICLR_RANK = 64
RW_GATE_RANK = 128
RW_GN_EPS = 64e-5
RW_COLS = 3 * RW_W + RW_DECAY_RANK + RW_ICLR_RANK + RW_GATE_RANK

SSM_HEADS = 8
SSM_HD = 64
SSM_W = SSM_HEADS * SSM_HD
SSM_GROUPS = 2
SSM_STATE = 64
SSM_CONV_CH = SSM_W + 2 * SSM_GROUPS * SSM_STATE
SSM_COLS = SSM_W + SSM_CONV_CH + SSM_HEADS

GLA_HEADS = 4
GLA_DK = 32
GLA_DV = 64
GLA_KW = GLA_HEADS * GLA_DK
GLA_VW = GLA_HEADS * GLA_DV
GLA_GATE_RANK = 16
GLA_TAU = 16.0
GLA_COLS = 2 * GLA_KW + GLA_VW + GLA_GATE_RANK + GLA_VW

N_GROUPS = 4
EXPERTS_PER_GROUP = 4
N_EXPERTS = N_GROUPS * EXPERTS_PER_GROUP
D_EXPERT = 512

LANES = 128
SMALL_W = LANES
VMEM_LIMIT = 56 * 1024 * 1024

PROJ_TM = 256
MOE_TM = 1024
MOE_SUB = 128
MOE_SCATTER_COLS = 256

RW_CHUNK = 64
SSD_CHUNK = 128
GLA_CHUNK = 64

_NT = (((1,), (1,)), ((), ()))
_TN = (((0,), (0,)), ((), ()))


def _dot(a, b):
    return jnp.dot(a, b, preferred_element_type=F32)


def _dot_nt(a, b):
    return lax.dot_general(a, b, _NT, preferred_element_type=F32)


def _dot_tn(a, b):
    return lax.dot_general(a, b, _TN, preferred_element_type=F32)


def _split3(x):
    hi = x.astype(BF16)
    r1 = x - hi.astype(F32)
    mid = r1.astype(BF16)
    lo = (r1 - mid.astype(F32)).astype(BF16)
    return hi, mid, lo


def _cumsum_rows(tri, x):
    hi, mid, lo = _split3(x)
    return _dot(tri, hi) + _dot(tri, mid) + _dot(tri, lo)


def _norm_mod(x, g, shift, scale):
    h = x * lax.rsqrt(jnp.mean(x * x, -1, keepdims=True) + NORM_EPS) * g
    return h * (1.0 + scale) + shift


def _in_proj_kernel(x_ref, g_ref, sh_ref, sc_ref, w_ref, *refs, widths):
    out_refs = refs[-len(widths):]
    h = _norm_mod(x_ref[...], g_ref[...], sh_ref[0], sc_ref[0]).astype(BF16)
    off = 0
    for o_ref, wd in zip(out_refs, widths):
        o_ref[0] = _dot(h, w_ref[:, off:off + wd])
        off += wd


def _mod_spec(mod, tiles_per_batch):
    d = mod.shape[-1]
    if mod.shape[0] == 1:
        return pl.BlockSpec((1, 1, d), lambda i: (0, 0, 0))
    return pl.BlockSpec((1, 1, d), lambda i: (i // tiles_per_batch, 0, 0))


def in_proj(x, g, shift, scale, w, widths, *, tm, t_all, t_off, bufs=None):
    bsz, t, d = x.shape
    n = bsz * t
    assert t % tm == 0 and t_off % tm == 0
    tpb = t // tm
    off_blk = t_off // tm
    x2 = x.reshape(n, d)
    ins = [x2, g[None], shift[:, None], scale[:, None], w]
    in_specs = [
        pl.BlockSpec((tm, d), lambda i: (i, 0)),
        pl.BlockSpec((1, d), lambda i: (0, 0)),
        _mod_spec(shift[:, None], tpb),
        _mod_spec(scale[:, None], tpb),
        pl.BlockSpec(w.shape, lambda i: (0, 0)),
    ]
    aliases = {}
    if bufs is not None:
        aliases = {len(ins) + j: j for j in range(len(bufs))}
        in_specs = in_specs + [pl.BlockSpec(memory_space=pl.ANY)] * len(bufs)
        ins = ins + list(bufs)
    return pl.pallas_call(
        functools.partial(_in_proj_kernel, widths=widths),
        grid=(n // tm,),
        in_specs=in_specs,
        out_specs=[pl.BlockSpec((1, tm, wd), lambda i: (i // tpb, off_blk + i % tpb, 0)) for wd in widths],
        out_shape=[jax.ShapeDtypeStruct((bsz, t_all, wd), F32) for wd in widths],
        input_output_aliases=aliases,
        compiler_params=pltpu.CompilerParams(dimension_semantics=("parallel",), vmem_limit_bytes=VMEM_LIMIT),
        name="in_proj",
    )(*ins)


def _dot3(a, b):
    a_hi, a_mid, _ = _split3(a)
    b_hi, b_mid, _ = _split3(b)
    return _dot(a_hi, b_hi) + _dot(a_hi, b_mid) + _dot(a_mid, b_hi)


def _head_ones(width, head):
    r = lax.broadcasted_iota(jnp.int32, (width, width), 0)
    c = lax.broadcasted_iota(jnp.int32, (width, width), 1)
    return jnp.where(r // head == c // head, 1.0, 0.0).astype(BF16)


def _head_sum(x, ones):
    return sum(_dot(p, ones) for p in _split3(x))


def _silu(x):
    return x * jax.nn.sigmoid(x)


def _out_proj_kernel(yr_ref, bonus_ref, rgate_ref, ys_ref, xs_ref, z_ref, yg_ref, gr_ref,
                     gng_ref, gnb_ref, dsk_ref, sng_ref, gng2_ref, w_ref, x_ref, gate_ref, o_ref):
    ones_rw = _head_ones(RW_W, RW_HD)
    yr = yr_ref[0, 0] + yr_ref[1, 0]
    mu = _head_sum(yr, ones_rw) * (1.0 / RW_HD)
    yc = yr - mu
    var = _head_sum(yc * yc, ones_rw) * (1.0 / RW_HD)
    a_out = (yc * lax.rsqrt(var + RW_GN_EPS) * gng_ref[...] + gnb_ref[...] + bonus_ref[0]) * rgate_ref[0]
    ysd = ys_ref[0, 0] + ys_ref[1, 0] + dsk_ref[...] * xs_ref[0]
    t = ysd * _silu(z_ref[0])
    b_out = t * lax.rsqrt(jnp.mean(t * t, -1, keepdims=True) + NORM_EPS) * sng_ref[...]
    yg = yg_ref[0, 0] + yg_ref[1, 0]
    ms = _head_sum(yg * yg, _head_ones(GLA_VW, GLA_DV)) * (1.0 / GLA_DV)
    g_out = yg * lax.rsqrt(ms + NORM_EPS) * gng2_ref[...] * _silu(gr_ref[0])
    m = jnp.concatenate([a_out, b_out, g_out], 1).astype(BF16)
    o_ref[...] = x_ref[...] + gate_ref[0] * _dot(m, w_ref[...])


def out_proj(y_rw, bonus, rgate, y_ssd, conv, z, y_gla, gla_slab, gn_g, gn_b, d_skip, ssm_norm_g, gla_norm_g,
             w, x, gate, *, tm, t_off):
    bsz, t, d = x.shape
    n = bsz * t
    assert t % tm == 0 and t_off % tm == 0
    tpb = t // tm
    ob = t_off // tm
    row = lambda i: (i // tpb, ob + i % tpb)
    both = lambda wd: pl.BlockSpec((2, 1, tm, wd), lambda i: (0,) + row(i) + (0,))
    one = lambda wd, cb=0: pl.BlockSpec((1, tm, wd), lambda i: row(i) + (cb,))
    par = lambda wd: pl.BlockSpec((1, wd), lambda i: (0, 0))
    r_blk = (2 * GLA_KW + GLA_VW) // GLA_VW
    out = pl.pallas_call(
        _out_proj_kernel,
        grid=(n // tm,),
        in_specs=[
            both(RW_W), one(RW_W), one(RW_W),
            both(SSM_W), one(SSM_W), one(SSM_W),
            both(GLA_VW), one(GLA_VW, r_blk),
            par(RW_W), par(RW_W), par(SSM_W), par(SSM_W), par(GLA_VW),
            pl.BlockSpec(w.shape, lambda i: (0, 0)),
            pl.BlockSpec((tm, d), lambda i: (i, 0)),
            _mod_spec(gate[:, None], tpb),
        ],
        out_specs=pl.BlockSpec((tm, d), lambda i: (i, 0)),
        out_shape=jax.ShapeDtypeStruct((n, d), F32),
        compiler_params=pltpu.CompilerParams(dimension_semantics=("parallel",), vmem_limit_bytes=VMEM_LIMIT),
        name="out_proj",
    )(y_rw, bonus, rgate, y_ssd, conv, z, y_gla, gla_slab,
      gn_g[None], gn_b[None], jnp.repeat(d_skip, SSM_HD)[None], ssm_norm_g[None], gla_norm_g[None],
      w, x.reshape(n, d), gate[:, None])
    return out.reshape(bsz, t, d)


def _softplus(x):
    return jnp.maximum(x, 0.0) + jnp.log1p(jnp.exp(-jnp.abs(x)))


def _rwkv_prep_kernel(u_ref, up_ref, un_ref, mup_ref, mun_ref, w2_ref, a2_ref, w0_ref, a0_ref, g2_ref, kk_ref, rk_ref,
                      r_o, k_o, v_o, kn_o, bonus_o, gate_o, lw_o, as_o, *, ctx_blk, n_blk):
    j = pl.program_id(1)
    u = u_ref[0]
    tm = u.shape[0]
    seg_start = jnp.logical_or(j == 0, j == ctx_blk)
    seg_end = jnp.logical_or(j == ctx_blk - 1, j == n_blk - 1)
    prev_row = jnp.where(seg_start, 0.0, up_ref[0, 7:8, :])
    next_row = jnp.where(seg_end, 0.0, un_ref[0, 0:1, :])
    row = lax.broadcasted_iota(jnp.int32, u.shape, 0)
    prev = jnp.where(row == 0, prev_row, pltpu.roll(u, 1, 0))
    nxt = jnp.where(row == tm - 1, next_row, pltpu.roll(u, tm - 1, 0))
    s = u + mup_ref[...] * (prev - u) + mun_ref[...] * (nxt - u)
    r, k, v = s[:, :RW_W], s[:, RW_W:2 * RW_W], s[:, 2 * RW_W:3 * RW_W]
    lora_in = s[:, 3 * RW_W:3 * RW_W + RW_DECAY_RANK + RW_ICLR_RANK]
    gl = s[:, 3 * RW_W + RW_DECAY_RANK + RW_ICLR_RANK:]
    ones = _head_ones(RW_W, RW_HD)
    kr = k * kk_ref[...]
    r_o[0] = r
    k_o[0] = k
    v_o[0] = v
    kn_o[0] = kr * lax.rsqrt(_head_sum(kr * kr, ones) + 1e-12)
    bonus_o[0] = _head_sum(r * k * rk_ref[...], ones) * v
    gate_o[0] = _dot3(jax.nn.sigmoid(gl), g2_ref[...])
    lora_t = jnp.tanh(lora_in)
    for d in range(2):
        w_log = -_softplus(-(w0_ref[d] + _dot3(lora_t, w2_ref[d]))) - 0.5
        lw_o[d, 0] = -jnp.exp(w_log)
        as_o[d, 0] = jax.nn.sigmoid(a0_ref[d] + _dot3(lora_in, a2_ref[d]))


def rwkv_prep(u, mu_prev, mu_next, w0, w2, a0, a2, g2, k_k, r_k, *, t_ctx, tm):
    bsz, t_all, wc = u.shape
    assert t_all % tm == 0 and t_ctx % tm == 0 and tm % 8 == 0
    n_blk = t_all // tm
    h8 = tm // 8
    last8 = t_all // 8 - 1
    zpad = jnp.zeros((2, RW_DECAY_RANK, RW_W), F32)
    w2p = jnp.concatenate([w2, zpad], 1)
    a2p = jnp.concatenate([zpad, a2], 1)
    par = lambda a: pl.BlockSpec(a.shape, lambda b, j: (0,) * a.ndim)
    params = [mu_prev[None], mu_next[None], w2p, a2p, w0[:, None], a0[:, None], g2, k_k[None], r_k[None]]
    o1 = pl.BlockSpec((1, tm, RW_W), lambda b, j: (b, j, 0))
    o2 = pl.BlockSpec((2, 1, tm, RW_W), lambda b, j: (0, b, j, 0))
    s1 = jax.ShapeDtypeStruct((bsz, t_all, RW_W), F32)
    s2 = jax.ShapeDtypeStruct((2, bsz, t_all, RW_W), F32)
    return pl.pallas_call(
        functools.partial(_rwkv_prep_kernel, ctx_blk=t_ctx // tm, n_blk=n_blk),
        grid=(bsz, n_blk),
        in_specs=[
            pl.BlockSpec((1, tm, wc), lambda b, j: (b, j, 0)),
            pl.BlockSpec((1, 8, wc), lambda b, j: (b, jnp.maximum(j * h8 - 1, 0), 0)),
            pl.BlockSpec((1, 8, wc), lambda b, j: (b, jnp.minimum((j + 1) * h8, last8), 0)),
        ] + [par(a) for a in params],
        out_specs=[o1] * 6 + [o2] * 2,
        out_shape=[s1] * 6 + [s2] * 2,
        compiler_params=pltpu.CompilerParams(dimension_semantics=("parallel", "parallel"),
                                             vmem_limit_bytes=VMEM_LIMIT),
        name="rwkv_prep",
    )(u, u, u, *params)


def _ssm_prep_kernel(x_ref, xp_ref, xn_ref, sm_ref, cw_ref, cb_ref, dtb_ref, alog_ref, o_ref, dd_ref,
                     *, t_ctx, t_all, halo):
    j = pl.program_id(1)
    tm = x_ref.shape[1]
    t0 = j * tm
    is_ctx = t0 < t_ctx
    seg_lo = jnp.where(is_ctx, 0, t_ctx)
    seg_hi = jnp.where(is_ctx, t_ctx, t_all)
    ctx_i = jnp.where(is_ctx, 1, 0)
    lat_f = jnp.where(is_ctx, 0.0, 1.0)
    ext = jnp.concatenate([xp_ref[0], x_ref[0], xn_ref[0]], 0)
    n_ext = ext.shape[0]
    reps = ext.shape[1] // LANES
    e_row = lax.broadcasted_iota(jnp.int32, (n_ext, LANES), 0)
    t_src = e_row + (t0 - halo)
    in_seg = jnp.logical_and(t_src >= seg_lo, t_src < seg_hi)
    col = e_row % GRID_W
    keep_left = jnp.logical_and(in_seg, (jnp.where(col != GRID_W - 1, 1, 0) | ctx_i) > 0)
    keep_right = jnp.logical_and(in_seg, (jnp.where(col != 0, 1, 0) | ctx_i) > 0)
    wide = lambda m: jnp.concatenate([m] * reps, 1)
    src = {0: jnp.where(wide(in_seg), ext, 0.0), -1: jnp.where(wide(keep_left), ext, 0.0),
           1: jnp.where(wide(keep_right), ext, 0.0)}
    acc = jnp.zeros((tm, ext.shape[1]), F32) + cb_ref[...]
    for dr in (-1, 0, 1):
        for dc in (-1, 0, 1):
            off = halo + GRID_W * dr + dc
            wt = cw_ref[(dr + 1) * 3 + dc + 1:(dr + 1) * 3 + dc + 2, :]
            if dr != 0:
                wt = wt * lat_f
            acc = acc + src[dc][off:off + tm] * wt
    o_ref[0] = _silu(acc)
    dt = sm_ref[0].T[:SSM_HEADS]
    for d in range(2):
        dtp = _softplus(dt + dtb_ref[d])
        da = -jnp.exp(alog_ref[d]) * dtp
        for c in range(tm // SSD_CHUNK):
            cs = slice(c * SSD_CHUNK, (c + 1) * SSD_CHUNK)
            dd_ref[d, 0, c, :SSM_HEADS, :] = da[:, cs]
            dd_ref[d, 0, c, SSM_HEADS:, :] = dtp[:, cs]


def ssm_prep(xbc, small, conv_w, conv_b, dt_bias, a_log, *, t_ctx, tm):
    bsz, t_all, ch = xbc.shape
    halo = 128
    assert t_all % tm == 0 and t_ctx % tm == 0 and tm % halo == 0 and halo > GRID_W and tm % SSD_CHUNK == 0
    n_blk = t_all // tm
    hb = tm // halo
    last = t_all // halo - 1
    par = lambda a: pl.BlockSpec(a.shape, lambda b, j: (0,) * a.ndim)
    params = [conv_w.reshape(9, ch), conv_b[None], dt_bias[:, :, None], a_log[:, :, None]]
    nc = tm // SSD_CHUNK
    return pl.pallas_call(
        functools.partial(_ssm_prep_kernel, t_ctx=t_ctx, t_all=t_all, halo=halo),
        grid=(bsz, n_blk),
        in_specs=[
            pl.BlockSpec((1, tm, ch), lambda b, j: (b, j, 0)),
            pl.BlockSpec((1, halo, ch), lambda b, j: (b, jnp.maximum(j * hb - 1, 0), 0)),
            pl.BlockSpec((1, halo, ch), lambda b, j: (b, jnp.minimum((j + 1) * hb, last), 0)),
            pl.BlockSpec((1, tm, LANES), lambda b, j: (b, j, 0)),
        ] + [par(a) for a in params],
        out_specs=[pl.BlockSpec((1, tm, ch), lambda b, j: (b, j, 0)),
                   pl.BlockSpec((2, 1, nc, 2 * SSM_HEADS, SSD_CHUNK), lambda b, j: (0, b, j, 0, 0))],
        out_shape=[jax.ShapeDtypeStruct((bsz, t_all, ch), F32),
                   jax.ShapeDtypeStruct((2, bsz, t_all // SSD_CHUNK, 2 * SSM_HEADS, SSD_CHUNK), F32)],
        compiler_params=pltpu.CompilerParams(dimension_semantics=("parallel", "parallel"),
                                             vmem_limit_bytes=VMEM_LIMIT),
        name="ssm_prep",
    )(xbc, xbc, xbc, small, *params)


def _route(logits):
    lane = lax.broadcasted_iota(jnp.int32, logits.shape, 1)
    valid = lane < N_EXPERTS
    neg = -jnp.inf
    big = jnp.int32(1 << 20)
    el = pltpu.roll(logits, LANES - N_EXPERTS, 1)
    glm = jnp.where(valid, logits, neg)
    gmax = jnp.max(glm, -1, keepdims=True)
    g_sel_lane = jnp.min(jnp.where(glm == gmax, lane, big), -1, keepdims=True)
    g_sel = g_sel_lane // EXPERTS_PER_GROUP
    in_group = (lane // EXPERTS_PER_GROUP) == g_sel
    gsum = jnp.sum(jnp.where(valid, jnp.exp(glm - gmax), 0.0), -1, keepdims=True) / EXPERTS_PER_GROUP
    p_group = 1.0 / gsum
    elm = jnp.where(in_group & valid, el, neg)
    m1 = jnp.max(elm, -1, keepdims=True)
    i1 = jnp.min(jnp.where(elm == m1, lane, big), -1, keepdims=True)
    elm2 = jnp.where(lane == i1, neg, elm)
    m2 = jnp.max(elm2, -1, keepdims=True)
    i2 = jnp.min(jnp.where(elm2 == m2, lane, big), -1, keepdims=True)
    p2 = jnp.exp(m2 - m1)
    wa = p_group / (1.0 + p2)
    wb = p_group * p2 / (1.0 + p2)
    return jnp.where(lane == i1, wa, 0.0) + jnp.where(lane == i2, wb, 0.0), g_sel


def _moe_kernel(x_ref, g_ref, sh_ref, sc_ref, gate_ref, rw_ref, rb_ref, w1_ref, w3_ref, w2_ref,
                o_ref, h_ref, comb_ref, code_ref):
    grp = pl.program_id(1)
    tm, d = x_ref.shape

    @pl.when(grp == 0)
    def _():
        h = _norm_mod(x_ref[...], g_ref[...], sh_ref[0], sc_ref[0])
        h_ref[...] = h.astype(BF16)
        h_hi, h_mid, _ = _split3(h)
        w_hi, w_mid, _ = _split3(rw_ref[...])
        logits = _dot(h_hi, w_hi) + _dot(h_hi, w_mid) + _dot(h_mid, w_hi) + rb_ref[...]
        comb, g_sel = _route(logits)
        comb_ref[...] = comb
        lane = lax.broadcasted_iota(jnp.int32, (tm, LANES), 1)
        member = jnp.where(lane == g_sel, 1.0, 0.0)
        r_t = lax.broadcasted_iota(jnp.int32, (tm, tm), 0)
        c_t = lax.broadcasted_iota(jnp.int32, (tm, tm), 1)
        before = jnp.where(c_t < r_t, 1.0, 0.0).astype(BF16)
        rank = _dot(before, member.astype(BF16))
        code = jnp.where(member > 0.0, rank, -1.0)
        code_ref[...] = code.T[:code_ref.shape[0]]
        o_ref[...] = jnp.zeros_like(o_ref)

    crow = code_ref[pl.ds(grp, 1), :]
    count = jnp.max(crow).astype(jnp.int32) + 1
    sub_i = lax.broadcasted_iota(jnp.int32, (MOE_SUB, tm), 0).astype(F32)
    comb_parts = _split3(comb_ref[...])
    lane_c = lax.broadcasted_iota(jnp.int32, (MOE_SUB, LANES), 1)

    def body(j, carry):
        base = (j * MOE_SUB).astype(F32)
        sel = jnp.where(crow - base == sub_i, 1.0, 0.0).astype(BF16)
        hg = _dot(sel, h_ref[...]).astype(BF16)
        cg = sum(_dot(sel, p) for p in comb_parts)
        ysum = jnp.zeros((MOE_SUB, d), F32)
        for e in range(EXPERTS_PER_GROUP):
            a = _dot(hg, w1_ref[0, e])
            b = _dot(hg, w3_ref[0, e])
            hid = (a * jax.nn.sigmoid(a) * b).astype(BF16)
            col = jnp.sum(jnp.where(lane_c == grp * EXPERTS_PER_GROUP + e, cg, 0.0), -1, keepdims=True)
            ysum = ysum + col * _dot(hid, w2_ref[0, e])
        y_hi = ysum.astype(BF16)
        y_lo = (ysum - y_hi.astype(F32)).astype(BF16)
        for c0 in range(0, d, MOE_SCATTER_COLS):
            cs = slice(c0, c0 + MOE_SCATTER_COLS)
            o_ref[:, cs] += _dot_tn(sel, y_hi[:, cs]) + _dot_tn(sel, y_lo[:, cs])
        return carry

    lax.fori_loop(0, (count + MOE_SUB - 1) // MOE_SUB, body, 0)

    @pl.when(grp == N_GROUPS - 1)
    def _():
        o_ref[...] = x_ref[...] + gate_ref[0] * o_ref[...]


def moe_block(x, g, shift, scale, gate, rw, rb, w1, w3, w2, *, tm):
    bsz, t, d = x.shape
    n = bsz * t
    tm = min(tm, t)
    assert t % tm == 0 and tm % MOE_SUB == 0 and d % MOE_SCATTER_COLS == 0
    tpb = t // tm
    epg = EXPERTS_PER_GROUP

    def mod_spec(mod):
        if mod.shape[0] == 1:
            return pl.BlockSpec((1, 1, d), lambda i, e: (0, 0, 0))
        return pl.BlockSpec((1, 1, d), lambda i, e: (i // tpb, 0, 0))

    const = lambda shape: pl.BlockSpec(shape, lambda i, e: (0,) * len(shape))
    out = pl.pallas_call(
        _moe_kernel,
        grid=(n // tm, N_GROUPS),
        in_specs=[
            pl.BlockSpec((tm, d), lambda i, e: (i, 0)),
            const((1, d)),
            mod_spec(shift[:, None]), mod_spec(scale[:, None]), mod_spec(gate[:, None]),
            const(rw.shape), const(rb.shape),
            pl.BlockSpec((1, epg, d, D_EXPERT), lambda i, e: (e, 0, 0, 0)),
            pl.BlockSpec((1, epg, d, D_EXPERT), lambda i, e: (e, 0, 0, 0)),
            pl.BlockSpec((1, epg, D_EXPERT, d), lambda i, e: (e, 0, 0, 0)),
        ],
        out_specs=pl.BlockSpec((tm, d), lambda i, e: (i, 0)),
        out_shape=jax.ShapeDtypeStruct((n, d), F32),
        scratch_shapes=[pltpu.VMEM((tm, d), BF16), pltpu.VMEM((tm, LANES), F32), pltpu.VMEM((8, tm), F32)],
        compiler_params=pltpu.CompilerParams(dimension_semantics=("parallel", "arbitrary"),
                                             vmem_limit_bytes=VMEM_LIMIT),
        name="moe",
    )(x.reshape(n, d), g[None], shift[:, None], scale[:, None], gate[:, None], rw, rb, w1, w3, w2)
    return out.reshape(bsz, t, d)


def _scan_masks(n, rev):
    row = lax.broadcasted_iota(jnp.int32, (n, n), 0)
    col = lax.broadcasted_iota(jnp.int32, (n, n), 1)
    d = (row - col) * jnp.where(rev, -1, 1)
    return d >= 0, d > 0


def _chunk_rows(s, nsub, size, rev):
    return pl.ds(pl.multiple_of(jnp.where(rev, (nsub - 1 - s) * size, s * size), size), size)


def _last_row(x, rev):
    n = x.shape[0]
    return jnp.where(rev, x[0:1], x[n - 1:n])


def _scan_specs(nseq, blk, n_ctx_blk, n_blk, widths_shared, widths_dir):
    def tblock(i, c):
        back = jnp.where(c < n_ctx_blk, n_ctx_blk - 1 - c, n_ctx_blk + n_blk - 1 - c)
        return jnp.where(i >= nseq, back, c)
    def shared_spec(w, col_blk=0):
        return pl.BlockSpec((1, blk, w), lambda i, c: (i % nseq, tblock(i, c), col_blk))
    shared = [shared_spec(*w) if isinstance(w, tuple) else shared_spec(w) for w in widths_shared]
    per_dir = [pl.BlockSpec((1, blk, w), lambda i, c: (i, tblock(i, c), 0)) for w in widths_dir]
    return shared, per_dir, tblock


def _rwkv_kernel(r_ref, k_ref, v_ref, kk_ref, ka_ref, lw_ref, as_ref, y_ref, st_ref, *, nsub, nseq):
    L = RW_CHUNK
    hd = RW_HD
    rev = pl.program_id(0) >= nseq

    @pl.when(pl.program_id(1) == 0)
    def _():
        st_ref[...] = jnp.zeros_like(st_ref)

    incl, strict = _scan_masks(L, rev)
    tri = jnp.where(incl, 1.0, 0.0).astype(BF16)
    zeros = jnp.zeros((L, hd), BF16)
    chains = [(s, h) for s in range(nsub) for h in range(RW_HEADS)]
    rows = [_chunk_rows(s, nsub, L, rev) for s in range(nsub)]

    prep = []
    for s in range(nsub):
        lw = lw_ref[0, rows[s], :]
        c = _cumsum_rows(tri, lw)
        c_end = _last_row(c, rev)
        e_inv = jnp.exp(-c)
        e_end = jnp.exp(c_end - c)
        a_sig = as_ref[0, rows[s], :]
        kk = kk_ref[0, rows[s], :]
        k = k_ref[0, rows[s], :]
        kmod = k * (1.0 + (a_sig - 1.0) * ka_ref[...])
        bv = kk * a_sig
        prep.append(dict(
            at=-kk * jnp.exp(c - lw), rt=r_ref[0, rows[s], :] * jnp.exp(c),
            bt=(bv * e_inv).astype(BF16), kt=(kmod * e_inv).astype(BF16),
            bh=(bv * e_end).astype(BF16), kh=(kmod * e_end).astype(BF16),
            p_end=jnp.exp(c_end), v=v_ref[0, rows[s], :].astype(BF16)))

    def head(name, s, h):
        return prep[s][name][:, h * hd:(h + 1) * hd]

    sc = {}
    for s, h in chains:
        lhs = jnp.concatenate([head("at", s, h), head("rt", s, h)], 0).astype(BF16)
        rhs = jnp.concatenate([head("bt", s, h), head("kt", s, h)], 0)
        sc[s, h] = _dot_nt(lhs, rhs)
    nmat, x, mr = {}, {}, {}
    for s, h in chains:
        m = sc[s, h]
        nmat[s, h] = jnp.where(strict, m[:L, :L], 0.0)
        mak = jnp.where(strict, m[:L, L:], 0.0).astype(BF16)
        mr[s, h] = jnp.concatenate([jnp.where(incl, m[L:, :L], 0.0), jnp.where(incl, m[L:, L:], 0.0)], 1).astype(BF16)
        x[s, h] = jnp.concatenate([head("at", s, h), _dot(mak, head("v", s, h))], 1)
    steps = L.bit_length() - 1
    for i in range(steps):
        for s, h in chains:
            nb = nmat[s, h].astype(BF16)
            xb = x[s, h].astype(BF16)
            if i + 1 < steps:
                prod = _dot(nb, jnp.concatenate([xb, nb], 1))
                x[s, h] = x[s, h] + prod[:, :2 * hd]
                nmat[s, h] = prod[:, 2 * hd:]
            else:
                x[s, h] = x[s, h] + _dot(nb, xb)
    ftop, bhw_t, gt = {}, {}, {}
    for s, h in chains:
        xb = x[s, h].astype(BF16)
        vb = head("v", s, h)
        z = jnp.concatenate([xb, jnp.concatenate([zeros, vb], 1)], 0)
        ftop[s, h] = _dot(mr[s, h], z)
        t1 = _dot_tn(xb, head("bh", s, h))
        bhw_t[s, h] = t1[:hd].astype(BF16)
        gt[s, h] = t1[hd:] + _dot_tn(vb, head("kh", s, h))
    st = [st_ref[h] for h in range(RW_HEADS)]
    for s in range(nsub):
        ys = []
        for h in range(RW_HEADS):
            stb = st[h].astype(BF16)
            q = (head("rt", s, h) + ftop[s, h][:, :hd]).astype(BF16)
            ys.append(_dot_nt(q, stb) + ftop[s, h][:, hd:])
            st[h] = st[h] * head("p_end", s, h) + _dot(stb, bhw_t[s, h]) + gt[s, h]
        y_ref[0, rows[s], :] = jnp.concatenate(ys, 1)
    for h in range(RW_HEADS):
        st_ref[h] = st[h]


def rwkv_scan(r, k, v, kk, k_a, lw, a_sig, *, t_ctx, nsub=4):
    nseq, t, w = r.shape
    blk = RW_CHUNK * nsub
    assert t % blk == 0 and t_ctx % blk == 0
    shared, per_dir, _ = _scan_specs(nseq, blk, t_ctx // blk, t // blk, [w] * 4, [w] * 3)
    return pl.pallas_call(
        functools.partial(_rwkv_kernel, nsub=nsub, nseq=nseq),
        grid=(2 * nseq, t // blk),
        in_specs=shared + [pl.BlockSpec((1, w), lambda i, c: (0, 0))] + per_dir[:2],
        out_specs=per_dir[2],
        out_shape=jax.ShapeDtypeStruct((2 * nseq, t, w), F32),
        scratch_shapes=[pltpu.VMEM((RW_HEADS, RW_HD, RW_HD), F32)],
        compiler_params=pltpu.CompilerParams(dimension_semantics=("parallel", "arbitrary"),
                                             vmem_limit_bytes=VMEM_LIMIT),
        name="rwkv_scan",
    )(r, k, v, kk, k_a[None], lw, a_sig)


def _expand_cols(x, e):
    return sum(_dot(p, e) for p in _split3(x))


def _ssd_kernel(x_ref, b_ref, c_ref, dd_ref, y_ref, st_ref, *, nsub, nseq):
    L = SSD_CHUNK
    nh = SSM_HEADS
    pw = 2 * SSM_HD
    npairs = nh // 2
    pairs_per_group = npairs // SSM_GROUPS
    rev = pl.program_id(0) >= nseq

    @pl.when(pl.program_id(1) == 0)
    def _():
        st_ref[...] = jnp.zeros_like(st_ref)

    incl, _ = _scan_masks(L, rev)
    tri = jnp.where(incl, 1.0, 0.0).astype(BF16)
    h_i = lax.broadcasted_iota(jnp.int32, (nh, nh * SSM_HD), 0)
    c_i = lax.broadcasted_iota(jnp.int32, (nh, nh * SSM_HD), 1)
    e_head = jnp.where(c_i // SSM_HD == h_i, 1.0, 0.0).astype(BF16)
    h_j = lax.broadcasted_iota(jnp.int32, (nh, nh * L), 0)
    c_j = lax.broadcasted_iota(jnp.int32, (nh, nh * L), 1)
    e_wide = jnp.where(c_j // L == h_j, 1.0, 0.0).astype(BF16)
    lane_p = lax.broadcasted_iota(jnp.int32, (L, pw), 1)
    first_half = lane_p < SSM_HD
    r_bd = lax.broadcasted_iota(jnp.int32, (pw, pw), 0)
    c_bd = lax.broadcasted_iota(jnp.int32, (pw, pw), 1)
    bd_mask = (r_bd < SSM_HD) == (c_bd < SSM_HD)

    rows = [_chunk_rows(s, nsub, L, rev) for s in range(nsub)]
    work = []
    for s in range(nsub):
        dd = dd_ref[0, jnp.where(rev, nsub - 1 - s, s)]
        da_parts = _split3(dd[:nh])
        acs_row = sum(_dot_nt(p, tri) for p in da_parts)
        acs_col = sum(_dot_nt(tri, p) for p in da_parts)
        colx = _expand_cols(acs_col, e_wide)
        dtx = sum(_dot_tn(p, e_head) for p in _split3(dd[nh:]))
        xdt = (x_ref[0, rows[s], :] * dtx).astype(BF16)
        bm = b_ref[0, rows[s], :]
        cm = c_ref[0, rows[s], :]
        bm_sw = pltpu.roll(bm, SSM_STATE, 1)
        cm_sw = pltpu.roll(cm, SSM_STATE, 1)
        b2 = [jnp.where(first_half, bm, bm_sw), jnp.where(first_half, bm_sw, bm)]
        c2 = [jnp.where(first_half, cm, cm_sw), jnp.where(first_half, cm_sw, cm)]
        cb = [_dot_nt(cm[:, g * SSM_STATE:(g + 1) * SSM_STATE].astype(BF16),
                      bm[:, g * SSM_STATE:(g + 1) * SSM_STATE].astype(BF16)) for g in range(SSM_GROUPS)]
        work.append(dict(acs_row=acs_row, colx=colx, xdt=xdt, b2=b2, c2=c2, cb=cb))

    y_diag, ce, new, ea = {}, {}, {}, {}
    for s in range(nsub):
        w = work[s]
        for p in range(npairs):
            g = p // pairs_per_group
            h0, h1 = 2 * p, 2 * p + 1
            cx0 = w["colx"][:, h0 * L:(h0 + 1) * L]
            cx1 = w["colx"][:, h1 * L:(h1 + 1) * L]
            s0 = w["cb"][g] * jnp.exp(jnp.where(incl, cx0 - w["acs_row"][h0:h0 + 1, :], -jnp.inf))
            s1 = w["cb"][g] * jnp.exp(jnp.where(incl, cx1 - w["acs_row"][h1:h1 + 1, :], -jnp.inf))
            xp = w["xdt"][:, p * pw:(p + 1) * pw]
            zero = jnp.zeros_like(xp)
            x_bd = jnp.concatenate([jnp.where(first_half, xp, zero), jnp.where(first_half, zero, xp)], 0)
            y_diag[s, p] = _dot(jnp.concatenate([s0, s1], 1).astype(BF16), x_bd)
            col = jnp.where(first_half, cx0[:, :pw], cx1[:, :pw])
            a_end = _last_row(col, rev)
            ce[s, p] = (w["c2"][g] * jnp.exp(col)).astype(BF16)
            be = (w["b2"][g] * jnp.exp(a_end - col)).astype(BF16)
            new[s, p] = jnp.where(bd_mask, _dot_tn(be, xp), 0.0)
            ea[s, p] = jnp.exp(a_end)
    st = [st_ref[p] for p in range(npairs)]
    for s in range(nsub):
        ys = []
        for p in range(npairs):
            ys.append(y_diag[s, p] + _dot(ce[s, p], st[p].astype(BF16)))
            st[p] = st[p] * ea[s, p] + new[s, p]
        y_ref[0, rows[s], :] = jnp.concatenate(ys, 1)
    for p in range(npairs):
        st_ref[p] = st[p]


def ssd_scan(conv, dd, *, t_ctx, nsub=2):
    nseq, t, _ = conv.shape
    w = SSM_W
    blk = SSD_CHUNK * nsub
    assert t % blk == 0 and t_ctx % blk == 0
    assert SSM_STATE == SSM_HD and SSD_CHUNK == 2 * SSM_HD
    gw = SSM_GROUPS * SSM_STATE
    shared, per_dir, tblock = _scan_specs(nseq, blk, t_ctx // blk, t // blk,
                                          [(w, 0), (gw, w // gw), (gw, w // gw + 1)], [w])
    dd_spec = pl.BlockSpec((1, nsub, 2 * SSM_HEADS, SSD_CHUNK), lambda i, c: (i, tblock(i, c), 0, 0))
    return pl.pallas_call(
        functools.partial(_ssd_kernel, nsub=nsub, nseq=nseq),
        grid=(2 * nseq, t // blk),
        in_specs=shared + [dd_spec],
        out_specs=per_dir[0],
        out_shape=jax.ShapeDtypeStruct((2 * nseq, t, w), F32),
        scratch_shapes=[pltpu.VMEM((SSM_HEADS // 2, 2 * SSM_STATE, 2 * SSM_HD), F32)],
        compiler_params=pltpu.CompilerParams(dimension_semantics=("parallel", "arbitrary"),
                                             vmem_limit_bytes=VMEM_LIMIT),
        name="ssd_scan",
    )(conv, conv, conv, dd)


def _gla_kernel(q_ref, k_ref, v_ref, sm_ref, ga_ref, gb_ref, y_ref, st_ref, *, nsub, nseq):
    L = GLA_CHUNK
    rev = pl.program_id(0) >= nseq

    @pl.when(pl.program_id(1) == 0)
    def _():
        st_ref[...] = jnp.zeros_like(st_ref)

    nh = GLA_HEADS
    incl, _ = _scan_masks(L, rev)
    tri = jnp.where(incl, 1.0, 0.0).astype(BF16)
    incl_h = jnp.concatenate([incl] * nh, 0)
    r_q = lax.broadcasted_iota(jnp.int32, (nh * L, GLA_KW), 0)
    c_q = lax.broadcasted_iota(jnp.int32, (nh * L, GLA_KW), 1)
    q_mask = r_q // L == c_q // GLA_DK
    r_s = lax.broadcasted_iota(jnp.int32, (GLA_VW, GLA_KW), 0)
    c_s = lax.broadcasted_iota(jnp.int32, (GLA_VW, GLA_KW), 1)
    st_mask = r_s // GLA_DV == c_s // GLA_DK
    lane_v = lax.broadcasted_iota(jnp.int32, (L, GLA_VW), 1) // GLA_DV
    rows = [_chunk_rows(s, nsub, L, rev) for s in range(nsub)]

    o_intra, qd, kv_t, eb = [], [], [], []
    for s in range(nsub):
        gate_logit = _dot3(sm_ref[0, rows[s], :], ga_ref[0]) + gb_ref[0]
        log_gate = -_softplus(-gate_logit) * (1.0 / GLA_TAU)
        bcs = _cumsum_rows(tri, log_gate)
        mid = jnp.where(rev, bcs[L - 1 - L // 2:L - L // 2], bcs[L // 2:L // 2 + 1])
        bend = _last_row(bcs, rev)
        q = q_ref[0, rows[s], :] * GLA_DK ** -0.5
        k = k_ref[0, rows[s], :]
        vb = v_ref[0, rows[s], :].astype(BF16)
        qe = (q * jnp.exp(bcs - mid)).astype(BF16)
        ke = (k * jnp.exp(mid - bcs)).astype(BF16)
        q_st = jnp.where(q_mask, jnp.concatenate([qe] * nh, 0), jnp.zeros((), BF16))
        att = jnp.where(incl_h, _dot_nt(q_st, ke), 0.0).astype(BF16)
        full = _dot(att, vb)
        o_intra.append(sum(jnp.where(lane_v == h, full[h * L:(h + 1) * L], 0.0) for h in range(nh)))
        qd.append((q * jnp.exp(bcs)).astype(BF16))
        kd = (k * jnp.exp(bend - bcs)).astype(BF16)
        kv_t.append(jnp.where(st_mask, _dot_tn(vb, kd), 0.0))
        eb.append(jnp.exp(bend))
    st = st_ref[...]
    for s in range(nsub):
        y_ref[0, rows[s], :] = o_intra[s] + _dot_nt(qd[s], st.astype(BF16))
        st = st * eb[s] + kv_t[s]
    st_ref[...] = st


def gla_scan(slab, small, ga2, gb, *, t_ctx, nsub=4):
    nseq, t, _ = slab.shape
    kw, vw = GLA_KW, GLA_VW
    blk = GLA_CHUNK * nsub
    assert t % blk == 0 and t_ctx % blk == 0
    shared, per_dir, _ = _scan_specs(nseq, blk, t_ctx // blk, t // blk,
                                     [(kw, 0), (kw, 1), (vw, 2 * kw // vw), LANES], [vw])
    rank = ga2.shape[1]
    ga_pad = jnp.zeros((2, LANES, kw), F32).at[:, SSM_HEADS:SSM_HEADS + rank].set(ga2)
    return pl.pallas_call(
        functools.partial(_gla_kernel, nsub=nsub, nseq=nseq),
        grid=(2 * nseq, t // blk),
        in_specs=shared + [pl.BlockSpec((1, LANES, kw), lambda i, c: (i // nseq, 0, 0)),
                           pl.BlockSpec((1, 1, kw), lambda i, c: (i // nseq, 0, 0))],
        out_specs=per_dir[0],
        out_shape=jax.ShapeDtypeStruct((2 * nseq, t, vw), F32),
        scratch_shapes=[pltpu.VMEM((GLA_VW, GLA_KW), F32)],
        compiler_params=pltpu.CompilerParams(dimension_semantics=("parallel", "arbitrary"),
                                             vmem_limit_bytes=VMEM_LIMIT),
        name="gla_scan",
    )(slab, slab, slab, small, ga_pad, gb[:, None])


_HP = lax.Precision.HIGHEST


def rmsnorm_f32(h, g):
    return h * lax.rsqrt(jnp.mean(h * h, -1, keepdims=True) + NORM_EPS) * g


_IN_WIDTHS = (RW_COLS, SSM_W, SSM_CONV_CH, 2 * GLA_KW + 2 * GLA_VW, SMALL_W)


def _arrange_w_in(w_in):
    rw, ssm, gla = jnp.split(w_in, [RW_COLS, RW_COLS + SSM_COLS], -1)
    z, xbc, dt = jnp.split(ssm, [SSM_W, SSM_W + SSM_CONV_CH], -1)
    qkv, gl, r = jnp.split(gla, [2 * GLA_KW + GLA_VW, 2 * GLA_KW + GLA_VW + GLA_GATE_RANK], -1)
    pad = jnp.zeros((w_in.shape[0], SMALL_W - SSM_HEADS - GLA_GATE_RANK), w_in.dtype)
    return jnp.concatenate([rw, z, xbc, qkv, r, dt, gl, pad], -1).astype(BF16)


def _pad_lanes(w, reps=1):
    w = jnp.repeat(w, reps, axis=-1) if reps > 1 else w
    return jnp.pad(w, [(0, 0)] * (w.ndim - 1) + [(0, LANES - w.shape[-1])])


def kernel(x, c, ctx, c_ctx, ada_w, ada_b, norm1_g, norm2_g, w_in, w_out, rw_mu_prev, rw_mu_next, rw_w0, rw_w2, rw_a0, rw_a2, rw_g2, rw_k_k, rw_k_a, rw_r_k, rw_gn_g, rw_gn_b, ssm_conv_w, ssm_conv_b, ssm_dt_bias, ssm_a_log, ssm_d, ssm_norm_g, gla_ga2, gla_gb, gla_norm_g, moe_rg_w, moe_rg_b, moe_re_w, moe_re_b, moe_w1, moe_w3, moe_w2, final_g):
    depth = ada_w.shape[0]
    bsz, t_lat, _ = x.shape
    t_ctx = ctx.shape[1]
    t_all = t_ctx + t_lat
    assert t_lat // GRID_W * GRID_W == t_lat
    cond_l = jax.nn.silu(c)
    cond_c = jax.nn.silu(c_ctx)[None]
    tm = PROJ_TM
    for l in range(depth):
        ctx_out = l < depth - 1
        mod_l = jnp.split(jnp.dot(cond_l, ada_w[l], precision=_HP) + ada_b[l], 6, -1)
        mod_c = jnp.split(jnp.dot(cond_c, ada_w[l], precision=_HP) + ada_b[l], 6, -1)
        w_in_l = _arrange_w_in(w_in[l])
        slabs = in_proj(x, norm1_g[l], mod_l[0], mod_l[1], w_in_l, _IN_WIDTHS, tm=tm, t_all=t_all, t_off=t_ctx)
        u_rw, z, xbc, u_gla, small = in_proj(ctx, norm1_g[l], mod_c[0], mod_c[1], w_in_l, _IN_WIDTHS, tm=tm,
                                             t_all=t_all, t_off=0, bufs=slabs)
        r, k, v, kk, bonus, rgate, lw, a_sig = rwkv_prep(u_rw, rw_mu_prev[l], rw_mu_next[l], rw_w0[l], rw_w2[l],
                                                         rw_a0[l], rw_a2[l], rw_g2[l], rw_k_k[l], rw_r_k[l],
                                                         t_ctx=t_ctx, tm=tm)
        merge = lambda a: a.reshape((2 * bsz,) + a.shape[2:])
        split = lambda a: a.reshape((2, bsz) + a.shape[1:])
        y_rw = split(rwkv_scan(r, k, v, kk, rw_k_a[l], merge(lw), merge(a_sig), t_ctx=t_ctx))
        conv, dd = ssm_prep(xbc, small, ssm_conv_w[l], ssm_conv_b[l], ssm_dt_bias[l], ssm_a_log[l], t_ctx=t_ctx, tm=tm)
        y_ssd = split(ssd_scan(conv, merge(dd), t_ctx=t_ctx))
        y_gla = split(gla_scan(u_gla, small, gla_ga2[l], gla_gb[l], t_ctx=t_ctx))
        mixed = (y_rw, bonus, rgate, y_ssd, conv, z, y_gla, u_gla,
                 rw_gn_g[l], rw_gn_b[l], ssm_d[l], ssm_norm_g[l], gla_norm_g[l])
        w_out_l = w_out[l].astype(BF16)
        epg = EXPERTS_PER_GROUP
        rw = _pad_lanes(jnp.concatenate([jnp.repeat(moe_rg_w[l], epg, -1), moe_re_w[l]], -1))
        rb = _pad_lanes(jnp.concatenate([jnp.repeat(moe_rg_b[l], epg, -1), moe_re_b[l]], -1)[None])
        by_group = lambda w: w.astype(BF16).reshape((N_GROUPS, epg) + w.shape[1:])
        w1, w3, w2 = by_group(moe_w1[l]), by_group(moe_w3[l]), by_group(moe_w2[l])
        x = out_proj(*mixed, w_out_l, x, mod_l[2], tm=tm, t_off=t_ctx)
        x = moe_block(x, norm2_g[l], mod_l[3], mod_l[4], mod_l[5], rw, rb, w1, w3, w2, tm=MOE_TM)
        if ctx_out:
            ctx = out_proj(*mixed, w_out_l, ctx, mod_c[2], tm=tm, t_off=0)
            ctx = moe_block(ctx, norm2_g[l], mod_c[3], mod_c[4], mod_c[5], rw, rb, w1, w3, w2, tm=MOE_TM)
    return rmsnorm_f32(x, final_g)
```

```python
import functools

import jax
import jax.numpy as jnp
from jax import lax
from jax.experimental import pallas as pl
from jax.experimental.pallas import tpu as pltpu

F32 = jnp.float32
BF16 = jnp.bfloat16

D_MODEL = 1024
GRID_W = 64
NORM_EPS = 1e-6

RW_HEADS = 4
RW_HD = 64
RW_W = RW_HEADS * RW_HD
RW_DECAY_RANK = 64
RW_ICLR_RANK = 64
RW_GATE_RANK = 128
RW_GN_EPS = 64e-5
RW_COLS = 3 * RW_W + RW_DECAY_RANK + RW_ICLR_RANK + RW_GATE_RANK

SSM_HEADS = 8
SSM_HD = 64
SSM_W = SSM_HEADS * SSM_HD
SSM_GROUPS = 2
SSM_STATE = 64
SSM_CONV_CH = SSM_W + 2 * SSM_GROUPS * SSM_STATE
SSM_COLS = SSM_W + SSM_CONV_CH + SSM_HEADS

GLA_HEADS = 4
GLA_DK = 32
GLA_DV = 64
GLA_KW = GLA_HEADS * GLA_DK
GLA_VW = GLA_HEADS * GLA_DV
GLA_GATE_RANK = 16
GLA_TAU = 16.0
GLA_COLS = 2 * GLA_KW + GLA_VW + GLA_GATE_RANK + GLA_VW

N_GROUPS = 4
EXPERTS_PER_GROUP = 4
N_EXPERTS = N_GROUPS * EXPERTS_PER_GROUP
D_EXPERT = 512

LANES = 128
SMALL_W = LANES
VMEM_LIMIT = 56 * 1024 * 1024

PROJ_TM = 256
MOE_TM = 1024
MOE_SUB = 128
MOE_SCATTER_COLS = 256

RW_CHUNK = 64
SSD_CHUNK = 128
GLA_CHUNK = 64

_NT = (((1,), (1,)), ((), ()))
_TN = (((0,), (0,)), ((), ()))


def _dot(a, b):
    return jnp.dot(a, b, preferred_element_type=F32)


def _dot_nt(a, b):
    return lax.dot_general(a, b, _NT, preferred_element_type=F32)


def _dot_tn(a, b):
    return lax.dot_general(a, b, _TN, preferred_element_type=F32)


def _split3(x):
    hi = x.astype(BF16)
    r1 = x - hi.astype(F32)
    mid = r1.astype(BF16)
    lo = (r1 - mid.astype(F32)).astype(BF16)
    return hi, mid, lo


def _cumsum_rows(tri, x):
    hi, mid, lo = _split3(x)
    return _dot(tri, hi) + _dot(tri, mid) + _dot(tri, lo)


def _norm_mod(x, g, shift, scale):
    h = x * lax.rsqrt(jnp.mean(x * x, -1, keepdims=True) + NORM_EPS) * g
    return h * (1.0 + scale) + shift


def _in_proj_kernel(x_ref, g_ref, sh_ref, sc_ref, w_ref, *refs, widths):
    out_refs = refs[-len(widths):]
    h = _norm_mod(x_ref[...], g_ref[...], sh_ref[0], sc_ref[0]).astype(BF16)
    off = 0
    for o_ref, wd in zip(out_refs, widths):
        o_ref[0] = _dot(h, w_ref[:, off:off + wd])
        off += wd


def _mod_spec(mod, tiles_per_batch):
    d = mod.shape[-1]
    if mod.shape[0] == 1:
        return pl.BlockSpec((1, 1, d), lambda i: (0, 0, 0))
    return pl.BlockSpec((1, 1, d), lambda i: (i // tiles_per_batch, 0, 0))


def in_proj(x, g, shift, scale, w, widths, *, tm, t_all, t_off, bufs=None):
    bsz, t, d = x.shape
    n = bsz * t
    assert t % tm == 0 and t_off % tm == 0
    tpb = t // tm
    off_blk = t_off // tm
    x2 = x.reshape(n, d)
    ins = [x2, g[None], shift[:, None], scale[:, None], w]
    in_specs = [
        pl.BlockSpec((tm, d), lambda i: (i, 0)),
        pl.BlockSpec((1, d), lambda i: (0, 0)),
        _mod_spec(shift[:, None], tpb),
        _mod_spec(scale[:, None], tpb),
        pl.BlockSpec(w.shape, lambda i: (0, 0)),
    ]
    aliases = {}
    if bufs is not None:
        aliases = {len(ins) + j: j for j in range(len(bufs))}
        in_specs = in_specs + [pl.BlockSpec(memory_space=pl.ANY)] * len(bufs)
        ins = ins + list(bufs)
    return pl.pallas_call(
        functools.partial(_in_proj_kernel, widths=widths),
        grid=(n // tm,),
        in_specs=in_specs,
        out_specs=[pl.BlockSpec((1, tm, wd), lambda i: (i // tpb, off_blk + i % tpb, 0)) for wd in widths],
        out_shape=[jax.ShapeDtypeStruct((bsz, t_all, wd), F32) for wd in widths],
        input_output_aliases=aliases,
        compiler_params=pltpu.CompilerParams(dimension_semantics=("parallel",), vmem_limit_bytes=VMEM_LIMIT),
        name="in_proj",
    )(*ins)


def _dot3(a, b):
    a_hi, a_mid, _ = _split3(a)
    b_hi, b_mid, _ = _split3(b)
    return _dot(a_hi, b_hi) + _dot(a_hi, b_mid) + _dot(a_mid, b_hi)


def _head_ones(width, head):
    r = lax.broadcasted_iota(jnp.int32, (width, width), 0)
    c = lax.broadcasted_iota(jnp.int32, (width, width), 1)
    return jnp.where(r // head == c // head, 1.0, 0.0).astype(BF16)


def _head_sum(x, ones):
    return sum(_dot(p, ones) for p in _split3(x))


def _silu(x):
    return x * jax.nn.sigmoid(x)


def _out_proj_kernel(yr_ref, bonus_ref, rgate_ref, ys_ref, xs_ref, z_ref, yg_ref, gr_ref,
                     gng_ref, gnb_ref, dsk_ref, sng_ref, gng2_ref, w_ref, x_ref, gate_ref, o_ref):
    ones_rw = _head_ones(RW_W, RW_HD)
    yr = yr_ref[0, 0] + yr_ref[1, 0]
    mu = _head_sum(yr, ones_rw) * (1.0 / RW_HD)
    yc = yr - mu
    var = _head_sum(yc * yc, ones_rw) * (1.0 / RW_HD)
    a_out = (yc * lax.rsqrt(var + RW_GN_EPS) * gng_ref[...] + gnb_ref[...] + bonus_ref[0]) * rgate_ref[0]
    ysd = ys_ref[0, 0] + ys_ref[1, 0] + dsk_ref[...] * xs_ref[0]
    t = ysd * _silu(z_ref[0])
    b_out = t * lax.rsqrt(jnp.mean(t * t, -1, keepdims=True) + NORM_EPS) * sng_ref[...]
    yg = yg_ref[0, 0] + yg_ref[1, 0]
    ms = _head_sum(yg * yg, _head_ones(GLA_VW, GLA_DV)) * (1.0 / GLA_DV)
    g_out = yg * lax.rsqrt(ms + NORM_EPS) * gng2_ref[...] * _silu(gr_ref[0])
    m = jnp.concatenate([a_out, b_out, g_out], 1).astype(BF16)
    o_ref[...] = x_ref[...] + gate_ref[0] * _dot(m, w_ref[...])


def out_proj(y_rw, bonus, rgate, y_ssd, conv, z, y_gla, gla_slab, gn_g, gn_b, d_skip, ssm_norm_g, gla_norm_g,
             w, x, gate, *, tm, t_off):
    bsz, t, d = x.shape
    n = bsz * t
    assert t % tm == 0 and t_off % tm == 0
    tpb = t // tm
    ob = t_off // tm
    row = lambda i: (i // tpb, ob + i % tpb)
    both = lambda wd: pl.BlockSpec((2, 1, tm, wd), lambda i: (0,) + row(i) + (0,))
    one = lambda wd, cb=0: pl.BlockSpec((1, tm, wd), lambda i: row(i) + (cb,))
    par = lambda wd: pl.BlockSpec((1, wd), lambda i: (0, 0))
    r_blk = (2 * GLA_KW + GLA_VW) // GLA_VW
    out = pl.pallas_call(
        _out_proj_kernel,
        grid=(n // tm,),
        in_specs=[
            both(RW_W), one(RW_W), one(RW_W),
            both(SSM_W), one(SSM_W), one(SSM_W),
            both(GLA_VW), one(GLA_VW, r_blk),
            par(RW_W), par(RW_W), par(SSM_W), par(SSM_W), par(GLA_VW),
            pl.BlockSpec(w.shape, lambda i: (0, 0)),
            pl.BlockSpec((tm, d), lambda i: (i, 0)),
            _mod_spec(gate[:, None], tpb),
        ],
        out_specs=pl.BlockSpec((tm, d), lambda i: (i, 0)),
        out_shape=jax.ShapeDtypeStruct((n, d), F32),
        compiler_params=pltpu.CompilerParams(dimension_semantics=("parallel",), vmem_limit_bytes=VMEM_LIMIT),
        name="out_proj",
    )(y_rw, bonus, rgate, y_ssd, conv, z, y_gla, gla_slab,
      gn_g[None], gn_b[None], jnp.repeat(d_skip, SSM_HD)[None], ssm_norm_g[None], gla_norm_g[None],
      w, x.reshape(n, d), gate[:, None])
    return out.reshape(bsz, t, d)


def _softplus(x):
    return jnp.maximum(x, 0.0) + jnp.log1p(jnp.exp(-jnp.abs(x)))


def _rwkv_prep_kernel(u_ref, up_ref, un_ref, mup_ref, mun_ref, w2_ref, a2_ref, w0_ref, a0_ref, g2_ref, kk_ref, rk_ref,
                      r_o, k_o, v_o, kn_o, bonus_o, gate_o, lw_o, as_o, *, ctx_blk, n_blk):
    j = pl.program_id(1)
    u = u_ref[0]
    tm = u.shape[0]
    seg_start = jnp.logical_or(j == 0, j == ctx_blk)
    seg_end = jnp.logical_or(j == ctx_blk - 1, j == n_blk - 1)
    prev_row = jnp.where(seg_start, 0.0, up_ref[0, 7:8, :])
    next_row = jnp.where(seg_end, 0.0, un_ref[0, 0:1, :])
    row = lax.broadcasted_iota(jnp.int32, u.shape, 0)
    prev = jnp.where(row == 0, prev_row, pltpu.roll(u, 1, 0))
    nxt = jnp.where(row == tm - 1, next_row, pltpu.roll(u, tm - 1, 0))
    s = u + mup_ref[...] * (prev - u) + mun_ref[...] * (nxt - u)
    r, k, v = s[:, :RW_W], s[:, RW_W:2 * RW_W], s[:, 2 * RW_W:3 * RW_W]
    lora_in = s[:, 3 * RW_W:3 * RW_W + RW_DECAY_RANK + RW_ICLR_RANK]
    gl = s[:, 3 * RW_W + RW_DECAY_RANK + RW_ICLR_RANK:]
    ones = _head_ones(RW_W, RW_HD)
    kr = k * kk_ref[...]
    r_o[0] = r
    k_o[0] = k
    v_o[0] = v
    kn_o[0] = kr * lax.rsqrt(_head_sum(kr * kr, ones) + 1e-12)
    bonus_o[0] = _head_sum(r * k * rk_ref[...], ones) * v
    gate_o[0] = _dot3(jax.nn.sigmoid(gl), g2_ref[...])
    lora_t = jnp.tanh(lora_in)
    for d in range(2):
        w_log = -_softplus(-(w0_ref[d] + _dot3(lora_t, w2_ref[d]))) - 0.5
        lw_o[d, 0] = -jnp.exp(w_log)
        as_o[d, 0] = jax.nn.sigmoid(a0_ref[d] + _dot3(lora_in, a2_ref[d]))


def rwkv_prep(u, mu_prev, mu_next, w0, w2, a0, a2, g2, k_k, r_k, *, t_ctx, tm):
    bsz, t_all, wc = u.shape
    assert t_all % tm == 0 and t_ctx % tm == 0 and tm % 8 == 0
    n_blk = t_all // tm
    h8 = tm // 8
    last8 = t_all // 8 - 1
    zpad = jnp.zeros((2, RW_DECAY_RANK, RW_W), F32)
    w2p = jnp.concatenate([w2, zpad], 1)
    a2p = jnp.concatenate([zpad, a2], 1)
    par = lambda a: pl.BlockSpec(a.shape, lambda b, j: (0,) * a.ndim)
    params = [mu_prev[None], mu_next[None], w2p, a2p, w0[:, None], a0[:, None], g2, k_k[None], r_k[None]]
    o1 = pl.BlockSpec((1, tm, RW_W), lambda b, j: (b, j, 0))
    o2 = pl.BlockSpec((2, 1, tm, RW_W), lambda b, j: (0, b, j, 0))
    s1 = jax.ShapeDtypeStruct((bsz, t_all, RW_W), F32)
    s2 = jax.ShapeDtypeStruct((2, bsz, t_all, RW_W), F32)
    return pl.pallas_call(
        functools.partial(_rwkv_prep_kernel, ctx_blk=t_ctx // tm, n_blk=n_blk),
        grid=(bsz, n_blk),
        in_specs=[
            pl.BlockSpec((1, tm, wc), lambda b, j: (b, j, 0)),
            pl.BlockSpec((1, 8, wc), lambda b, j: (b, jnp.maximum(j * h8 - 1, 0), 0)),
            pl.BlockSpec((1, 8, wc), lambda b, j: (b, jnp.minimum((j + 1) * h8, last8), 0)),
        ] + [par(a) for a in params],
        out_specs=[o1] * 6 + [o2] * 2,
        out_shape=[s1] * 6 + [s2] * 2,
        compiler_params=pltpu.CompilerParams(dimension_semantics=("parallel", "parallel"),
                                             vmem_limit_bytes=VMEM_LIMIT),
        name="rwkv_prep",
    )(u, u, u, *params)


def _ssm_prep_kernel(x_ref, xp_ref, xn_ref, sm_ref, cw_ref, cb_ref, dtb_ref, alog_ref, o_ref, dd_ref,
                     *, t_ctx, t_all, halo):
    j = pl.program_id(1)
    tm = x_ref.shape[1]
    t0 = j * tm
    is_ctx = t0 < t_ctx
    seg_lo = jnp.where(is_ctx, 0, t_ctx)
    seg_hi = jnp.where(is_ctx, t_ctx, t_all)
    ctx_i = jnp.where(is_ctx, 1, 0)
    lat_f = jnp.where(is_ctx, 0.0, 1.0)
    ext = jnp.concatenate([xp_ref[0], x_ref[0], xn_ref[0]], 0)
    n_ext = ext.shape[0]
    reps = ext.shape[1] // LANES
    e_row = lax.broadcasted_iota(jnp.int32, (n_ext, LANES), 0)
    t_src = e_row + (t0 - halo)
    in_seg = jnp.logical_and(t_src >= seg_lo, t_src < seg_hi)
    col = e_row % GRID_W
    keep_left = jnp.logical_and(in_seg, (jnp.where(col != GRID_W - 1, 1, 0) | ctx_i) > 0)
    keep_right = jnp.logical_and(in_seg, (jnp.where(col != 0, 1, 0) | ctx_i) > 0)
    wide = lambda m: jnp.concatenate([m] * reps, 1)
    src = {0: jnp.where(wide(in_seg), ext, 0.0), -1: jnp.where(wide(keep_left), ext, 0.0),
           1: jnp.where(wide(keep_right), ext, 0.0)}
    acc = jnp.zeros((tm, ext.shape[1]), F32) + cb_ref[...]
    for dr in (-1, 0, 1):
        for dc in (-1, 0, 1):
            off = halo + GRID_W * dr + dc
            wt = cw_ref[(dr + 1) * 3 + dc + 1:(dr + 1) * 3 + dc + 2, :]
            if dr != 0:
                wt = wt * lat_f
            acc = acc + src[dc][off:off + tm] * wt
    o_ref[0] = _silu(acc)
    dt = sm_ref[0].T[:SSM_HEADS]
    for d in range(2):
        dtp = _softplus(dt + dtb_ref[d])
        da = -jnp.exp(alog_ref[d]) * dtp
        for c in range(tm // SSD_CHUNK):
            cs = slice(c * SSD_CHUNK, (c + 1) * SSD_CHUNK)
            dd_ref[d, 0, c, :SSM_HEADS, :] = da[:, cs]
            dd_ref[d, 0, c, SSM_HEADS:, :] = dtp[:, cs]


def ssm_prep(xbc, small, conv_w, conv_b, dt_bias, a_log, *, t_ctx, tm):
    bsz, t_all, ch = xbc.shape
    halo = 128
    assert t_all % tm == 0 and t_ctx % tm == 0 and tm % halo == 0 and halo > GRID_W and tm % SSD_CHUNK == 0
    n_blk = t_all // tm
    hb = tm // halo
    last = t_all // halo - 1
    par = lambda a: pl.BlockSpec(a.shape, lambda b, j: (0,) * a.ndim)
    params = [conv_w.reshape(9, ch), conv_b[None], dt_bias[:, :, None], a_log[:, :, None]]
    nc = tm // SSD_CHUNK
    return pl.pallas_call(
        functools.partial(_ssm_prep_kernel, t_ctx=t_ctx, t_all=t_all, halo=halo),
        grid=(bsz, n_blk),
        in_specs=[
            pl.BlockSpec((1, tm, ch), lambda b, j: (b, j, 0)),
            pl.BlockSpec((1, halo, ch), lambda b, j: (b, jnp.maximum(j * hb - 1, 0), 0)),
            pl.BlockSpec((1, halo, ch), lambda b, j: (b, jnp.minimum((j + 1) * hb, last), 0)),
            pl.BlockSpec((1, tm, LANES), lambda b, j: (b, j, 0)),
        ] + [par(a) for a in params],
        out_specs=[pl.BlockSpec((1, tm, ch), lambda b, j: (b, j, 0)),
                   pl.BlockSpec((2, 1, nc, 2 * SSM_HEADS, SSD_CHUNK), lambda b, j: (0, b, j, 0, 0))],
        out_shape=[jax.ShapeDtypeStruct((bsz, t_all, ch), F32),
                   jax.ShapeDtypeStruct((2, bsz, t_all // SSD_CHUNK, 2 * SSM_HEADS, SSD_CHUNK), F32)],
        compiler_params=pltpu.CompilerParams(dimension_semantics=("parallel", "parallel"),
                                             vmem_limit_bytes=VMEM_LIMIT),
        name="ssm_prep",
    )(xbc, xbc, xbc, small, *params)


def _route(logits):
    lane = lax.broadcasted_iota(jnp.int32, logits.shape, 1)
    valid = lane < N_EXPERTS
    neg = -jnp.inf
    big = jnp.int32(1 << 20)
    el = pltpu.roll(logits, LANES - N_EXPERTS, 1)
    glm = jnp.where(valid, logits, neg)
    gmax = jnp.max(glm, -1, keepdims=True)
    g_sel_lane = jnp.min(jnp.where(glm == gmax, lane, big), -1, keepdims=True)
    g_sel = g_sel_lane // EXPERTS_PER_GROUP
    in_group = (lane // EXPERTS_PER_GROUP) == g_sel
    gsum = jnp.sum(jnp.where(valid, jnp.exp(glm - gmax), 0.0), -1, keepdims=True) / EXPERTS_PER_GROUP
    p_group = 1.0 / gsum
    elm = jnp.where(in_group & valid, el, neg)
    m1 = jnp.max(elm, -1, keepdims=True)
    i1 = jnp.min(jnp.where(elm == m1, lane, big), -1, keepdims=True)
    elm2 = jnp.where(lane == i1, neg, elm)
    m2 = jnp.max(elm2, -1, keepdims=True)
    i2 = jnp.min(jnp.where(elm2 == m2, lane, big), -1, keepdims=True)
    p2 = jnp.exp(m2 - m1)
    wa = p_group / (1.0 + p2)
    wb = p_group * p2 / (1.0 + p2)
    return jnp.where(lane == i1, wa, 0.0) + jnp.where(lane == i2, wb, 0.0), g_sel


def _moe_kernel(x_ref, g_ref, sh_ref, sc_ref, gate_ref, rw_ref, rb_ref, w1_ref, w3_ref, w2_ref,
                o_ref, h_ref, comb_ref, code_ref):
    grp = pl.program_id(1)
    tm, d = x_ref.shape

    @pl.when(grp == 0)
    def _():
        h = _norm_mod(x_ref[...], g_ref[...], sh_ref[0], sc_ref[0])
        h_ref[...] = h.astype(BF16)
        h_hi, h_mid, _ = _split3(h)
        w_hi, w_mid, _ = _split3(rw_ref[...])
        logits = _dot(h_hi, w_hi) + _dot(h_hi, w_mid) + _dot(h_mid, w_hi) + rb_ref[...]
        comb, g_sel = _route(logits)
        comb_ref[...] = comb
        lane = lax.broadcasted_iota(jnp.int32, (tm, LANES), 1)
        member = jnp.where(lane == g_sel, 1.0, 0.0)
        r_t = lax.broadcasted_iota(jnp.int32, (tm, tm), 0)
        c_t = lax.broadcasted_iota(jnp.int32, (tm, tm), 1)
        before = jnp.where(c_t < r_t, 1.0, 0.0).astype(BF16)
        rank = _dot(before, member.astype(BF16))
        code = jnp.where(member > 0.0, rank, -1.0)
        code_ref[...] = code.T[:code_ref.shape[0]]
        o_ref[...] = jnp.zeros_like(o_ref)

    crow = code_ref[pl.ds(grp, 1), :]
    count = jnp.max(crow).astype(jnp.int32) + 1
    sub_i = lax.broadcasted_iota(jnp.int32, (MOE_SUB, tm), 0).astype(F32)
    comb_parts = _split3(comb_ref[...])
    lane_c = lax.broadcasted_iota(jnp.int32, (MOE_SUB, LANES), 1)

    def body(j, carry):
        base = (j * MOE_SUB).astype(F32)
        sel = jnp.where(crow - base == sub_i, 1.0, 0.0).astype(BF16)
        hg = _dot(sel, h_ref[...]).astype(BF16)
        cg = sum(_dot(sel, p) for p in comb_parts)
        ysum = jnp.zeros((MOE_SUB, d), F32)
        for e in range(EXPERTS_PER_GROUP):
            a = _dot(hg, w1_ref[0, e])
            b = _dot(hg, w3_ref[0, e])
            hid = (a * jax.nn.sigmoid(a) * b).astype(BF16)
            col = jnp.sum(jnp.where(lane_c == grp * EXPERTS_PER_GROUP + e, cg, 0.0), -1, keepdims=True)
            ysum = ysum + col * _dot(hid, w2_ref[0, e])
        yb = ysum.astype(BF16)
        for c0 in range(0, d, MOE_SCATTER_COLS):
            cs = slice(c0, c0 + MOE_SCATTER_COLS)
            o_ref[:, cs] += _dot_tn(sel, yb[:, cs])
        return carry

    lax.fori_loop(0, (count + MOE_SUB - 1) // MOE_SUB, body, 0)

    @pl.when(grp == N_GROUPS - 1)
    def _():
        o_ref[...] = x_ref[...] + gate_ref[0] * o_ref[...]


def moe_block(x, g, shift, scale, gate, rw, rb, w1, w3, w2, *, tm):
    bsz, t, d = x.shape
    n = bsz * t
    tm = min(tm, t)
    assert t % tm == 0 and tm % MOE_SUB == 0 and d % MOE_SCATTER_COLS == 0
    tpb = t // tm
    epg = EXPERTS_PER_GROUP

    def mod_spec(mod):
        if mod.shape[0] == 1:
            return pl.BlockSpec((1, 1, d), lambda i, e: (0, 0, 0))
        return pl.BlockSpec((1, 1, d), lambda i, e: (i // tpb, 0, 0))

    const = lambda shape: pl.BlockSpec(shape, lambda i, e: (0,) * len(shape))
    out = pl.pallas_call(
        _moe_kernel,
        grid=(n // tm, N_GROUPS),
        in_specs=[
            pl.BlockSpec((tm, d), lambda i, e: (i, 0)),
            const((1, d)),
            mod_spec(shift[:, None]), mod_spec(scale[:, None]), mod_spec(gate[:, None]),
            const(rw.shape), const(rb.shape),
            pl.BlockSpec((1, epg, d, D_EXPERT), lambda i, e: (e, 0, 0, 0)),
            pl.BlockSpec((1, epg, d, D_EXPERT), lambda i, e: (e, 0, 0, 0)),
            pl.BlockSpec((1, epg, D_EXPERT, d), lambda i, e: (e, 0, 0, 0)),
        ],
        out_specs=pl.BlockSpec((tm, d), lambda i, e: (i, 0)),
        out_shape=jax.ShapeDtypeStruct((n, d), F32),
        scratch_shapes=[pltpu.VMEM((tm, d), BF16), pltpu.VMEM((tm, LANES), F32), pltpu.VMEM((8, tm), F32)],
        compiler_params=pltpu.CompilerParams(dimension_semantics=("parallel", "arbitrary"),
                                             vmem_limit_bytes=VMEM_LIMIT),
        name="moe",
    )(x.reshape(n, d), g[None], shift[:, None], scale[:, None], gate[:, None], rw, rb, w1, w3, w2)
    return out.reshape(bsz, t, d)


def _scan_masks(n, rev):
    row = lax.broadcasted_iota(jnp.int32, (n, n), 0)
    col = lax.broadcasted_iota(jnp.int32, (n, n), 1)
    d = (row - col) * jnp.where(rev, -1, 1)
    return d >= 0, d > 0


def _chunk_rows(s, nsub, size, rev):
    return pl.ds(pl.multiple_of(jnp.where(rev, (nsub - 1 - s) * size, s * size), size), size)


def _last_row(x, rev):
    n = x.shape[0]
    return jnp.where(rev, x[0:1], x[n - 1:n])


def _scan_specs(nseq, blk, n_ctx_blk, n_blk, widths_shared, widths_dir):
    def tblock(i, c):
        back = jnp.where(c < n_ctx_blk, n_ctx_blk - 1 - c, n_ctx_blk + n_blk - 1 - c)
        return jnp.where(i >= nseq, back, c)
    def shared_spec(w, col_blk=0):
        return pl.BlockSpec((1, blk, w), lambda i, c: (i % nseq, tblock(i, c), col_blk))
    shared = [shared_spec(*w) if isinstance(w, tuple) else shared_spec(w) for w in widths_shared]
    per_dir = [pl.BlockSpec((1, blk, w), lambda i, c: (i, tblock(i, c), 0)) for w in widths_dir]
    return shared, per_dir, tblock


def _rwkv_kernel(r_ref, k_ref, v_ref, kk_ref, ka_ref, lw_ref, as_ref, y_ref, st_ref, *, nsub, nseq):
    L = RW_CHUNK
    hd = RW_HD
    rev = pl.program_id(0) >= nseq

    @pl.when(pl.program_id(1) == 0)
    def _():
        st_ref[...] = jnp.zeros_like(st_ref)

    incl, strict = _scan_masks(L, rev)
    tri = jnp.where(incl, 1.0, 0.0).astype(BF16)
    zeros = jnp.zeros((L, hd), BF16)
    chains = [(s, h) for s in range(nsub) for h in range(RW_HEADS)]
    rows = [_chunk_rows(s, nsub, L, rev) for s in range(nsub)]

    prep = []
    for s in range(nsub):
        lw = lw_ref[0, rows[s], :]
        c = _cumsum_rows(tri, lw)
        c_end = _last_row(c, rev)
        e_inv = jnp.exp(-c)
        e_end = jnp.exp(c_end - c)
        a_sig = as_ref[0, rows[s], :]
        kk = kk_ref[0, rows[s], :]
        k = k_ref[0, rows[s], :]
        kmod = k * (1.0 + (a_sig - 1.0) * ka_ref[...])
        bv = kk * a_sig
        prep.append(dict(
            at=-kk * jnp.exp(c - lw), rt=r_ref[0, rows[s], :] * jnp.exp(c),
            bt=(bv * e_inv).astype(BF16), kt=(kmod * e_inv).astype(BF16),
            bh=(bv * e_end).astype(BF16), kh=(kmod * e_end).astype(BF16),
            p_end=jnp.exp(c_end), v=v_ref[0, rows[s], :].astype(BF16)))

    def head(name, s, h):
        return prep[s][name][:, h * hd:(h + 1) * hd]

    sc = {}
    for s, h in chains:
        lhs = jnp.concatenate([head("at", s, h), head("rt", s, h)], 0).astype(BF16)
        rhs = jnp.concatenate([head("bt", s, h), head("kt", s, h)], 0)
        sc[s, h] = _dot_nt(lhs, rhs)
    nmat, x, mr = {}, {}, {}
    for s, h in chains:
        m = sc[s, h]
        nmat[s, h] = jnp.where(strict, m[:L, :L], 0.0)
        mak = jnp.where(strict, m[:L, L:], 0.0).astype(BF16)
        mr[s, h] = jnp.concatenate([jnp.where(incl, m[L:, :L], 0.0), jnp.where(incl, m[L:, L:], 0.0)], 1).astype(BF16)
        x[s, h] = jnp.concatenate([head("at", s, h), _dot(mak, head("v", s, h))], 1)
    steps = L.bit_length() - 1
    for i in range(steps):
        for s, h in chains:
            nb = nmat[s, h].astype(BF16)
            xb = x[s, h].astype(BF16)
            if i + 1 < steps:
                prod = _dot(nb, jnp.concatenate([xb, nb], 1))
                x[s, h] = x[s, h] + prod[:, :2 * hd]
                nmat[s, h] = prod[:, 2 * hd:]
            else:
                x[s, h] = x[s, h] + _dot(nb, xb)
    ftop, bhw_t, gt = {}, {}, {}
    for s, h in chains:
        xb = x[s, h].astype(BF16)
        vb = head("v", s, h)
        z = jnp.concatenate([xb, jnp.concatenate([zeros, vb], 1)], 0)
        ftop[s, h] = _dot(mr[s, h], z)
        t1 = _dot_tn(xb, head("bh", s, h))
        bhw_t[s, h] = t1[:hd].astype(BF16)
        gt[s, h] = t1[hd:] + _dot_tn(vb, head("kh", s, h))
    st = [st_ref[h] for h in range(RW_HEADS)]
    for s in range(nsub):
        ys = []
        for h in range(RW_HEADS):
            stb = st[h].astype(BF16)
            q = (head("rt", s, h) + ftop[s, h][:, :hd]).astype(BF16)
            ys.append(_dot_nt(q, stb) + ftop[s, h][:, hd:])
            st[h] = st[h] * head("p_end", s, h) + _dot(stb, bhw_t[s, h]) + gt[s, h]
        y_ref[0, rows[s], :] = jnp.concatenate(ys, 1)
    for h in range(RW_HEADS):
        st_ref[h] = st[h]


def rwkv_scan(r, k, v, kk, k_a, lw, a_sig, *, t_ctx, nsub=4):
    nseq, t, w = r.shape
    blk = RW_CHUNK * nsub
    assert t % blk == 0 and t_ctx % blk == 0
    shared, per_dir, _ = _scan_specs(nseq, blk, t_ctx // blk, t // blk, [w] * 4, [w] * 3)
    return pl.pallas_call(
        functools.partial(_rwkv_kernel, nsub=nsub, nseq=nseq),
        grid=(2 * nseq, t // blk),
        in_specs=shared + [pl.BlockSpec((1, w), lambda i, c: (0, 0))] + per_dir[:2],
        out_specs=per_dir[2],
        out_shape=jax.ShapeDtypeStruct((2 * nseq, t, w), F32),
        scratch_shapes=[pltpu.VMEM((RW_HEADS, RW_HD, RW_HD), F32)],
        compiler_params=pltpu.CompilerParams(dimension_semantics=("parallel", "arbitrary"),
                                             vmem_limit_bytes=VMEM_LIMIT),
        name="rwkv_scan",
    )(r, k, v, kk, k_a[None], lw, a_sig)


def _expand_cols(x, e):
    return sum(_dot(p, e) for p in _split3(x))


def _ssd_kernel(x_ref, b_ref, c_ref, dd_ref, y_ref, st_ref, *, nsub, nseq):
    L = SSD_CHUNK
    nh = SSM_HEADS
    pw = 2 * SSM_HD
    npairs = nh // 2
    pairs_per_group = npairs // SSM_GROUPS
    rev = pl.program_id(0) >= nseq

    @pl.when(pl.program_id(1) == 0)
    def _():
        st_ref[...] = jnp.zeros_like(st_ref)

    incl, _ = _scan_masks(L, rev)
    tri = jnp.where(incl, 1.0, 0.0).astype(BF16)
    h_i = lax.broadcasted_iota(jnp.int32, (nh, nh * SSM_HD), 0)
    c_i = lax.broadcasted_iota(jnp.int32, (nh, nh * SSM_HD), 1)
    e_head = jnp.where(c_i // SSM_HD == h_i, 1.0, 0.0).astype(BF16)
    h_j = lax.broadcasted_iota(jnp.int32, (nh, nh * L), 0)
    c_j = lax.broadcasted_iota(jnp.int32, (nh, nh * L), 1)
    e_wide = jnp.where(c_j // L == h_j, 1.0, 0.0).astype(BF16)
    lane_p = lax.broadcasted_iota(jnp.int32, (L, pw), 1)
    first_half = lane_p < SSM_HD
    r_bd = lax.broadcasted_iota(jnp.int32, (pw, pw), 0)
    c_bd = lax.broadcasted_iota(jnp.int32, (pw, pw), 1)
    bd_mask = (r_bd < SSM_HD) == (c_bd < SSM_HD)

    rows = [_chunk_rows(s, nsub, L, rev) for s in range(nsub)]
    work = []
    for s in range(nsub):
        dd = dd_ref[0, jnp.where(rev, nsub - 1 - s, s)]
        da_parts = _split3(dd[:nh])
        acs_row = sum(_dot_nt(p, tri) for p in da_parts)
        acs_col = sum(_dot_nt(tri, p) for p in da_parts)
        colx = _expand_cols(acs_col, e_wide)
        dtx = sum(_dot_tn(p, e_head) for p in _split3(dd[nh:]))
        xdt = (x_ref[0, rows[s], :] * dtx).astype(BF16)
        bm = b_ref[0, rows[s], :]
        cm = c_ref[0, rows[s], :]
        bm_sw = pltpu.roll(bm, SSM_STATE, 1)
        cm_sw = pltpu.roll(cm, SSM_STATE, 1)
        b2 = [jnp.where(first_half, bm, bm_sw), jnp.where(first_half, bm_sw, bm)]
        c2 = [jnp.where(first_half, cm, cm_sw), jnp.where(first_half, cm_sw, cm)]
        cb = [_dot_nt(cm[:, g * SSM_STATE:(g + 1) * SSM_STATE].astype(BF16),
                      bm[:, g * SSM_STATE:(g + 1) * SSM_STATE].astype(BF16)) for g in range(SSM_GROUPS)]
        work.append(dict(acs_row=acs_row, colx=colx, xdt=xdt, b2=b2, c2=c2, cb=cb))

    y_diag, ce, new, ea = {}, {}, {}, {}
    for s in range(nsub):
        w = work[s]
        for p in range(npairs):
            g = p // pairs_per_group
            h0, h1 = 2 * p, 2 * p + 1
            cx0 = w["colx"][:, h0 * L:(h0 + 1) * L]
            cx1 = w["colx"][:, h1 * L:(h1 + 1) * L]
            s0 = w["cb"][g] * jnp.exp(jnp.where(incl, cx0 - w["acs_row"][h0:h0 + 1, :], -jnp.inf))
            s1 = w["cb"][g] * jnp.exp(jnp.where(incl, cx1 - w["acs_row"][h1:h1 + 1, :], -jnp.inf))
            xp = w["xdt"][:, p * pw:(p + 1) * pw]
            zero = jnp.zeros_like(xp)
            x_bd = jnp.concatenate([jnp.where(first_half, xp, zero), jnp.where(first_half, zero, xp)], 0)
            y_diag[s, p] = _dot(jnp.concatenate([s0, s1], 1).astype(BF16), x_bd)
            col = jnp.where(first_half, cx0[:, :pw], cx1[:, :pw])
            a_end = _last_row(col, rev)
            ce[s, p] = (w["c2"][g] * jnp.exp(col)).astype(BF16)
            be = (w["b2"][g] * jnp.exp(a_end - col)).astype(BF16)
            new[s, p] = jnp.where(bd_mask, _dot_tn(be, xp), 0.0)
            ea[s, p] = jnp.exp(a_end)
    st = [st_ref[p] for p in range(npairs)]
    for s in range(nsub):
        ys = []
        for p in range(npairs):
            ys.append(y_diag[s, p] + _dot(ce[s, p], st[p].astype(BF16)))
            st[p] = st[p] * ea[s, p] + new[s, p]
        y_ref[0, rows[s], :] = jnp.concatenate(ys, 1)
    for p in range(npairs):
        st_ref[p] = st[p]


def ssd_scan(conv, dd, *, t_ctx, nsub=2):
    nseq, t, _ = conv.shape
    w = SSM_W
    blk = SSD_CHUNK * nsub
    assert t % blk == 0 and t_ctx % blk == 0
    assert SSM_STATE == SSM_HD and SSD_CHUNK == 2 * SSM_HD
    gw = SSM_GROUPS * SSM_STATE
    shared, per_dir, tblock = _scan_specs(nseq, blk, t_ctx // blk, t // blk,
                                          [(w, 0), (gw, w // gw), (gw, w // gw + 1)], [w])
    dd_spec = pl.BlockSpec((1, nsub, 2 * SSM_HEADS, SSD_CHUNK), lambda i, c: (i, tblock(i, c), 0, 0))
    return pl.pallas_call(
        functools.partial(_ssd_kernel, nsub=nsub, nseq=nseq),
        grid=(2 * nseq, t // blk),
        in_specs=shared + [dd_spec],
        out_specs=per_dir[0],
        out_shape=jax.ShapeDtypeStruct((2 * nseq, t, w), F32),
        scratch_shapes=[pltpu.VMEM((SSM_HEADS // 2, 2 * SSM_STATE, 2 * SSM_HD), F32)],
        compiler_params=pltpu.CompilerParams(dimension_semantics=("parallel", "arbitrary"),
                                             vmem_limit_bytes=VMEM_LIMIT),
        name="ssd_scan",
    )(conv, conv, conv, dd)


def _gla_kernel(q_ref, k_ref, v_ref, sm_ref, ga_ref, gb_ref, y_ref, st_ref, lg_ref, *, nsub, nseq):
    L = GLA_CHUNK
    rev = pl.program_id(0) >= nseq

    @pl.when(pl.program_id(1) == 0)
    def _():
        st_ref[...] = jnp.zeros_like(st_ref)

    nh = GLA_HEADS
    incl, _ = _scan_masks(L, rev)
    tri = jnp.where(incl, 1.0, 0.0).astype(BF16)
    incl_h = jnp.concatenate([incl] * nh, 0)
    r_q = lax.broadcasted_iota(jnp.int32, (nh * L, GLA_KW), 0)
    c_q = lax.broadcasted_iota(jnp.int32, (nh * L, GLA_KW), 1)
    q_mask = r_q // L == c_q // GLA_DK
    r_s = lax.broadcasted_iota(jnp.int32, (GLA_VW, GLA_KW), 0)
    c_s = lax.broadcasted_iota(jnp.int32, (GLA_VW, GLA_KW), 1)
    st_mask = r_s // GLA_DV == c_s // GLA_DK
    lane_v = lax.broadcasted_iota(jnp.int32, (L, GLA_VW), 1) // GLA_DV
    rows = [_chunk_rows(s, nsub, L, rev) for s in range(nsub)]

    gate_logit = _dot3(sm_ref[0], ga_ref[0]) + gb_ref[0]
    lg_ref[...] = -_softplus(-gate_logit) * (1.0 / GLA_TAU)
    bcs = [_cumsum_rows(tri, lg_ref[rows[s], :]) for s in range(nsub)]
    vb, ke, q_st, qd, kd, eb = [], [], [], [], [], []
    for s in range(nsub):
        mid = jnp.where(rev, bcs[s][L - 1 - L // 2:L - L // 2], bcs[s][L // 2:L // 2 + 1])
        bend = _last_row(bcs[s], rev)
        q = q_ref[0, rows[s], :] * GLA_DK ** -0.5
        k = k_ref[0, rows[s], :]
        vb.append(v_ref[0, rows[s], :].astype(BF16))
        qe = (q * jnp.exp(bcs[s] - mid)).astype(BF16)
        ke.append((k * jnp.exp(mid - bcs[s])).astype(BF16))
        q_st.append(jnp.where(q_mask, jnp.concatenate([qe] * nh, 0), jnp.zeros((), BF16)))
        qd.append((q * jnp.exp(bcs[s])).astype(BF16))
        kd.append((k * jnp.exp(bend - bcs[s])).astype(BF16))
        eb.append(jnp.exp(bend))
    att = [jnp.where(incl_h, _dot_nt(q_st[s], ke[s]), 0.0).astype(BF16) for s in range(nsub)]
    full = [_dot(att[s], vb[s]) for s in range(nsub)]
    o_intra = [sum(jnp.where(lane_v == h, full[s][h * L:(h + 1) * L], 0.0) for h in range(nh)) for s in range(nsub)]
    kv_t = [jnp.where(st_mask, _dot_tn(vb[s], kd[s]), 0.0) for s in range(nsub)]
    st = st_ref[...]
    for s in range(nsub):
        y_ref[0, rows[s], :] = o_intra[s] + _dot_nt(qd[s], st.astype(BF16))
        st = st * eb[s] + kv_t[s]
    st_ref[...] = st


def gla_scan(slab, small, ga2, gb, *, t_ctx, nsub=4):
    nseq, t, _ = slab.shape
    kw, vw = GLA_KW, GLA_VW
    blk = GLA_CHUNK * nsub
    assert t % blk == 0 and t_ctx % blk == 0
    shared, per_dir, _ = _scan_specs(nseq, blk, t_ctx // blk, t // blk,
                                     [(kw, 0), (kw, 1), (vw, 2 * kw // vw), LANES], [vw])
    rank = ga2.shape[1]
    ga_pad = jnp.zeros((2, LANES, kw), F32).at[:, SSM_HEADS:SSM_HEADS + rank].set(ga2)
    return pl.pallas_call(
        functools.partial(_gla_kernel, nsub=nsub, nseq=nseq),
        grid=(2 * nseq, t // blk),
        in_specs=shared + [pl.BlockSpec((1, LANES, kw), lambda i, c: (i // nseq, 0, 0)),
                           pl.BlockSpec((1, 1, kw), lambda i, c: (i // nseq, 0, 0))],
        out_specs=per_dir[0],
        out_shape=jax.ShapeDtypeStruct((2 * nseq, t, vw), F32),
        scratch_shapes=[pltpu.VMEM((GLA_VW, GLA_KW), F32), pltpu.VMEM((blk, kw), F32)],
        compiler_params=pltpu.CompilerParams(dimension_semantics=("parallel", "arbitrary"),
                                             vmem_limit_bytes=VMEM_LIMIT),
        name="gla_scan",
    )(slab, slab, slab, small, ga_pad, gb[:, None])


_HP = lax.Precision.HIGHEST


def rmsnorm_f32(h, g):
    return h * lax.rsqrt(jnp.mean(h * h, -1, keepdims=True) + NORM_EPS) * g


_IN_WIDTHS = (RW_COLS, SSM_W, SSM_CONV_CH, 2 * GLA_KW + 2 * GLA_VW, SMALL_W)


def _arrange_w_in(w_in):
    rw, ssm, gla = jnp.split(w_in, [RW_COLS, RW_COLS + SSM_COLS], -1)
    z, xbc, dt = jnp.split(ssm, [SSM_W, SSM_W + SSM_CONV_CH], -1)
    qkv, gl, r = jnp.split(gla, [2 * GLA_KW + GLA_VW, 2 * GLA_KW + GLA_VW + GLA_GATE_RANK], -1)
    pad = jnp.zeros((w_in.shape[0], SMALL_W - SSM_HEADS - GLA_GATE_RANK), w_in.dtype)
    return jnp.concatenate([rw, z, xbc, qkv, r, dt, gl, pad], -1).astype(BF16)


def _pad_lanes(w, reps=1):
    w = jnp.repeat(w, reps, axis=-1) if reps > 1 else w
    return jnp.pad(w, [(0, 0)] * (w.ndim - 1) + [(0, LANES - w.shape[-1])])


def kernel(x, c, ctx, c_ctx, ada_w, ada_b, norm1_g, norm2_g, w_in, w_out, rw_mu_prev, rw_mu_next, rw_w0, rw_w2, rw_a0, rw_a2, rw_g2, rw_k_k, rw_k_a, rw_r_k, rw_gn_g, rw_gn_b, ssm_conv_w, ssm_conv_b, ssm_dt_bias, ssm_a_log, ssm_d, ssm_norm_g, gla_ga2, gla_gb, gla_norm_g, moe_rg_w, moe_rg_b, moe_re_w, moe_re_b, moe_w1, moe_w3, moe_w2, final_g):
    depth = ada_w.shape[0]
    bsz, t_lat, _ = x.shape
    t_ctx = ctx.shape[1]
    t_all = t_ctx + t_lat
    assert t_lat // GRID_W * GRID_W == t_lat
    cond_l = jax.nn.silu(c)
    cond_c = jax.nn.silu(c_ctx)[None]
    tm = PROJ_TM
    for l in range(depth):
        ctx_out = l < depth - 1
        mod_l = jnp.split(jnp.dot(cond_l, ada_w[l], precision=_HP) + ada_b[l], 6, -1)
        mod_c = jnp.split(jnp.dot(cond_c, ada_w[l], precision=_HP) + ada_b[l], 6, -1)
        w_in_l = _arrange_w_in(w_in[l])
        slabs = in_proj(x, norm1_g[l], mod_l[0], mod_l[1], w_in_l, _IN_WIDTHS, tm=tm, t_all=t_all, t_off=t_ctx)
        u_rw, z, xbc, u_gla, small = in_proj(ctx, norm1_g[l], mod_c[0], mod_c[1], w_in_l, _IN_WIDTHS, tm=tm,
                                             t_all=t_all, t_off=0, bufs=slabs)
        r, k, v, kk, bonus, rgate, lw, a_sig = rwkv_prep(u_rw, rw_mu_prev[l], rw_mu_next[l], rw_w0[l], rw_w2[l],
                                                         rw_a0[l], rw_a2[l], rw_g2[l], rw_k_k[l], rw_r_k[l],
                                                         t_ctx=t_ctx, tm=tm)
        merge = lambda a: a.reshape((2 * bsz,) + a.shape[2:])
        split = lambda a: a.reshape((2, bsz) + a.shape[1:])
        y_rw = split(rwkv_scan(r, k, v, kk, rw_k_a[l], merge(lw), merge(a_sig), t_ctx=t_ctx))
        conv, dd = ssm_prep(xbc, small, ssm_conv_w[l], ssm_conv_b[l], ssm_dt_bias[l], ssm_a_log[l], t_ctx=t_ctx, tm=tm)
        y_ssd = split(ssd_scan(conv, merge(dd), t_ctx=t_ctx))
        y_gla = split(gla_scan(u_gla, small, gla_ga2[l], gla_gb[l], t_ctx=t_ctx))
        mixed = (y_rw, bonus, rgate, y_ssd, conv, z, y_gla, u_gla,
                 rw_gn_g[l], rw_gn_b[l], ssm_d[l], ssm_norm_g[l], gla_norm_g[l])
        w_out_l = w_out[l].astype(BF16)
        epg = EXPERTS_PER_GROUP
        rw = _pad_lanes(jnp.concatenate([jnp.repeat(moe_rg_w[l], epg, -1), moe_re_w[l]], -1))
        rb = _pad_lanes(jnp.concatenate([jnp.repeat(moe_rg_b[l], epg, -1), moe_re_b[l]], -1)[None])
        by_group = lambda w: w.astype(BF16).reshape((N_GROUPS, epg) + w.shape[1:])
        w1, w3, w2 = by_group(moe_w1[l]), by_group(moe_w3[l]), by_group(moe_w2[l])
        x = out_proj(*mixed, w_out_l, x, mod_l[2], tm=tm, t_off=t_ctx)
        x = moe_block(x, norm2_g[l], mod_l[3], mod_l[4], mod_l[5], rw, rb, w1, w3, w2, tm=MOE_TM)
        if ctx_out:
            ctx = out_proj(*mixed, w_out_l, ctx, mod_c[2], tm=tm, t_off=0)
            ctx = moe_block(ctx, norm2_g[l], mod_c[3], mod_c[4], mod_c[5], rw, rb, w1, w3, w2, tm=MOE_TM)
    return rmsnorm_f32(x, final_g)
```

```python
import functools

import jax
import jax.numpy as jnp
from jax import lax
from jax.experimental import pallas as pl
from jax.experimental.pallas import tpu as pltpu

F32 = jnp.float32
BF16 = jnp.bfloat16

D_MODEL = 1024
GRID_W = 64
NORM_EPS = 1e-6

RW_HEADS = 4
RW_HD = 64
RW_W = RW_HEADS * RW_HD
RW_DECAY_RANK = 64
RW_ICLR_RANK = 64
RW_GATE_RANK = 128
RW_GN_EPS = 64e-5
RW_COLS = 3 * RW_W + RW_DECAY_RANK + RW_ICLR_RANK + RW_GATE_RANK

SSM_HEADS = 8
SSM_HD = 64
SSM_W = SSM_HEADS * SSM_HD
SSM_GROUPS = 2
SSM_STATE = 64
SSM_CONV_CH = SSM_W + 2 * SSM_GROUPS * SSM_STATE
SSM_COLS = SSM_W + SSM_CONV_CH + SSM_HEADS

GLA_HEADS = 4
GLA_DK = 32
GLA_DV = 64
GLA_KW = GLA_HEADS * GLA_DK
GLA_VW = GLA_HEADS * GLA_DV
GLA_GATE_RANK = 16
GLA_TAU = 16.0
GLA_COLS = 2 * GLA_KW + GLA_VW + GLA_GATE_RANK + GLA_VW

N_GROUPS = 4
EXPERTS_PER_GROUP = 4
N_EXPERTS = N_GROUPS * EXPERTS_PER_GROUP
D_EXPERT = 512

LANES = 128
SMALL_W = LANES
VMEM_LIMIT = 56 * 1024 * 1024

PROJ_TM = 256
MOE_TM = 1024
MOE_SUB = 128
MOE_SCATTER_COLS = 256

RW_CHUNK = 64
SSD_CHUNK = 128
GLA_CHUNK = 64

_NT = (((1,), (1,)), ((), ()))
_TN = (((0,), (0,)), ((), ()))


def _dot(a, b):
    return jnp.dot(a, b, preferred_element_type=F32)


def _dot_nt(a, b):
    return lax.dot_general(a, b, _NT, preferred_element_type=F32)


def _dot_tn(a, b):
    return lax.dot_general(a, b, _TN, preferred_element_type=F32)


def _split3(x):
    hi = x.astype(BF16)
    r1 = x - hi.astype(F32)
    mid = r1.astype(BF16)
    lo = (r1 - mid.astype(F32)).astype(BF16)
    return hi, mid, lo


def _cumsum_rows(tri, x):
    hi, mid, lo = _split3(x)
    return _dot(tri, hi) + _dot(tri, mid) + _dot(tri, lo)


def _norm_mod(x, g, shift, scale):
    h = x * lax.rsqrt(jnp.mean(x * x, -1, keepdims=True) + NORM_EPS) * g
    return h * (1.0 + scale) + shift


def _in_proj_kernel(ctx_ref, x_ref, g_ref, sh_ref, sc_ref, w_ref, *out_refs, widths, ctx_blk):
    xin = jnp.where(pl.program_id(1) < ctx_blk, ctx_ref[0], x_ref[0])
    h = _norm_mod(xin, g_ref[...], sh_ref[0], sc_ref[0]).astype(BF16)
    off = 0
    for o_ref, wd in zip(out_refs, widths):
        o_ref[0] = _dot(h, w_ref[:, off:off + wd])
        off += wd


def _mod_spec(mod, tiles_per_batch):
    d = mod.shape[-1]
    if mod.shape[0] == 1:
        return pl.BlockSpec((1, 1, d), lambda i: (0, 0, 0))
    return pl.BlockSpec((1, 1, d), lambda i: (i // tiles_per_batch, 0, 0))


def in_proj(ctx, x, g, shift_c, scale_c, shift_l, scale_l, w, widths, *, tm):
    bsz, t, d = x.shape
    t_ctx = ctx.shape[1]
    assert t % tm == 0 and t_ctx % tm == 0
    ctx_blk = t_ctx // tm
    n_blk = ctx_blk + t // tm
    shift = jnp.concatenate([shift_l, shift_c], 0)[:, None]
    scale = jnp.concatenate([scale_l, scale_c], 0)[:, None]
    mod_spec = pl.BlockSpec((1, 1, d), lambda b, j: (jnp.where(j < ctx_blk, bsz, b), 0, 0))
    return pl.pallas_call(
        functools.partial(_in_proj_kernel, widths=widths, ctx_blk=ctx_blk),
        grid=(bsz, n_blk),
        in_specs=[
            pl.BlockSpec((1, tm, d), lambda b, j: (b, jnp.minimum(j, ctx_blk - 1), 0)),
            pl.BlockSpec((1, tm, d), lambda b, j: (b, jnp.maximum(j - ctx_blk, 0), 0)),
            pl.BlockSpec((1, d), lambda b, j: (0, 0)),
            mod_spec, mod_spec,
            pl.BlockSpec(w.shape, lambda b, j: (0, 0)),
        ],
        out_specs=[pl.BlockSpec((1, tm, wd), lambda b, j: (b, j, 0)) for wd in widths],
        out_shape=[jax.ShapeDtypeStruct((bsz, t_ctx + t, wd), F32) for wd in widths],
        compiler_params=pltpu.CompilerParams(dimension_semantics=("parallel", "parallel"),
                                             vmem_limit_bytes=VMEM_LIMIT),
        name="in_proj",
    )(ctx, x, g[None], shift, scale, w)


def _dot3(a, b):
    a_hi, a_mid, _ = _split3(a)
    b_hi, b_mid, _ = _split3(b)
    return _dot(a_hi, b_hi) + _dot(a_hi, b_mid) + _dot(a_mid, b_hi)


def _head_ones(width, head):
    r = lax.broadcasted_iota(jnp.int32, (width, width), 0)
    c = lax.broadcasted_iota(jnp.int32, (width, width), 1)
    return jnp.where(r // head == c // head, 1.0, 0.0).astype(BF16)


def _head_sum(x, ones):
    return sum(_dot(p, ones) for p in _split3(x))


def _silu(x):
    return x * jax.nn.sigmoid(x)


def _out_proj_kernel(yr_ref, bonus_ref, rgate_ref, ys_ref, xs_ref, z_ref, yg_ref, gr_ref,
                     gng_ref, gnb_ref, dsk_ref, sng_ref, gng2_ref, w_ref, x_ref, gate_ref, o_ref):
    ones_rw = _head_ones(RW_W, RW_HD)
    yr = yr_ref[0, 0] + yr_ref[1, 0]
    mu = _head_sum(yr, ones_rw) * (1.0 / RW_HD)
    yc = yr - mu
    var = _head_sum(yc * yc, ones_rw) * (1.0 / RW_HD)
    a_out = (yc * lax.rsqrt(var + RW_GN_EPS) * gng_ref[...] + gnb_ref[...] + bonus_ref[0]) * rgate_ref[0]
    ysd = ys_ref[0, 0] + ys_ref[1, 0] + dsk_ref[...] * xs_ref[0]
    t = ysd * _silu(z_ref[0])
    b_out = t * lax.rsqrt(jnp.mean(t * t, -1, keepdims=True) + NORM_EPS) * sng_ref[...]
    yg = yg_ref[0, 0] + yg_ref[1, 0]
    ms = _head_sum(yg * yg, _head_ones(GLA_VW, GLA_DV)) * (1.0 / GLA_DV)
    g_out = yg * lax.rsqrt(ms + NORM_EPS) * gng2_ref[...] * _silu(gr_ref[0])
    m = jnp.concatenate([a_out, b_out, g_out], 1).astype(BF16)
    o_ref[...] = x_ref[...] + gate_ref[0] * _dot(m, w_ref[...])


def out_proj(y_rw, bonus, rgate, y_ssd, conv, z, y_gla, gla_slab, gn_g, gn_b, d_skip, ssm_norm_g, gla_norm_g,
             w, x, gate, *, tm, t_off):
    bsz, t, d = x.shape
    n = bsz * t
    assert t % tm == 0 and t_off % tm == 0
    tpb = t // tm
    ob = t_off // tm
    row = lambda i: (i // tpb, ob + i % tpb)
    both = lambda wd: pl.BlockSpec((2, 1, tm, wd), lambda i: (0,) + row(i) + (0,))
    one = lambda wd, cb=0: pl.BlockSpec((1, tm, wd), lambda i: row(i) + (cb,))
    par = lambda wd: pl.BlockSpec((1, wd), lambda i: (0, 0))
    r_blk = (2 * GLA_KW + GLA_VW) // GLA_VW
    out = pl.pallas_call(
        _out_proj_kernel,
        grid=(n // tm,),
        in_specs=[
            both(RW_W), one(RW_W), one(RW_W),
            both(SSM_W), one(SSM_W), one(SSM_W),
            both(GLA_VW), one(GLA_VW, r_blk),
            par(RW_W), par(RW_W), par(SSM_W), par(SSM_W), par(GLA_VW),
            pl.BlockSpec(w.shape, lambda i: (0, 0)),
            pl.BlockSpec((tm, d), lambda i: (i, 0)),
            _mod_spec(gate[:, None], tpb),
        ],
        out_specs=pl.BlockSpec((tm, d), lambda i: (i, 0)),
        out_shape=jax.ShapeDtypeStruct((n, d), F32),
        compiler_params=pltpu.CompilerParams(dimension_semantics=("parallel",), vmem_limit_bytes=VMEM_LIMIT),
        name="out_proj",
    )(y_rw, bonus, rgate, y_ssd, conv, z, y_gla, gla_slab,
      gn_g[None], gn_b[None], jnp.repeat(d_skip, SSM_HD)[None], ssm_norm_g[None], gla_norm_g[None],
      w, x.reshape(n, d), gate[:, None])
    return out.reshape(bsz, t, d)


def _softplus(x):
    return jnp.maximum(x, 0.0) + jnp.log1p(jnp.exp(-jnp.abs(x)))


def _rwkv_prep_kernel(u_ref, up_ref, un_ref, mup_ref, mun_ref, w2_ref, a2_ref, w0_ref, a0_ref, g2_ref, kk_ref, rk_ref,
                      r_o, k_o, v_o, kn_o, bonus_o, gate_o, lw_o, as_o, *, ctx_blk, n_blk):
    j = pl.program_id(1)
    u = u_ref[0]
    tm = u.shape[0]
    seg_start = jnp.logical_or(j == 0, j == ctx_blk)
    seg_end = jnp.logical_or(j == ctx_blk - 1, j == n_blk - 1)
    prev_row = jnp.where(seg_start, 0.0, up_ref[0, 7:8, :])
    next_row = jnp.where(seg_end, 0.0, un_ref[0, 0:1, :])
    row = lax.broadcasted_iota(jnp.int32, u.shape, 0)
    prev = jnp.where(row == 0, prev_row, pltpu.roll(u, 1, 0))
    nxt = jnp.where(row == tm - 1, next_row, pltpu.roll(u, tm - 1, 0))
    s = u + mup_ref[...] * (prev - u) + mun_ref[...] * (nxt - u)
    r, k, v = s[:, :RW_W], s[:, RW_W:2 * RW_W], s[:, 2 * RW_W:3 * RW_W]
    lora_in = s[:, 3 * RW_W:3 * RW_W + RW_DECAY_RANK + RW_ICLR_RANK]
    gl = s[:, 3 * RW_W + RW_DECAY_RANK + RW_ICLR_RANK:]
    ones = _head_ones(RW_W, RW_HD)
    kr = k * kk_ref[...]
    r_o[0] = r
    k_o[0] = k
    v_o[0] = v
    kn_o[0] = kr * lax.rsqrt(_head_sum(kr * kr, ones) + 1e-12)
    bonus_o[0] = _head_sum(r * k * rk_ref[...], ones) * v
    gate_o[0] = _dot3(jax.nn.sigmoid(gl), g2_ref[...])
    lora_t = jnp.tanh(lora_in)
    for d in range(2):
        w_log = -_softplus(-(w0_ref[d] + _dot3(lora_t, w2_ref[d]))) - 0.5
        lw_o[d, 0] = -jnp.exp(w_log)
        as_o[d, 0] = jax.nn.sigmoid(a0_ref[d] + _dot3(lora_in, a2_ref[d]))


def rwkv_prep(u, mu_prev, mu_next, w0, w2, a0, a2, g2, k_k, r_k, *, t_ctx, tm):
    bsz, t_all, wc = u.shape
    assert t_all % tm == 0 and t_ctx % tm == 0 and tm % 8 == 0
    n_blk = t_all // tm
    h8 = tm // 8
    last8 = t_all // 8 - 1
    zpad = jnp.zeros((2, RW_DECAY_RANK, RW_W), F32)
    w2p = jnp.concatenate([w2, zpad], 1)
    a2p = jnp.concatenate([zpad, a2], 1)
    par = lambda a: pl.BlockSpec(a.shape, lambda b, j: (0,) * a.ndim)
    params = [mu_prev[None], mu_next[None], w2p, a2p, w0[:, None], a0[:, None], g2, k_k[None], r_k[None]]
    o1 = pl.BlockSpec((1, tm, RW_W), lambda b, j: (b, j, 0))
    o2 = pl.BlockSpec((2, 1, tm, RW_W), lambda b, j: (0, b, j, 0))
    s1 = jax.ShapeDtypeStruct((bsz, t_all, RW_W), F32)
    s2 = jax.ShapeDtypeStruct((2, bsz, t_all, RW_W), F32)
    return pl.pallas_call(
        functools.partial(_rwkv_prep_kernel, ctx_blk=t_ctx // tm, n_blk=n_blk),
        grid=(bsz, n_blk),
        in_specs=[
            pl.BlockSpec((1, tm, wc), lambda b, j: (b, j, 0)),
            pl.BlockSpec((1, 8, wc), lambda b, j: (b, jnp.maximum(j * h8 - 1, 0), 0)),
            pl.BlockSpec((1, 8, wc), lambda b, j: (b, jnp.minimum((j + 1) * h8, last8), 0)),
        ] + [par(a) for a in params],
        out_specs=[o1] * 6 + [o2] * 2,
        out_shape=[s1] * 6 + [s2] * 2,
        compiler_params=pltpu.CompilerParams(dimension_semantics=("parallel", "parallel"),
                                             vmem_limit_bytes=VMEM_LIMIT),
        name="rwkv_prep",
    )(u, u, u, *params)


def _ssm_prep_kernel(x_ref, xp_ref, xn_ref, sm_ref, cw_ref, cb_ref, dtb_ref, alog_ref, o_ref, dd_ref,
                     *, t_ctx, t_all, halo):
    j = pl.program_id(1)
    tm = x_ref.shape[1]
    t0 = j * tm
    is_ctx = t0 < t_ctx
    seg_lo = jnp.where(is_ctx, 0, t_ctx)
    seg_hi = jnp.where(is_ctx, t_ctx, t_all)
    ctx_i = jnp.where(is_ctx, 1, 0)
    lat_f = jnp.where(is_ctx, 0.0, 1.0)
    ext = jnp.concatenate([xp_ref[0], x_ref[0], xn_ref[0]], 0)
    n_ext = ext.shape[0]
    reps = ext.shape[1] // LANES
    e_row = lax.broadcasted_iota(jnp.int32, (n_ext, LANES), 0)
    t_src = e_row + (t0 - halo)
    in_seg = jnp.logical_and(t_src >= seg_lo, t_src < seg_hi)
    col = e_row % GRID_W
    keep_left = jnp.logical_and(in_seg, (jnp.where(col != GRID_W - 1, 1, 0) | ctx_i) > 0)
    keep_right = jnp.logical_and(in_seg, (jnp.where(col != 0, 1, 0) | ctx_i) > 0)
    wide = lambda m: jnp.concatenate([m] * reps, 1)
    src = {0: jnp.where(wide(in_seg), ext, 0.0), -1: jnp.where(wide(keep_left), ext, 0.0),
           1: jnp.where(wide(keep_right), ext, 0.0)}
    acc = jnp.zeros((tm, ext.shape[1]), F32) + cb_ref[...]
    for dr in (-1, 0, 1):
        for dc in (-1, 0, 1):
            off = halo + GRID_W * dr + dc
            wt = cw_ref[(dr + 1) * 3 + dc + 1:(dr + 1) * 3 + dc + 2, :]
            if dr != 0:
                wt = wt * lat_f
            acc = acc + src[dc][off:off + tm] * wt
    o_ref[0] = _silu(acc)
    dt = sm_ref[0].T[:SSM_HEADS]
    for d in range(2):
        dtp = _softplus(dt + dtb_ref[d])
        da = -jnp.exp(alog_ref[d]) * dtp
        for c in range(tm // SSD_CHUNK):
            cs = slice(c * SSD_CHUNK, (c + 1) * SSD_CHUNK)
            dd_ref[d, 0, c, :SSM_HEADS, :] = da[:, cs]
            dd_ref[d, 0, c, SSM_HEADS:, :] = dtp[:, cs]


def ssm_prep(xbc, small, conv_w, conv_b, dt_bias, a_log, *, t_ctx, tm):
    bsz, t_all, ch = xbc.shape
    halo = 128
    assert t_all % tm == 0 and t_ctx % tm == 0 and tm % halo == 0 and halo > GRID_W and tm % SSD_CHUNK == 0
    n_blk = t_all // tm
    hb = tm // halo
    last = t_all // halo - 1
    par = lambda a: pl.BlockSpec(a.shape, lambda b, j: (0,) * a.ndim)
    params = [conv_w.reshape(9, ch), conv_b[None], dt_bias[:, :, None], a_log[:, :, None]]
    nc = tm // SSD_CHUNK
    return pl.pallas_call(
        functools.partial(_ssm_prep_kernel, t_ctx=t_ctx, t_all=t_all, halo=halo),
        grid=(bsz, n_blk),
        in_specs=[
            pl.BlockSpec((1, tm, ch), lambda b, j: (b, j, 0)),
            pl.BlockSpec((1, halo, ch), lambda b, j: (b, jnp.maximum(j * hb - 1, 0), 0)),
            pl.BlockSpec((1, halo, ch), lambda b, j: (b, jnp.minimum((j + 1) * hb, last), 0)),
            pl.BlockSpec((1, tm, LANES), lambda b, j: (b, j, 0)),
        ] + [par(a) for a in params],
        out_specs=[pl.BlockSpec((1, tm, ch), lambda b, j: (b, j, 0)),
                   pl.BlockSpec((2, 1, nc, 2 * SSM_HEADS, SSD_CHUNK), lambda b, j: (0, b, j, 0, 0))],
        out_shape=[jax.ShapeDtypeStruct((bsz, t_all, ch), F32),
                   jax.ShapeDtypeStruct((2, bsz, t_all // SSD_CHUNK, 2 * SSM_HEADS, SSD_CHUNK), F32)],
        compiler_params=pltpu.CompilerParams(dimension_semantics=("parallel", "parallel"),
                                             vmem_limit_bytes=VMEM_LIMIT),
        name="ssm_prep",
    )(xbc, xbc, xbc, small, *params)


def _route(logits):
    lane = lax.broadcasted_iota(jnp.int32, logits.shape, 1)
    valid = lane < N_EXPERTS
    neg = -jnp.inf
    big = jnp.int32(1 << 20)
    el = pltpu.roll(logits, LANES - N_EXPERTS, 1)
    glm = jnp.where(valid, logits, neg)
    gmax = jnp.max(glm, -1, keepdims=True)
    g_sel_lane = jnp.min(jnp.where(glm == gmax, lane, big), -1, keepdims=True)
    g_sel = g_sel_lane // EXPERTS_PER_GROUP
    in_group = (lane // EXPERTS_PER_GROUP) == g_sel
    gsum = jnp.sum(jnp.where(valid, jnp.exp(glm - gmax), 0.0), -1, keepdims=True) / EXPERTS_PER_GROUP
    p_group = 1.0 / gsum
    elm = jnp.where(in_group & valid, el, neg)
    m1 = jnp.max(elm, -1, keepdims=True)
    i1 = jnp.min(jnp.where(elm == m1, lane, big), -1, keepdims=True)
    elm2 = jnp.where(lane == i1, neg, elm)
    m2 = jnp.max(elm2, -1, keepdims=True)
    i2 = jnp.min(jnp.where(elm2 == m2, lane, big), -1, keepdims=True)
    p2 = jnp.exp(m2 - m1)
    wa = p_group / (1.0 + p2)
    wb = p_group * p2 / (1.0 + p2)
    return jnp.where(lane == i1, wa, 0.0) + jnp.where(lane == i2, wb, 0.0), g_sel


def _moe_kernel(x_ref, g_ref, sh_ref, sc_ref, gate_ref, rw_ref, rb_ref, fg_ref, w1_ref, w3_ref, w2_ref,
                o_ref, h_ref, comb_ref, code_ref, *, final_norm):
    grp = pl.program_id(1)
    tm, d = x_ref.shape

    @pl.when(grp == 0)
    def _():
        h = _norm_mod(x_ref[...], g_ref[...], sh_ref[0], sc_ref[0])
        h_ref[...] = h.astype(BF16)
        h_hi, h_mid, _ = _split3(h)
        w_hi, w_mid, _ = _split3(rw_ref[...])
        logits = _dot(h_hi, w_hi) + _dot(h_hi, w_mid) + _dot(h_mid, w_hi) + rb_ref[...]
        comb, g_sel = _route(logits)
        comb_ref[...] = comb
        lane = lax.broadcasted_iota(jnp.int32, (tm, LANES), 1)
        member = jnp.where(lane == g_sel, 1.0, 0.0)
        r_t = lax.broadcasted_iota(jnp.int32, (tm, tm), 0)
        c_t = lax.broadcasted_iota(jnp.int32, (tm, tm), 1)
        before = jnp.where(c_t < r_t, 1.0, 0.0).astype(BF16)
        rank = _dot(before, member.astype(BF16))
        code = jnp.where(member > 0.0, rank, -1.0)
        code_ref[...] = code.T[:code_ref.shape[0]]
        o_ref[...] = jnp.zeros_like(o_ref)

    crow = code_ref[pl.ds(grp, 1), :]
    count = jnp.max(crow).astype(jnp.int32) + 1
    sub_i = lax.broadcasted_iota(jnp.int32, (MOE_SUB, tm), 0).astype(F32)
    comb_parts = _split3(comb_ref[...])
    lane_c = lax.broadcasted_iota(jnp.int32, (MOE_SUB, LANES), 1)

    def body(j, carry):
        base = (j * MOE_SUB).astype(F32)
        sel = jnp.where(crow - base == sub_i, 1.0, 0.0).astype(BF16)
        hg = _dot(sel, h_ref[...]).astype(BF16)
        cg = sum(_dot(sel, p) for p in comb_parts)
        ysum = jnp.zeros((MOE_SUB, d), F32)
        for e in range(EXPERTS_PER_GROUP):
            a = _dot(hg, w1_ref[0, e])
            b = _dot(hg, w3_ref[0, e])
            hid = (a * jax.nn.sigmoid(a) * b).astype(BF16)
            col = jnp.sum(jnp.where(lane_c == grp * EXPERTS_PER_GROUP + e, cg, 0.0), -1, keepdims=True)
            ysum = ysum + col * _dot(hid, w2_ref[0, e])
        yb = ysum.astype(BF16)
        for c0 in range(0, d, MOE_SCATTER_COLS):
            cs = slice(c0, c0 + MOE_SCATTER_COLS)
            o_ref[:, cs] += _dot_tn(sel, yb[:, cs])
        return carry

    lax.fori_loop(0, (count + MOE_SUB - 1) // MOE_SUB, body, 0)

    @pl.when(grp == N_GROUPS - 1)
    def _():
        out = x_ref[...] + gate_ref[0] * o_ref[...]
        if final_norm:
            out = out * lax.rsqrt(jnp.mean(out * out, -1, keepdims=True) + NORM_EPS) * fg_ref[...]
        o_ref[...] = out


def moe_block(x, g, shift, scale, gate, rw, rb, final_g, w1, w3, w2, *, tm, final_norm):
    bsz, t, d = x.shape
    n = bsz * t
    tm = min(tm, t)
    assert t % tm == 0 and tm % MOE_SUB == 0 and d % MOE_SCATTER_COLS == 0
    tpb = t // tm
    epg = EXPERTS_PER_GROUP

    def mod_spec(mod):
        if mod.shape[0] == 1:
            return pl.BlockSpec((1, 1, d), lambda i, e: (0, 0, 0))
        return pl.BlockSpec((1, 1, d), lambda i, e: (i // tpb, 0, 0))

    const = lambda shape: pl.BlockSpec(shape, lambda i, e: (0,) * len(shape))
    out = pl.pallas_call(
        functools.partial(_moe_kernel, final_norm=final_norm),
        grid=(n // tm, N_GROUPS),
        in_specs=[
            pl.BlockSpec((tm, d), lambda i, e: (i, 0)),
            const((1, d)),
            mod_spec(shift[:, None]), mod_spec(scale[:, None]), mod_spec(gate[:, None]),
            const(rw.shape), const(rb.shape), const((1, d)),
            pl.BlockSpec((1, epg, d, D_EXPERT), lambda i, e: (e, 0, 0, 0)),
            pl.BlockSpec((1, epg, d, D_EXPERT), lambda i, e: (e, 0, 0, 0)),
            pl.BlockSpec((1, epg, D_EXPERT, d), lambda i, e: (e, 0, 0, 0)),
        ],
        out_specs=pl.BlockSpec((tm, d), lambda i, e: (i, 0)),
        out_shape=jax.ShapeDtypeStruct((n, d), F32),
        scratch_shapes=[pltpu.VMEM((tm, d), BF16), pltpu.VMEM((tm, LANES), F32), pltpu.VMEM((8, tm), F32)],
        compiler_params=pltpu.CompilerParams(dimension_semantics=("parallel", "arbitrary"),
                                             vmem_limit_bytes=VMEM_LIMIT),
        name="moe",
    )(x.reshape(n, d), g[None], shift[:, None], scale[:, None], gate[:, None], rw, rb, final_g[None], w1, w3, w2)
    return out.reshape(bsz, t, d)


def _scan_masks(n, rev):
    row = lax.broadcasted_iota(jnp.int32, (n, n), 0)
    col = lax.broadcasted_iota(jnp.int32, (n, n), 1)
    d = (row - col) * jnp.where(rev, -1, 1)
    return d >= 0, d > 0


def _chunk_rows(s, nsub, size, rev):
    return pl.ds(pl.multiple_of(jnp.where(rev, (nsub - 1 - s) * size, s * size), size), size)


def _last_row(x, rev):
    n = x.shape[0]
    return jnp.where(rev, x[0:1], x[n - 1:n])


def _scan_specs(nseq, blk, n_ctx_blk, n_blk, widths_shared, widths_dir):
    def tblock(i, c):
        back = jnp.where(c < n_ctx_blk, n_ctx_blk - 1 - c, n_ctx_blk + n_blk - 1 - c)
        return jnp.where(i >= nseq, back, c)
    def shared_spec(w, col_blk=0):
        return pl.BlockSpec((1, blk, w), lambda i, c: (i % nseq, tblock(i, c), col_blk))
    shared = [shared_spec(*w) if isinstance(w, tuple) else shared_spec(w) for w in widths_shared]
    per_dir = [pl.BlockSpec((1, blk, w), lambda i, c: (i, tblock(i, c), 0)) for w in widths_dir]
    return shared, per_dir, tblock


def _rwkv_kernel(r_ref, k_ref, v_ref, kk_ref, ka_ref, lw_ref, as_ref, y_ref, st_ref, *, nsub, nseq):
    L = RW_CHUNK
    hd = RW_HD
    rev = pl.program_id(0) >= nseq

    @pl.when(pl.program_id(1) == 0)
    def _():
        st_ref[...] = jnp.zeros_like(st_ref)

    incl, strict = _scan_masks(L, rev)
    tri = jnp.where(incl, 1.0, 0.0).astype(BF16)
    zeros = jnp.zeros((L, hd), BF16)
    chains = [(s, h) for s in range(nsub) for h in range(RW_HEADS)]
    rows = [_chunk_rows(s, nsub, L, rev) for s in range(nsub)]

    prep = []
    for s in range(nsub):
        lw = lw_ref[0, rows[s], :]
        c = _cumsum_rows(tri, lw)
        c_end = _last_row(c, rev)
        e_inv = jnp.exp(-c)
        e_end = jnp.exp(c_end - c)
        a_sig = as_ref[0, rows[s], :]
        kk = kk_ref[0, rows[s], :]
        k = k_ref[0, rows[s], :]
        kmod = k * (1.0 + (a_sig - 1.0) * ka_ref[...])
        bv = kk * a_sig
        prep.append(dict(
            at=-kk * jnp.exp(c - lw), rt=r_ref[0, rows[s], :] * jnp.exp(c),
            bt=(bv * e_inv).astype(BF16), kt=(kmod * e_inv).astype(BF16),
            bh=(bv * e_end).astype(BF16), kh=(kmod * e_end).astype(BF16),
            p_end=jnp.exp(c_end), v=v_ref[0, rows[s], :].astype(BF16)))

    def head(name, s, h):
        return prep[s][name][:, h * hd:(h + 1) * hd]

    sc = {}
    for s, h in chains:
        lhs = jnp.concatenate([head("at", s, h), head("rt", s, h)], 0).astype(BF16)
        rhs = jnp.concatenate([head("bt", s, h), head("kt", s, h)], 0)
        sc[s, h] = _dot_nt(lhs, rhs)
    nmat, x, mr = {}, {}, {}
    for s, h in chains:
        m = sc[s, h]
        nmat[s, h] = jnp.where(strict, m[:L, :L], 0.0)
        mak = jnp.where(strict, m[:L, L:], 0.0).astype(BF16)
        mr[s, h] = jnp.concatenate([jnp.where(incl, m[L:, :L], 0.0), jnp.where(incl, m[L:, L:], 0.0)], 1).astype(BF16)
        x[s, h] = jnp.concatenate([head("at", s, h), _dot(mak, head("v", s, h))], 1)
    steps = L.bit_length() - 1
    for i in range(steps):
        for s, h in chains:
            nb = nmat[s, h].astype(BF16)
            xb = x[s, h].astype(BF16)
            if i + 1 < steps:
                prod = _dot(nb, jnp.concatenate([xb, nb], 1))
                x[s, h] = x[s, h] + prod[:, :2 * hd]
                nmat[s, h] = prod[:, 2 * hd:]
            else:
                x[s, h] = x[s, h] + _dot(nb, xb)
    ftop, bhw_t, gt = {}, {}, {}
    for s, h in chains:
        xb = x[s, h].astype(BF16)
        vb = head("v", s, h)
        z = jnp.concatenate([xb, jnp.concatenate([zeros, vb], 1)], 0)
        ftop[s, h] = _dot(mr[s, h], z)
        t1 = _dot_tn(xb, head("bh", s, h))
        bhw_t[s, h] = t1[:hd].astype(BF16)
        gt[s, h] = t1[hd:] + _dot_tn(vb, head("kh", s, h))
    st = [st_ref[h] for h in range(RW_HEADS)]
    for s in range(nsub):
        ys = []
        for h in range(RW_HEADS):
            stb = st[h].astype(BF16)
            q = (head("rt", s, h) + ftop[s, h][:, :hd]).astype(BF16)
            ys.append(_dot_nt(q, stb) + ftop[s, h][:, hd:])
            st[h] = st[h] * head("p_end", s, h) + _dot(stb, bhw_t[s, h]) + gt[s, h]
        y_ref[0, rows[s], :] = jnp.concatenate(ys, 1)
    for h in range(RW_HEADS):
        st_ref[h] = st[h]


def rwkv_scan(r, k, v, kk, k_a, lw, a_sig, *, t_ctx, nsub=4):
    nseq, t, w = r.shape
    blk = RW_CHUNK * nsub
    assert t % blk == 0 and t_ctx % blk == 0
    shared, per_dir, _ = _scan_specs(nseq, blk, t_ctx // blk, t // blk, [w] * 4, [w] * 3)
    return pl.pallas_call(
        functools.partial(_rwkv_kernel, nsub=nsub, nseq=nseq),
        grid=(2 * nseq, t // blk),
        in_specs=shared + [pl.BlockSpec((1, w), lambda i, c: (0, 0))] + per_dir[:2],
        out_specs=per_dir[2],
        out_shape=jax.ShapeDtypeStruct((2 * nseq, t, w), F32),
        scratch_shapes=[pltpu.VMEM((RW_HEADS, RW_HD, RW_HD), F32)],
        compiler_params=pltpu.CompilerParams(dimension_semantics=("parallel", "arbitrary"),
                                             vmem_limit_bytes=VMEM_LIMIT),
        name="rwkv_scan",
    )(r, k, v, kk, k_a[None], lw, a_sig)


def _expand_cols(x, e):
    return sum(_dot(p, e) for p in _split3(x))


def _ssd_kernel(x_ref, b_ref, c_ref, dd_ref, y_ref, st_ref, *, nsub, nseq):
    L = SSD_CHUNK
    nh = SSM_HEADS
    pw = 2 * SSM_HD
    npairs = nh // 2
    pairs_per_group = npairs // SSM_GROUPS
    rev = pl.program_id(0) >= nseq

    @pl.when(pl.program_id(1) == 0)
    def _():
        st_ref[...] = jnp.zeros_like(st_ref)

    incl, _ = _scan_masks(L, rev)
    tri = jnp.where(incl, 1.0, 0.0).astype(BF16)
    h_i = lax.broadcasted_iota(jnp.int32, (nh, nh * SSM_HD), 0)
    c_i = lax.broadcasted_iota(jnp.int32, (nh, nh * SSM_HD), 1)
    e_head = jnp.where(c_i // SSM_HD == h_i, 1.0, 0.0).astype(BF16)
    h_j = lax.broadcasted_iota(jnp.int32, (nh, nh * L), 0)
    c_j = lax.broadcasted_iota(jnp.int32, (nh, nh * L), 1)
    e_wide = jnp.where(c_j // L == h_j, 1.0, 0.0).astype(BF16)
    lane_p = lax.broadcasted_iota(jnp.int32, (L, pw), 1)
    first_half = lane_p < SSM_HD
    r_bd = lax.broadcasted_iota(jnp.int32, (pw, pw), 0)
    c_bd = lax.broadcasted_iota(jnp.int32, (pw, pw), 1)
    bd_mask = (r_bd < SSM_HD) == (c_bd < SSM_HD)

    rows = [_chunk_rows(s, nsub, L, rev) for s in range(nsub)]
    work = []
    for s in range(nsub):
        dd = dd_ref[0, jnp.where(rev, nsub - 1 - s, s)]
        da_parts = _split3(dd[:nh])
        acs_row = sum(_dot_nt(p, tri) for p in da_parts)
        acs_col = sum(_dot_nt(tri, p) for p in da_parts)
        colx = _expand_cols(acs_col, e_wide)
        dtx = sum(_dot_tn(p, e_head) for p in _split3(dd[nh:]))
        xdt = (x_ref[0, rows[s], :] * dtx).astype(BF16)
        bm = b_ref[0, rows[s], :]
        cm = c_ref[0, rows[s], :]
        bm_sw = pltpu.roll(bm, SSM_STATE, 1)
        cm_sw = pltpu.roll(cm, SSM_STATE, 1)
        b2 = [jnp.where(first_half, bm, bm_sw), jnp.where(first_half, bm_sw, bm)]
        c2 = [jnp.where(first_half, cm, cm_sw), jnp.where(first_half, cm_sw, cm)]
        cb = [_dot_nt(cm[:, g * SSM_STATE:(g + 1) * SSM_STATE].astype(BF16),
                      bm[:, g * SSM_STATE:(g + 1) * SSM_STATE].astype(BF16)) for g in range(SSM_GROUPS)]
        work.append(dict(acs_row=acs_row, colx=colx, xdt=xdt, b2=b2, c2=c2, cb=cb))

    y_diag, ce, new, ea = {}, {}, {}, {}
    for s in range(nsub):
        w = work[s]
        for p in range(npairs):
            g = p // pairs_per_group
            h0, h1 = 2 * p, 2 * p + 1
            cx0 = w["colx"][:, h0 * L:(h0 + 1) * L]
            cx1 = w["colx"][:, h1 * L:(h1 + 1) * L]
            s0 = w["cb"][g] * jnp.exp(jnp.where(incl, cx0 - w["acs_row"][h0:h0 + 1, :], -jnp.inf))
            s1 = w["cb"][g] * jnp.exp(jnp.where(incl, cx1 - w["acs_row"][h1:h1 + 1, :], -jnp.inf))
            xp = w["xdt"][:, p * pw:(p + 1) * pw]
            zero = jnp.zeros_like(xp)
            x_bd = jnp.concatenate([jnp.where(first_half, xp, zero), jnp.where(first_half, zero, xp)], 0)
            y_diag[s, p] = _dot(jnp.concatenate([s0, s1], 1).astype(BF16), x_bd)
            col = jnp.where(first_half, cx0[:, :pw], cx1[:, :pw])
            a_end = _last_row(col, rev)
            ce[s, p] = (w["c2"][g] * jnp.exp(col)).astype(BF16)
            be = (w["b2"][g] * jnp.exp(a_end - col)).astype(BF16)
            new[s, p] = jnp.where(bd_mask, _dot_tn(be, xp), 0.0)
            ea[s, p] = jnp.exp(a_end)
    st = [st_ref[p] for p in range(npairs)]
    for s in range(nsub):
        ys = []
        for p in range(npairs):
            ys.append(y_diag[s, p] + _dot(ce[s, p], st[p].astype(BF16)))
            st[p] = st[p] * ea[s, p] + new[s, p]
        y_ref[0, rows[s], :] = jnp.concatenate(ys, 1)
    for p in range(npairs):
        st_ref[p] = st[p]


def ssd_scan(conv, dd, *, t_ctx, nsub=2):
    nseq, t, _ = conv.shape
    w = SSM_W
    blk = SSD_CHUNK * nsub
    assert t % blk == 0 and t_ctx % blk == 0
    assert SSM_STATE == SSM_HD and SSD_CHUNK == 2 * SSM_HD
    gw = SSM_GROUPS * SSM_STATE
    shared, per_dir, tblock = _scan_specs(nseq, blk, t_ctx // blk, t // blk,
                                          [(w, 0), (gw, w // gw), (gw, w // gw + 1)], [w])
    dd_spec = pl.BlockSpec((1, nsub, 2 * SSM_HEADS, SSD_CHUNK), lambda i, c: (i, tblock(i, c), 0, 0))
    return pl.pallas_call(
        functools.partial(_ssd_kernel, nsub=nsub, nseq=nseq),
        grid=(2 * nseq, t // blk),
        in_specs=shared + [dd_spec],
        out_specs=per_dir[0],
        out_shape=jax.ShapeDtypeStruct((2 * nseq, t, w), F32),
        scratch_shapes=[pltpu.VMEM((SSM_HEADS // 2, 2 * SSM_STATE, 2 * SSM_HD), F32)],
        compiler_params=pltpu.CompilerParams(dimension_semantics=("parallel", "arbitrary"),
                                             vmem_limit_bytes=VMEM_LIMIT),
        name="ssd_scan",
    )(conv, conv, conv, dd)


def _gla_kernel(q_ref, k_ref, v_ref, sm_ref, ga_ref, gb_ref, y_ref, st_ref, lg_ref, *, nsub, nseq):
    L = GLA_CHUNK
    rev = pl.program_id(0) >= nseq

    @pl.when(pl.program_id(1) == 0)
    def _():
        st_ref[...] = jnp.zeros_like(st_ref)

    nh = GLA_HEADS
    incl, _ = _scan_masks(L, rev)
    tri = jnp.where(incl, 1.0, 0.0).astype(BF16)
    incl_h = jnp.concatenate([incl] * nh, 0)
    r_q = lax.broadcasted_iota(jnp.int32, (nh * L, GLA_KW), 0)
    c_q = lax.broadcasted_iota(jnp.int32, (nh * L, GLA_KW), 1)
    q_mask = r_q // L == c_q // GLA_DK
    r_s = lax.broadcasted_iota(jnp.int32, (GLA_VW, GLA_KW), 0)
    c_s = lax.broadcasted_iota(jnp.int32, (GLA_VW, GLA_KW), 1)
    st_mask = r_s // GLA_DV == c_s // GLA_DK
    lane_v = lax.broadcasted_iota(jnp.int32, (L, GLA_VW), 1) // GLA_DV
    rows = [_chunk_rows(s, nsub, L, rev) for s in range(nsub)]

    gate_logit = _dot3(sm_ref[0], ga_ref[0]) + gb_ref[0]
    lg_ref[...] = -_softplus(-gate_logit) * (1.0 / GLA_TAU)
    bcs = [_cumsum_rows(tri, lg_ref[rows[s], :]) for s in range(nsub)]
    vb, ke, q_st, qd, kd, eb = [], [], [], [], [], []
    for s in range(nsub):
        mid = jnp.where(rev, bcs[s][L - 1 - L // 2:L - L // 2], bcs[s][L // 2:L // 2 + 1])
        bend = _last_row(bcs[s], rev)
        q = q_ref[0, rows[s], :] * GLA_DK ** -0.5
        k = k_ref[0, rows[s], :]
        vb.append(v_ref[0, rows[s], :].astype(BF16))
        qe = (q * jnp.exp(bcs[s] - mid)).astype(BF16)
        ke.append((k * jnp.exp(mid - bcs[s])).astype(BF16))
        q_st.append(jnp.where(q_mask, jnp.concatenate([qe] * nh, 0), jnp.zeros((), BF16)))
        qd.append((q * jnp.exp(bcs[s])).astype(BF16))
        kd.append((k * jnp.exp(bend - bcs[s])).astype(BF16))
        eb.append(jnp.exp(bend))
    att = [jnp.where(incl_h, _dot_nt(q_st[s], ke[s]), 0.0).astype(BF16) for s in range(nsub)]
    full = [_dot(att[s], vb[s]) for s in range(nsub)]
    o_intra = [sum(jnp.where(lane_v == h, full[s][h * L:(h + 1) * L], 0.0) for h in range(nh)) for s in range(nsub)]
    kv_t = [jnp.where(st_mask, _dot_tn(vb[s], kd[s]), 0.0) for s in range(nsub)]
    st = st_ref[...]
    for s in range(nsub):
        y_ref[0, rows[s], :] = o_intra[s] + _dot_nt(qd[s], st.astype(BF16))
        st = st * eb[s] + kv_t[s]
    st_ref[...] = st


def gla_scan(slab, small, ga2, gb, *, t_ctx, nsub=4):
    nseq, t, _ = slab.shape
    kw, vw = GLA_KW, GLA_VW
    blk = GLA_CHUNK * nsub
    assert t % blk == 0 and t_ctx % blk == 0
    shared, per_dir, _ = _scan_specs(nseq, blk, t_ctx // blk, t // blk,
                                     [(kw, 0), (kw, 1), (vw, 2 * kw // vw), LANES], [vw])
    rank = ga2.shape[1]
    ga_pad = jnp.zeros((2, LANES, kw), F32).at[:, SSM_HEADS:SSM_HEADS + rank].set(ga2)
    return pl.pallas_call(
        functools.partial(_gla_kernel, nsub=nsub, nseq=nseq),
        grid=(2 * nseq, t // blk),
        in_specs=shared + [pl.BlockSpec((1, LANES, kw), lambda i, c: (i // nseq, 0, 0)),
                           pl.BlockSpec((1, 1, kw), lambda i, c: (i // nseq, 0, 0))],
        out_specs=per_dir[0],
        out_shape=jax.ShapeDtypeStruct((2 * nseq, t, vw), F32),
        scratch_shapes=[pltpu.VMEM((GLA_VW, GLA_KW), F32), pltpu.VMEM((blk, kw), F32)],
        compiler_params=pltpu.CompilerParams(dimension_semantics=("parallel", "arbitrary"),
                                             vmem_limit_bytes=VMEM_LIMIT),
        name="gla_scan",
    )(slab, slab, slab, small, ga_pad, gb[:, None])


_HP = lax.Precision.HIGHEST


_IN_WIDTHS = (RW_COLS, SSM_W, SSM_CONV_CH, 2 * GLA_KW + 2 * GLA_VW, SMALL_W)


def _arrange_w_in(w_in):
    rw, ssm, gla = jnp.split(w_in, [RW_COLS, RW_COLS + SSM_COLS], -1)
    z, xbc, dt = jnp.split(ssm, [SSM_W, SSM_W + SSM_CONV_CH], -1)
    qkv, gl, r = jnp.split(gla, [2 * GLA_KW + GLA_VW, 2 * GLA_KW + GLA_VW + GLA_GATE_RANK], -1)
    pad = jnp.zeros((w_in.shape[0], SMALL_W - SSM_HEADS - GLA_GATE_RANK), w_in.dtype)
    return jnp.concatenate([rw, z, xbc, qkv, r, dt, gl, pad], -1).astype(BF16)


def _pad_lanes(w, reps=1):
    w = jnp.repeat(w, reps, axis=-1) if reps > 1 else w
    return jnp.pad(w, [(0, 0)] * (w.ndim - 1) + [(0, LANES - w.shape[-1])])


def kernel(x, c, ctx, c_ctx, ada_w, ada_b, norm1_g, norm2_g, w_in, w_out, rw_mu_prev, rw_mu_next, rw_w0, rw_w2, rw_a0, rw_a2, rw_g2, rw_k_k, rw_k_a, rw_r_k, rw_gn_g, rw_gn_b, ssm_conv_w, ssm_conv_b, ssm_dt_bias, ssm_a_log, ssm_d, ssm_norm_g, gla_ga2, gla_gb, gla_norm_g, moe_rg_w, moe_rg_b, moe_re_w, moe_re_b, moe_w1, moe_w3, moe_w2, final_g):
    depth = ada_w.shape[0]
    bsz, t_lat, _ = x.shape
    t_ctx = ctx.shape[1]
    assert t_lat % GRID_W == 0
    cond_l = jax.nn.silu(c)
    cond_c = jax.nn.silu(c_ctx)[None]
    tm = PROJ_TM
    for l in range(depth):
        ctx_out = l < depth - 1
        mod_l = jnp.split(jnp.dot(cond_l, ada_w[l], precision=_HP) + ada_b[l], 6, -1)
        mod_c = jnp.split(jnp.dot(cond_c, ada_w[l], precision=_HP) + ada_b[l], 6, -1)
        w_in_l = _arrange_w_in(w_in[l])
        u_rw, z, xbc, u_gla, small = in_proj(ctx, x, norm1_g[l], mod_c[0], mod_c[1], mod_l[0], mod_l[1], w_in_l,
                                             _IN_WIDTHS, tm=tm)
        r, k, v, kk, bonus, rgate, lw, a_sig = rwkv_prep(u_rw, rw_mu_prev[l], rw_mu_next[l], rw_w0[l], rw_w2[l],
                                                         rw_a0[l], rw_a2[l], rw_g2[l], rw_k_k[l], rw_r_k[l],
                                                         t_ctx=t_ctx, tm=tm)
        merge = lambda a: a.reshape((2 * bsz,) + a.shape[2:])
        split = lambda a: a.reshape((2, bsz) + a.shape[1:])
        y_rw = split(rwkv_scan(r, k, v, kk, rw_k_a[l], merge(lw), merge(a_sig), t_ctx=t_ctx))
        conv, dd = ssm_prep(xbc, small, ssm_conv_w[l], ssm_conv_b[l], ssm_dt_bias[l], ssm_a_log[l], t_ctx=t_ctx, tm=tm)
        y_ssd = split(ssd_scan(conv, merge(dd), t_ctx=t_ctx))
        y_gla = split(gla_scan(u_gla, small, gla_ga2[l], gla_gb[l], t_ctx=t_ctx))
        mixed = (y_rw, bonus, rgate, y_ssd, conv, z, y_gla, u_gla,
                 rw_gn_g[l], rw_gn_b[l], ssm_d[l], ssm_norm_g[l], gla_norm_g[l])
        w_out_l = w_out[l].astype(BF16)
        epg = EXPERTS_PER_GROUP
        rw = _pad_lanes(jnp.concatenate([jnp.repeat(moe_rg_w[l], epg, -1), moe_re_w[l]], -1))
        rb = _pad_lanes(jnp.concatenate([jnp.repeat(moe_rg_b[l], epg, -1), moe_re_b[l]], -1)[None])
        by_group = lambda w: w.astype(BF16).reshape((N_GROUPS, epg) + w.shape[1:])
        w1, w3, w2 = by_group(moe_w1[l]), by_group(moe_w3[l]), by_group(moe_w2[l])
        x = out_proj(*mixed, w_out_l, x, mod_l[2], tm=tm, t_off=t_ctx)
        x = moe_block(x, norm2_g[l], mod_l[3], mod_l[4], mod_l[5], rw, rb, final_g, w1, w3, w2, tm=MOE_TM,
                      final_norm=not ctx_out)
        if ctx_out:
            ctx = out_proj(*mixed, w_out_l, ctx, mod_c[2], tm=tm, t_off=0)
            ctx = moe_block(ctx, norm2_g[l], mod_c[3], mod_c[4], mod_c[5], rw, rb, final_g, w1, w3, w2, tm=MOE_TM,
                            final_norm=False)
    return x
```

```python
import functools

import jax
import jax.numpy as jnp
from jax import lax
from jax.experimental import pallas as pl
from jax.experimental.pallas import tpu as pltpu

F32 = jnp.float32
BF16 = jnp.bfloat16

D_MODEL = 1024
GRID_W = 64
NORM_EPS = 1e-6

RW_HEADS = 4
RW_HD = 64
RW_W = RW_HEADS * RW_HD
RW_DECAY_RANK = 64
RW_ICLR_RANK = 64
RW_GATE_RANK = 128
RW_GN_EPS = 64e-5
RW_COLS = 3 * RW_W + RW_DECAY_RANK + RW_ICLR_RANK + RW_GATE_RANK

SSM_HEADS = 8
SSM_HD = 64
SSM_W = SSM_HEADS * SSM_HD
SSM_GROUPS = 2
SSM_STATE = 64
SSM_CONV_CH = SSM_W + 2 * SSM_GROUPS * SSM_STATE
SSM_COLS = SSM_W + SSM_CONV_CH + SSM_HEADS

GLA_HEADS = 4
GLA_DK = 32
GLA_DV = 64
GLA_KW = GLA_HEADS * GLA_DK
GLA_VW = GLA_HEADS * GLA_DV
GLA_GATE_RANK = 16
GLA_TAU = 16.0
GLA_COLS = 2 * GLA_KW + GLA_VW + GLA_GATE_RANK + GLA_VW

N_GROUPS = 4
EXPERTS_PER_GROUP = 4
N_EXPERTS = N_GROUPS * EXPERTS_PER_GROUP
D_EXPERT = 512

LANES = 128
SMALL_W = LANES
VMEM_LIMIT = 56 * 1024 * 1024

PROJ_TM = 256
MOE_TM = 1024
MOE_SUB = 128
MOE_SCATTER_COLS = 256
CONV_ROWS = 64

RW_CHUNK = 64
SSD_CHUNK = 128
GLA_CHUNK = 64

_NT = (((1,), (1,)), ((), ()))
_TN = (((0,), (0,)), ((), ()))


def _dot(a, b):
    return jnp.dot(a, b, preferred_element_type=F32)


def _dot_nt(a, b):
    return lax.dot_general(a, b, _NT, preferred_element_type=F32)


def _dot_tn(a, b):
    return lax.dot_general(a, b, _TN, preferred_element_type=F32)


def _split3(x):
    hi = x.astype(BF16)
    r1 = x - hi.astype(F32)
    mid = r1.astype(BF16)
    lo = (r1 - mid.astype(F32)).astype(BF16)
    return hi, mid, lo


def _cumsum_rows(tri, x):
    hi, mid, lo = _split3(x)
    return _dot(tri, hi) + _dot(tri, mid) + _dot(tri, lo)


def _norm_mod(x, g, shift, scale):
    h = x * lax.rsqrt(jnp.mean(x * x, -1, keepdims=True) + NORM_EPS) * g
    return h * (1.0 + scale) + shift


def _in_proj_kernel(ctx_ref, x_ref, g_ref, sh_ref, sc_ref, w_ref, *out_refs, widths, ctx_blk):
    xin = jnp.where(pl.program_id(1) < ctx_blk, ctx_ref[0], x_ref[0])
    h = _norm_mod(xin, g_ref[...], sh_ref[0], sc_ref[0]).astype(BF16)
    off = 0
    for o_ref, wd in zip(out_refs, widths):
        o_ref[0] = _dot(h, w_ref[:, off:off + wd])
        off += wd


def _mod_spec(mod, tiles_per_batch):
    d = mod.shape[-1]
    if mod.shape[0] == 1:
        return pl.BlockSpec((1, 1, d), lambda i: (0, 0, 0))
    return pl.BlockSpec((1, 1, d), lambda i: (i // tiles_per_batch, 0, 0))


def in_proj(ctx, x, g, shift_c, scale_c, shift_l, scale_l, w, widths, *, tm):
    bsz, t, d = x.shape
    t_ctx = ctx.shape[1]
    assert t % tm == 0 and t_ctx % tm == 0
    ctx_blk = t_ctx // tm
    n_blk = ctx_blk + t // tm
    shift = jnp.concatenate([shift_l, shift_c], 0)[:, None]
    scale = jnp.concatenate([scale_l, scale_c], 0)[:, None]
    mod_spec = pl.BlockSpec((1, 1, d), lambda b, j: (jnp.where(j < ctx_blk, bsz, b), 0, 0))
    return pl.pallas_call(
        functools.partial(_in_proj_kernel, widths=widths, ctx_blk=ctx_blk),
        grid=(bsz, n_blk),
        in_specs=[
            pl.BlockSpec((1, tm, d), lambda b, j: (b, jnp.minimum(j, ctx_blk - 1), 0)),
            pl.BlockSpec((1, tm, d), lambda b, j: (b, jnp.maximum(j - ctx_blk, 0), 0)),
            pl.BlockSpec((1, d), lambda b, j: (0, 0)),
            mod_spec, mod_spec,
            pl.BlockSpec(w.shape, lambda b, j: (0, 0)),
        ],
        out_specs=[pl.BlockSpec((1, tm, wd), lambda b, j: (b, j, 0)) for wd in widths],
        out_shape=[jax.ShapeDtypeStruct((bsz, t_ctx + t, wd), F32) for wd in widths],
        compiler_params=pltpu.CompilerParams(dimension_semantics=("parallel", "parallel"),
                                             vmem_limit_bytes=VMEM_LIMIT),
        name="in_proj",
    )(ctx, x, g[None], shift, scale, w)


def _dot3(a, b):
    a_hi, a_mid, _ = _split3(a)
    b_hi, b_mid, _ = _split3(b)
    return _dot(a_hi, b_hi) + _dot(a_hi, b_mid) + _dot(a_mid, b_hi)


def _head_ones(width, head):
    r = lax.broadcasted_iota(jnp.int32, (width, width), 0)
    c = lax.broadcasted_iota(jnp.int32, (width, width), 1)
    return jnp.where(r // head == c // head, 1.0, 0.0).astype(BF16)


def _head_sum(x, ones):
    return sum(_dot(p, ones) for p in _split3(x))


def _silu(x):
    return x * jax.nn.sigmoid(x)


def _out_proj_kernel(yr_ref, bonus_ref, rgate_ref, ys_ref, xs_ref, z_ref, yg_ref, gr_ref,
                     gng_ref, gnb_ref, dsk_ref, sng_ref, gng2_ref, w_ref, x_ref, gate_ref, o_ref):
    ones_rw = _head_ones(RW_W, RW_HD)
    yr = yr_ref[0, 0] + yr_ref[1, 0]
    mu = _head_sum(yr, ones_rw) * (1.0 / RW_HD)
    yc = yr - mu
    var = _head_sum(yc * yc, ones_rw) * (1.0 / RW_HD)
    a_out = (yc * lax.rsqrt(var + RW_GN_EPS) * gng_ref[...] + gnb_ref[...] + bonus_ref[0]) * rgate_ref[0]
    ysd = ys_ref[0, 0] + ys_ref[1, 0] + dsk_ref[...] * xs_ref[0]
    t = ysd * _silu(z_ref[0])
    b_out = t * lax.rsqrt(jnp.mean(t * t, -1, keepdims=True) + NORM_EPS) * sng_ref[...]
    yg = yg_ref[0, 0] + yg_ref[1, 0]
    ms = _head_sum(yg * yg, _head_ones(GLA_VW, GLA_DV)) * (1.0 / GLA_DV)
    g_out = yg * lax.rsqrt(ms + NORM_EPS) * gng2_ref[...] * _silu(gr_ref[0])
    m = jnp.concatenate([a_out, b_out, g_out], 1).astype(BF16)
    o_ref[...] = x_ref[...] + gate_ref[0] * _dot(m, w_ref[...])


def out_proj(y_rw, bonus, rgate, y_ssd, conv, z, y_gla, gla_slab, gn_g, gn_b, d_skip, ssm_norm_g, gla_norm_g,
             w, x, gate, *, tm, t_off):
    bsz, t, d = x.shape
    n = bsz * t
    assert t % tm == 0 and t_off % tm == 0
    tpb = t // tm
    ob = t_off // tm
    row = lambda i: (i // tpb, ob + i % tpb)
    both = lambda wd: pl.BlockSpec((2, 1, tm, wd), lambda i: (0,) + row(i) + (0,))
    one = lambda wd, cb=0: pl.BlockSpec((1, tm, wd), lambda i: row(i) + (cb,))
    par = lambda wd: pl.BlockSpec((1, wd), lambda i: (0, 0))
    r_blk = (2 * GLA_KW + GLA_VW) // GLA_VW
    out = pl.pallas_call(
        _out_proj_kernel,
        grid=(n // tm,),
        in_specs=[
            both(RW_W), one(RW_W), one(RW_W),
            both(SSM_W), one(SSM_W), one(SSM_W),
            both(GLA_VW), one(GLA_VW, r_blk),
            par(RW_W), par(RW_W), par(SSM_W), par(SSM_W), par(GLA_VW),
            pl.BlockSpec(w.shape, lambda i: (0, 0)),
            pl.BlockSpec((tm, d), lambda i: (i, 0)),
            _mod_spec(gate[:, None], tpb),
        ],
        out_specs=pl.BlockSpec((tm, d), lambda i: (i, 0)),
        out_shape=jax.ShapeDtypeStruct((n, d), F32),
        compiler_params=pltpu.CompilerParams(dimension_semantics=("parallel",), vmem_limit_bytes=VMEM_LIMIT),
        name="out_proj",
    )(y_rw, bonus, rgate, y_ssd, conv, z, y_gla, gla_slab,
      gn_g[None], gn_b[None], jnp.repeat(d_skip, SSM_HD)[None], ssm_norm_g[None], gla_norm_g[None],
      w, x.reshape(n, d), gate[:, None])
    return out.reshape(bsz, t, d)


def _softplus(x):
    return jnp.maximum(x, 0.0) + jnp.log1p(jnp.exp(-jnp.abs(x)))


def _rwkv_prep_kernel(u_ref, up_ref, un_ref, mup_ref, mun_ref, w2_ref, a2_ref, w0_ref, a0_ref, g2_ref, kk_ref, rk_ref,
                      r_o, k_o, v_o, kn_o, bonus_o, gate_o, lw_o, as_o, *, ctx_blk, n_blk):
    j = pl.program_id(1)
    u = u_ref[0]
    tm = u.shape[0]
    seg_start = jnp.logical_or(j == 0, j == ctx_blk)
    seg_end = jnp.logical_or(j == ctx_blk - 1, j == n_blk - 1)
    prev_row = jnp.where(seg_start, 0.0, up_ref[0, 7:8, :])
    next_row = jnp.where(seg_end, 0.0, un_ref[0, 0:1, :])
    row = lax.broadcasted_iota(jnp.int32, u.shape, 0)
    prev = jnp.where(row == 0, prev_row, pltpu.roll(u, 1, 0))
    nxt = jnp.where(row == tm - 1, next_row, pltpu.roll(u, tm - 1, 0))
    s = u + mup_ref[...] * (prev - u) + mun_ref[...] * (nxt - u)
    r, k, v = s[:, :RW_W], s[:, RW_W:2 * RW_W], s[:, 2 * RW_W:3 * RW_W]
    lora_in = s[:, 3 * RW_W:3 * RW_W + RW_DECAY_RANK + RW_ICLR_RANK]
    gl = s[:, 3 * RW_W + RW_DECAY_RANK + RW_ICLR_RANK:]
    ones = _head_ones(RW_W, RW_HD)
    kr = k * kk_ref[...]
    r_o[0] = r
    k_o[0] = k
    v_o[0] = v
    kn_o[0] = kr * lax.rsqrt(_head_sum(kr * kr, ones) + 1e-12)
    bonus_o[0] = _head_sum(r * k * rk_ref[...], ones) * v
    gate_o[0] = _dot3(jax.nn.sigmoid(gl), g2_ref[...])
    lora_t = jnp.tanh(lora_in)
    for d in range(2):
        w_log = -_softplus(-(w0_ref[d] + _dot3(lora_t, w2_ref[d]))) - 0.5
        lw_o[d, 0] = -jnp.exp(w_log)
        as_o[d, 0] = jax.nn.sigmoid(a0_ref[d] + _dot3(lora_in, a2_ref[d]))


def rwkv_prep(u, mu_prev, mu_next, w0, w2, a0, a2, g2, k_k, r_k, *, t_ctx, tm):
    bsz, t_all, wc = u.shape
    assert t_all % tm == 0 and t_ctx % tm == 0 and tm % 8 == 0
    n_blk = t_all // tm
    h8 = tm // 8
    last8 = t_all // 8 - 1
    zpad = jnp.zeros((2, RW_DECAY_RANK, RW_W), F32)
    w2p = jnp.concatenate([w2, zpad], 1)
    a2p = jnp.concatenate([zpad, a2], 1)
    par = lambda a: pl.BlockSpec(a.shape, lambda b, j: (0,) * a.ndim)
    params = [mu_prev[None], mu_next[None], w2p, a2p, w0[:, None], a0[:, None], g2, k_k[None], r_k[None]]
    o1 = pl.BlockSpec((1, tm, RW_W), lambda b, j: (b, j, 0))
    o2 = pl.BlockSpec((2, 1, tm, RW_W), lambda b, j: (0, b, j, 0))
    s1 = jax.ShapeDtypeStruct((bsz, t_all, RW_W), F32)
    s2 = jax.ShapeDtypeStruct((2, bsz, t_all, RW_W), F32)
    return pl.pallas_call(
        functools.partial(_rwkv_prep_kernel, ctx_blk=t_ctx // tm, n_blk=n_blk),
        grid=(bsz, n_blk),
        in_specs=[
            pl.BlockSpec((1, tm, wc), lambda b, j: (b, j, 0)),
            pl.BlockSpec((1, 8, wc), lambda b, j: (b, jnp.maximum(j * h8 - 1, 0), 0)),
            pl.BlockSpec((1, 8, wc), lambda b, j: (b, jnp.minimum((j + 1) * h8, last8), 0)),
        ] + [par(a) for a in params],
        out_specs=[o1] * 6 + [o2] * 2,
        out_shape=[s1] * 6 + [s2] * 2,
        compiler_params=pltpu.CompilerParams(dimension_semantics=("parallel", "parallel"),
                                             vmem_limit_bytes=VMEM_LIMIT),
        name="rwkv_prep",
    )(u, u, u, *params)


def _ssm_prep_kernel(x_ref, xp_ref, xn_ref, sm_ref, cw_ref, cb_ref, dtb_ref, alog_ref, o_ref, dd_ref, src_ref,
                     *, t_ctx, t_all, halo):
    j = pl.program_id(1)
    tm = x_ref.shape[1]
    t0 = j * tm
    is_ctx = t0 < t_ctx
    seg_lo = jnp.where(is_ctx, 0, t_ctx)
    seg_hi = jnp.where(is_ctx, t_ctx, t_all)
    ctx_i = jnp.where(is_ctx, 1, 0)
    lat_f = jnp.where(is_ctx, 0.0, 1.0)
    ch = x_ref.shape[2]
    reps = ch // LANES
    for piece, start in ((xp_ref, 0), (x_ref, halo), (xn_ref, halo + tm)):
        n = piece.shape[1]
        e_row = lax.broadcasted_iota(jnp.int32, (n, LANES), 0) + start
        t_src = e_row + (t0 - halo)
        in_seg = jnp.logical_and(t_src >= seg_lo, t_src < seg_hi)
        col = e_row % GRID_W
        keep = {0: in_seg,
                -1: jnp.logical_and(in_seg, (jnp.where(col != GRID_W - 1, 1, 0) | ctx_i) > 0),
                1: jnp.logical_and(in_seg, (jnp.where(col != 0, 1, 0) | ctx_i) > 0)}
        val = piece[0]
        for dc in (-1, 0, 1):
            src_ref[dc + 1, start:start + n, :] = jnp.where(jnp.concatenate([keep[dc]] * reps, 1), val, 0.0)
    taps = [(dr, dc) for dr in (-1, 0, 1) for dc in (-1, 0, 1)]
    for c0 in range(0, ch, LANES):
        cs = slice(c0, c0 + LANES)
        wts = []
        for dr, dc in taps:
            wt = cw_ref[(dr + 1) * 3 + dc + 1:(dr + 1) * 3 + dc + 2, cs]
            wts.append(wt * lat_f if dr != 0 else wt)
        for r0 in range(0, tm, CONV_ROWS):
            acc = jnp.zeros((CONV_ROWS, LANES), F32) + cb_ref[:, cs]
            for (dr, dc), wt in zip(taps, wts):
                off = halo + GRID_W * dr + dc + r0
                acc = acc + src_ref[dc + 1, off:off + CONV_ROWS, cs] * wt
            o_ref[0, r0:r0 + CONV_ROWS, cs] = _silu(acc)
    dt = sm_ref[0].T[:SSM_HEADS]
    for d in range(2):
        dtp = _softplus(dt + dtb_ref[d])
        da = -jnp.exp(alog_ref[d]) * dtp
        for c in range(tm // SSD_CHUNK):
            cs = slice(c * SSD_CHUNK, (c + 1) * SSD_CHUNK)
            dd_ref[d, 0, c, :SSM_HEADS, :] = da[:, cs]
            dd_ref[d, 0, c, SSM_HEADS:, :] = dtp[:, cs]


def ssm_prep(xbc, small, conv_w, conv_b, dt_bias, a_log, *, t_ctx, tm):
    bsz, t_all, ch = xbc.shape
    halo = 128
    assert t_all % tm == 0 and t_ctx % tm == 0 and tm % halo == 0 and halo > GRID_W and tm % SSD_CHUNK == 0
    n_blk = t_all // tm
    hb = tm // halo
    last = t_all // halo - 1
    par = lambda a: pl.BlockSpec(a.shape, lambda b, j: (0,) * a.ndim)
    params = [conv_w.reshape(9, ch), conv_b[None], dt_bias[:, :, None], a_log[:, :, None]]
    nc = tm // SSD_CHUNK
    return pl.pallas_call(
        functools.partial(_ssm_prep_kernel, t_ctx=t_ctx, t_all=t_all, halo=halo),
        grid=(bsz, n_blk),
        in_specs=[
            pl.BlockSpec((1, tm, ch), lambda b, j: (b, j, 0)),
            pl.BlockSpec((1, halo, ch), lambda b, j: (b, jnp.maximum(j * hb - 1, 0), 0)),
            pl.BlockSpec((1, halo, ch), lambda b, j: (b, jnp.minimum((j + 1) * hb, last), 0)),
            pl.BlockSpec((1, tm, LANES), lambda b, j: (b, j, 0)),
        ] + [par(a) for a in params],
        out_specs=[pl.BlockSpec((1, tm, ch), lambda b, j: (b, j, 0)),
                   pl.BlockSpec((2, 1, nc, 2 * SSM_HEADS, SSD_CHUNK), lambda b, j: (0, b, j, 0, 0))],
        out_shape=[jax.ShapeDtypeStruct((bsz, t_all, ch), F32),
                   jax.ShapeDtypeStruct((2, bsz, t_all // SSD_CHUNK, 2 * SSM_HEADS, SSD_CHUNK), F32)],
        scratch_shapes=[pltpu.VMEM((3, tm + 2 * halo, ch), F32)],
        compiler_params=pltpu.CompilerParams(dimension_semantics=("parallel", "parallel"),
                                             vmem_limit_bytes=VMEM_LIMIT),
        name="ssm_prep",
    )(xbc, xbc, xbc, small, *params)


def _route(logits):
    lane = lax.broadcasted_iota(jnp.int32, logits.shape, 1)
    valid = lane < N_EXPERTS
    neg = -jnp.inf
    big = jnp.int32(1 << 20)
    el = pltpu.roll(logits, LANES - N_EXPERTS, 1)
    glm = jnp.where(valid, logits, neg)
    gmax = jnp.max(glm, -1, keepdims=True)
    g_sel_lane = jnp.min(jnp.where(glm == gmax, lane, big), -1, keepdims=True)
    g_sel = g_sel_lane // EXPERTS_PER_GROUP
    in_group = (lane // EXPERTS_PER_GROUP) == g_sel
    gsum = jnp.sum(jnp.where(valid, jnp.exp(glm - gmax), 0.0), -1, keepdims=True) / EXPERTS_PER_GROUP
    p_group = 1.0 / gsum
    elm = jnp.where(in_group & valid, el, neg)
    m1 = jnp.max(elm, -1, keepdims=True)
    i1 = jnp.min(jnp.where(elm == m1, lane, big), -1, keepdims=True)
    elm2 = jnp.where(lane == i1, neg, elm)
    m2 = jnp.max(elm2, -1, keepdims=True)
    i2 = jnp.min(jnp.where(elm2 == m2, lane, big), -1, keepdims=True)
    p2 = jnp.exp(m2 - m1)
    wa = p_group / (1.0 + p2)
    wb = p_group * p2 / (1.0 + p2)
    return jnp.where(lane == i1, wa, 0.0) + jnp.where(lane == i2, wb, 0.0), g_sel


def _moe_kernel(x_ref, g_ref, sh_ref, sc_ref, gate_ref, rw_ref, rb_ref, fg_ref, w1_ref, w3_ref, w2_ref,
                o_ref, h_ref, comb_ref, code_ref, *, final_norm):
    grp = pl.program_id(1)
    tm, d = x_ref.shape

    @pl.when(grp == 0)
    def _():
        h = _norm_mod(x_ref[...], g_ref[...], sh_ref[0], sc_ref[0])
        h_ref[...] = h.astype(BF16)
        h_hi, h_mid, _ = _split3(h)
        w_hi, w_mid, _ = _split3(rw_ref[...])
        logits = _dot(h_hi, w_hi) + _dot(h_hi, w_mid) + _dot(h_mid, w_hi) + rb_ref[...]
        comb, g_sel = _route(logits)
        comb_ref[...] = comb
        lane = lax.broadcasted_iota(jnp.int32, (tm, LANES), 1)
        member = jnp.where(lane == g_sel, 1.0, 0.0)
        r_t = lax.broadcasted_iota(jnp.int32, (tm, tm), 0)
        c_t = lax.broadcasted_iota(jnp.int32, (tm, tm), 1)
        before = jnp.where(c_t < r_t, 1.0, 0.0).astype(BF16)
        rank = _dot(before, member.astype(BF16))
        code = jnp.where(member > 0.0, rank, -1.0)
        code_ref[...] = code.T[:code_ref.shape[0]]
        o_ref[...] = jnp.zeros_like(o_ref)

    crow = code_ref[pl.ds(grp, 1), :]
    count = jnp.max(crow).astype(jnp.int32) + 1
    comb_parts = _split3(comb_ref[...])

    def sub_block(first, sub):
        sub_i = lax.broadcasted_iota(jnp.int32, (sub, tm), 0).astype(F32)
        lane_c = lax.broadcasted_iota(jnp.int32, (sub, LANES), 1)
        sel = jnp.where(crow - first.astype(F32) == sub_i, 1.0, 0.0).astype(BF16)
        hg = _dot(sel, h_ref[...]).astype(BF16)
        cg = sum(_dot(sel, p) for p in comb_parts)
        ysum = jnp.zeros((sub, d), F32)
        for e in range(EXPERTS_PER_GROUP):
            a = _dot(hg, w1_ref[0, e])
            b = _dot(hg, w3_ref[0, e])
            hid = (a * jax.nn.sigmoid(a) * b).astype(BF16)
            col = jnp.sum(jnp.where(lane_c == grp * EXPERTS_PER_GROUP + e, cg, 0.0), -1, keepdims=True)
            ysum = ysum + col * _dot(hid, w2_ref[0, e])
        yb = ysum.astype(BF16)
        for c0 in range(0, d, MOE_SCATTER_COLS):
            cs = slice(c0, c0 + MOE_SCATTER_COLS)
            o_ref[:, cs] += _dot_tn(sel, yb[:, cs])

    def body(j, carry):
        sub_block(j * MOE_SUB, MOE_SUB)
        return carry

    full = count // MOE_SUB
    rest = count - full * MOE_SUB
    lax.fori_loop(0, full, body, 0)

    @pl.when(rest > MOE_SUB // 2)
    def _():
        sub_block(full * MOE_SUB, MOE_SUB)

    @pl.when(jnp.logical_and(rest > 0, rest <= MOE_SUB // 2))
    def _():
        sub_block(full * MOE_SUB, MOE_SUB // 2)

    @pl.when(grp == N_GROUPS - 1)
    def _():
        out = x_ref[...] + gate_ref[0] * o_ref[...]
        if final_norm:
            out = out * lax.rsqrt(jnp.mean(out * out, -1, keepdims=True) + NORM_EPS) * fg_ref[...]
        o_ref[...] = out


def moe_block(x, g, shift, scale, gate, rw, rb, final_g, w1, w3, w2, *, tm, final_norm):
    bsz, t, d = x.shape
    n = bsz * t
    tm = min(tm, t)
    assert t % tm == 0 and tm % MOE_SUB == 0 and d % MOE_SCATTER_COLS == 0
    tpb = t // tm
    epg = EXPERTS_PER_GROUP

    def mod_spec(mod):
        if mod.shape[0] == 1:
            return pl.BlockSpec((1, 1, d), lambda i, e: (0, 0, 0))
        return pl.BlockSpec((1, 1, d), lambda i, e: (i // tpb, 0, 0))

    const = lambda shape: pl.BlockSpec(shape, lambda i, e: (0,) * len(shape))
    out = pl.pallas_call(
        functools.partial(_moe_kernel, final_norm=final_norm),
        grid=(n // tm, N_GROUPS),
        in_specs=[
            pl.BlockSpec((tm, d), lambda i, e: (i, 0)),
            const((1, d)),
            mod_spec(shift[:, None]), mod_spec(scale[:, None]), mod_spec(gate[:, None]),
            const(rw.shape), const(rb.shape), const((1, d)),
            pl.BlockSpec((1, epg, d, D_EXPERT), lambda i, e: (e, 0, 0, 0)),
            pl.BlockSpec((1, epg, d, D_EXPERT), lambda i, e: (e, 0, 0, 0)),
            pl.BlockSpec((1, epg, D_EXPERT, d), lambda i, e: (e, 0, 0, 0)),
        ],
        out_specs=pl.BlockSpec((tm, d), lambda i, e: (i, 0)),
        out_shape=jax.ShapeDtypeStruct((n, d), F32),
        scratch_shapes=[pltpu.VMEM((tm, d), BF16), pltpu.VMEM((tm, LANES), F32), pltpu.VMEM((8, tm), F32)],
        compiler_params=pltpu.CompilerParams(dimension_semantics=("parallel", "arbitrary"),
                                             vmem_limit_bytes=VMEM_LIMIT),
        name="moe",
    )(x.reshape(n, d), g[None], shift[:, None], scale[:, None], gate[:, None], rw, rb, final_g[None], w1, w3, w2)
    return out.reshape(bsz, t, d)


def _scan_masks(n, rev):
    row = lax.broadcasted_iota(jnp.int32, (n, n), 0)
    col = lax.broadcasted_iota(jnp.int32, (n, n), 1)
    d = (row - col) * jnp.where(rev, -1, 1)
    return d >= 0, d > 0


def _chunk_rows(s, nsub, size, rev):
    return pl.ds(pl.multiple_of(jnp.where(rev, (nsub - 1 - s) * size, s * size), size), size)


def _last_row(x, rev):
    n = x.shape[0]
    return jnp.where(rev, x[0:1], x[n - 1:n])


def _scan_specs(nseq, blk, n_ctx_blk, n_blk, widths_shared, widths_dir):
    def tblock(i, c):
        back = jnp.where(c < n_ctx_blk, n_ctx_blk - 1 - c, n_ctx_blk + n_blk - 1 - c)
        return jnp.where(i >= nseq, back, c)
    def shared_spec(w, col_blk=0):
        return pl.BlockSpec((1, blk, w), lambda i, c: (i % nseq, tblock(i, c), col_blk))
    shared = [shared_spec(*w) if isinstance(w, tuple) else shared_spec(w) for w in widths_shared]
    per_dir = [pl.BlockSpec((1, blk, w), lambda i, c: (i, tblock(i, c), 0)) for w in widths_dir]
    return shared, per_dir, tblock


def _rwkv_kernel(r_ref, k_ref, v_ref, kk_ref, ka_ref, lw_ref, as_ref, y_ref, st_ref, *, nsub, nseq):
    L = RW_CHUNK
    hd = RW_HD
    rev = pl.program_id(0) >= nseq

    @pl.when(pl.program_id(1) == 0)
    def _():
        st_ref[...] = jnp.zeros_like(st_ref)

    incl, strict = _scan_masks(L, rev)
    tri = jnp.where(incl, 1.0, 0.0).astype(BF16)
    zeros = jnp.zeros((L, hd), BF16)
    chains = [(s, h) for s in range(nsub) for h in range(RW_HEADS)]
    rows = [_chunk_rows(s, nsub, L, rev) for s in range(nsub)]

    prep = []
    for s in range(nsub):
        lw = lw_ref[0, rows[s], :]
        c = _cumsum_rows(tri, lw)
        c_end = _last_row(c, rev)
        e_inv = jnp.exp(-c)
        e_end = jnp.exp(c_end - c)
        a_sig = as_ref[0, rows[s], :]
        kk = kk_ref[0, rows[s], :]
        k = k_ref[0, rows[s], :]
        kmod = k * (1.0 + (a_sig - 1.0) * ka_ref[...])
        bv = kk * a_sig
        prep.append(dict(
            at=-kk * jnp.exp(c - lw), rt=r_ref[0, rows[s], :] * jnp.exp(c),
            bt=(bv * e_inv).astype(BF16), kt=(kmod * e_inv).astype(BF16),
            bh=(bv * e_end).astype(BF16), kh=(kmod * e_end).astype(BF16),
            p_end=jnp.exp(c_end), v=v_ref[0, rows[s], :].astype(BF16)))

    def head(name, s, h):
        return prep[s][name][:, h * hd:(h + 1) * hd]

    sc = {}
    for s, h in chains:
        lhs = jnp.concatenate([head("at", s, h), head("rt", s, h)], 0).astype(BF16)
        rhs = jnp.concatenate([head("bt", s, h), head("kt", s, h)], 0)
        sc[s, h] = _dot_nt(lhs, rhs)
    nmat, x, mr = {}, {}, {}
    for s, h in chains:
        m = sc[s, h]
        nmat[s, h] = jnp.where(strict, m[:L, :L], 0.0)
        mak = jnp.where(strict, m[:L, L:], 0.0).astype(BF16)
        mr[s, h] = jnp.concatenate([jnp.where(incl, m[L:, :L], 0.0), jnp.where(incl, m[L:, L:], 0.0)], 1).astype(BF16)
        x[s, h] = jnp.concatenate([head("at", s, h), _dot(mak, head("v", s, h))], 1)
    steps = L.bit_length() - 1
    for i in range(steps):
        for s, h in chains:
            nb = nmat[s, h].astype(BF16)
            xb = x[s, h].astype(BF16)
            if i + 1 < steps:
                prod = _dot(nb, jnp.concatenate([xb, nb], 1))
                x[s, h] = x[s, h] + prod[:, :2 * hd]
                nmat[s, h] = prod[:, 2 * hd:]
            else:
                x[s, h] = x[s, h] + _dot(nb, xb)
    ftop, bhw_t, gt = {}, {}, {}
    for s, h in chains:
        xb = x[s, h].astype(BF16)
        vb = head("v", s, h)
        z = jnp.concatenate([xb, jnp.concatenate([zeros, vb], 1)], 0)
        ftop[s, h] = _dot(mr[s, h], z)
        t1 = _dot_tn(xb, head("bh", s, h))
        bhw_t[s, h] = t1[:hd].astype(BF16)
        gt[s, h] = t1[hd:] + _dot_tn(vb, head("kh", s, h))
    st = [st_ref[h] for h in range(RW_HEADS)]
    for s in range(nsub):
        ys = []
        for h in range(RW_HEADS):
            stb = st[h].astype(BF16)
            q = (head("rt", s, h) + ftop[s, h][:, :hd]).astype(BF16)
            ys.append(_dot_nt(q, stb) + ftop[s, h][:, hd:])
            st[h] = st[h] * head("p_end", s, h) + _dot(stb, bhw_t[s, h]) + gt[s, h]
        y_ref[0, rows[s], :] = jnp.concatenate(ys, 1)
    for h in range(RW_HEADS):
        st_ref[h] = st[h]


def rwkv_scan(r, k, v, kk, k_a, lw, a_sig, *, t_ctx, nsub=4):
    nseq, t, w = r.shape
    blk = RW_CHUNK * nsub
    assert t % blk == 0 and t_ctx % blk == 0
    shared, per_dir, _ = _scan_specs(nseq, blk, t_ctx // blk, t // blk, [w] * 4, [w] * 3)
    return pl.pallas_call(
        functools.partial(_rwkv_kernel, nsub=nsub, nseq=nseq),
        grid=(2 * nseq, t // blk),
        in_specs=shared + [pl.BlockSpec((1, w), lambda i, c: (0, 0))] + per_dir[:2],
        out_specs=per_dir[2],
        out_shape=jax.ShapeDtypeStruct((2 * nseq, t, w), F32),
        scratch_shapes=[pltpu.VMEM((RW_HEADS, RW_HD, RW_HD), F32)],
        compiler_params=pltpu.CompilerParams(dimension_semantics=("parallel", "arbitrary"),
                                             vmem_limit_bytes=VMEM_LIMIT),
        name="rwkv_scan",
    )(r, k, v, kk, k_a[None], lw, a_sig)


def _expand_cols(x, e):
    return sum(_dot(p, e) for p in _split3(x))


def _ssd_kernel(x_ref, b_ref, c_ref, dd_ref, y_ref, st_ref, *, nsub, nseq):
    L = SSD_CHUNK
    nh = SSM_HEADS
    pw = 2 * SSM_HD
    npairs = nh // 2
    pairs_per_group = npairs // SSM_GROUPS
    rev = pl.program_id(0) >= nseq

    @pl.when(pl.program_id(1) == 0)
    def _():
        st_ref[...] = jnp.zeros_like(st_ref)

    incl, _ = _scan_masks(L, rev)
    tri = jnp.where(incl, 1.0, 0.0).astype(BF16)
    h_i = lax.broadcasted_iota(jnp.int32, (nh, nh * SSM_HD), 0)
    c_i = lax.broadcasted_iota(jnp.int32, (nh, nh * SSM_HD), 1)
    e_head = jnp.where(c_i // SSM_HD == h_i, 1.0, 0.0).astype(BF16)
    h_j = lax.broadcasted_iota(jnp.int32, (nh, nh * L), 0)
    c_j = lax.broadcasted_iota(jnp.int32, (nh, nh * L), 1)
    e_wide = jnp.where(c_j // L == h_j, 1.0, 0.0).astype(BF16)
    lane_p = lax.broadcasted_iota(jnp.int32, (L, pw), 1)
    first_half = lane_p < SSM_HD
    r_bd = lax.broadcasted_iota(jnp.int32, (pw, pw), 0)
    c_bd = lax.broadcasted_iota(jnp.int32, (pw, pw), 1)
    bd_mask = (r_bd < SSM_HD) == (c_bd < SSM_HD)

    rows = [_chunk_rows(s, nsub, L, rev) for s in range(nsub)]
    work = []
    for s in range(nsub):
        dd = dd_ref[0, jnp.where(rev, nsub - 1 - s, s)]
        da_parts = _split3(dd[:nh])
        acs_row = sum(_dot_nt(p, tri) for p in da_parts)
        acs_col = sum(_dot_nt(tri, p) for p in da_parts)
        colx = _expand_cols(acs_col, e_wide)
        dtx = sum(_dot_tn(p, e_head) for p in _split3(dd[nh:]))
        xdt = (x_ref[0, rows[s], :] * dtx).astype(BF16)
        bm = b_ref[0, rows[s], :]
        cm = c_ref[0, rows[s], :]
        bm_sw = pltpu.roll(bm, SSM_STATE, 1)
        cm_sw = pltpu.roll(cm, SSM_STATE, 1)
        b2 = [jnp.where(first_half, bm, bm_sw), jnp.where(first_half, bm_sw, bm)]
        c2 = [jnp.where(first_half, cm, cm_sw), jnp.where(first_half, cm_sw, cm)]
        cb = [_dot_nt(cm[:, g * SSM_STATE:(g + 1) * SSM_STATE].astype(BF16),
                      bm[:, g * SSM_STATE:(g + 1) * SSM_STATE].astype(BF16)) for g in range(SSM_GROUPS)]
        work.append(dict(acs_row=acs_row, colx=colx, xdt=xdt, b2=b2, c2=c2, cb=cb))

    y_diag, ce, new, ea = {}, {}, {}, {}
    for s in range(nsub):
        w = work[s]
        for p in range(npairs):
            g = p // pairs_per_group
            h0, h1 = 2 * p, 2 * p + 1
            cx0 = w["colx"][:, h0 * L:(h0 + 1) * L]
            cx1 = w["colx"][:, h1 * L:(h1 + 1) * L]
            s0 = w["cb"][g] * jnp.exp(jnp.where(incl, cx0 - w["acs_row"][h0:h0 + 1, :], -jnp.inf))
            s1 = w["cb"][g] * jnp.exp(jnp.where(incl, cx1 - w["acs_row"][h1:h1 + 1, :], -jnp.inf))
            xp = w["xdt"][:, p * pw:(p + 1) * pw]
            zero = jnp.zeros_like(xp)
            x_bd = jnp.concatenate([jnp.where(first_half, xp, zero), jnp.where(first_half, zero, xp)], 0)
            y_diag[s, p] = _dot(jnp.concatenate([s0, s1], 1).astype(BF16), x_bd)
            col = jnp.where(first_half, cx0[:, :pw], cx1[:, :pw])
            a_end = _last_row(col, rev)
            ce[s, p] = (w["c2"][g] * jnp.exp(col)).astype(BF16)
            be = (w["b2"][g] * jnp.exp(a_end - col)).astype(BF16)
            new[s, p] = jnp.where(bd_mask, _dot_tn(be, xp), 0.0)
            ea[s, p] = jnp.exp(a_end)
    st = [st_ref[p] for p in range(npairs)]
    for s in range(nsub):
        ys = []
        for p in range(npairs):
            ys.append(y_diag[s, p] + _dot(ce[s, p], st[p].astype(BF16)))
            st[p] = st[p] * ea[s, p] + new[s, p]
        y_ref[0, rows[s], :] = jnp.concatenate(ys, 1)
    for p in range(npairs):
        st_ref[p] = st[p]


def ssd_scan(conv, dd, *, t_ctx, nsub=2):
    nseq, t, _ = conv.shape
    w = SSM_W
    blk = SSD_CHUNK * nsub
    assert t % blk == 0 and t_ctx % blk == 0
    assert SSM_STATE == SSM_HD and SSD_CHUNK == 2 * SSM_HD
    gw = SSM_GROUPS * SSM_STATE
    shared, per_dir, tblock = _scan_specs(nseq, blk, t_ctx // blk, t // blk,
                                          [(w, 0), (gw, w // gw), (gw, w // gw + 1)], [w])
    dd_spec = pl.BlockSpec((1, nsub, 2 * SSM_HEADS, SSD_CHUNK), lambda i, c: (i, tblock(i, c), 0, 0))
    return pl.pallas_call(
        functools.partial(_ssd_kernel, nsub=nsub, nseq=nseq),
        grid=(2 * nseq, t // blk),
        in_specs=shared + [dd_spec],
        out_specs=per_dir[0],
        out_shape=jax.ShapeDtypeStruct((2 * nseq, t, w), F32),
        scratch_shapes=[pltpu.VMEM((SSM_HEADS // 2, 2 * SSM_STATE, 2 * SSM_HD), F32)],
        compiler_params=pltpu.CompilerParams(dimension_semantics=("parallel", "arbitrary"),
                                             vmem_limit_bytes=VMEM_LIMIT),
        name="ssd_scan",
    )(conv, conv, conv, dd)


def _gla_kernel(q_ref, k_ref, v_ref, sm_ref, ga_ref, gb_ref, y_ref, st_ref, lg_ref, *, nsub, nseq):
    L = GLA_CHUNK
    rev = pl.program_id(0) >= nseq

    @pl.when(pl.program_id(1) == 0)
    def _():
        st_ref[...] = jnp.zeros_like(st_ref)

    nh = GLA_HEADS
    incl, _ = _scan_masks(L, rev)
    tri = jnp.where(incl, 1.0, 0.0).astype(BF16)
    incl_h = jnp.concatenate([incl] * nh, 0)
    r_q = lax.broadcasted_iota(jnp.int32, (nh * L, GLA_KW), 0)
    c_q = lax.broadcasted_iota(jnp.int32, (nh * L, GLA_KW), 1)
    q_mask = r_q // L == c_q // GLA_DK
    r_s = lax.broadcasted_iota(jnp.int32, (GLA_VW, GLA_KW), 0)
    c_s = lax.broadcasted_iota(jnp.int32, (GLA_VW, GLA_KW), 1)
    st_mask = r_s // GLA_DV == c_s // GLA_DK
    lane_v = lax.broadcasted_iota(jnp.int32, (L, GLA_VW), 1) // GLA_DV
    rows = [_chunk_rows(s, nsub, L, rev) for s in range(nsub)]

    gate_logit = _dot3(sm_ref[0], ga_ref[0]) + gb_ref[0]
    lg_ref[...] = -_softplus(-gate_logit) * (1.0 / GLA_TAU)
    bcs = [_cumsum_rows(tri, lg_ref[rows[s], :]) for s in range(nsub)]
    vb, ke, q_st, qd, kd, eb = [], [], [], [], [], []
    for s in range(nsub):
        mid = jnp.where(rev, bcs[s][L - 1 - L // 2:L - L // 2], bcs[s][L // 2:L // 2 + 1])
        bend = _last_row(bcs[s], rev)
        q = q_ref[0, rows[s], :] * GLA_DK ** -0.5
        k = k_ref[0, rows[s], :]
        vb.append(v_ref[0, rows[s], :].astype(BF16))
        qe = (q * jnp.exp(bcs[s] - mid)).astype(BF16)
        ke.append((k * jnp.exp(mid - bcs[s])).astype(BF16))
        q_st.append(jnp.where(q_mask, jnp.concatenate([qe] * nh, 0), jnp.zeros((), BF16)))
        qd.append((q * jnp.exp(bcs[s])).astype(BF16))
        kd.append((k * jnp.exp(bend - bcs[s])).astype(BF16))
        eb.append(jnp.exp(bend))
    att = [jnp.where(incl_h, _dot_nt(q_st[s], ke[s]), 0.0).astype(BF16) for s in range(nsub)]
    full = [_dot(att[s], vb[s]) for s in range(nsub)]
    o_intra = [sum(jnp.where(lane_v == h, full[s][h * L:(h + 1) * L], 0.0) for h in range(nh)) for s in range(nsub)]
    kv_t = [jnp.where(st_mask, _dot_tn(vb[s], kd[s]), 0.0) for s in range(nsub)]
    st = st_ref[...]
    for s in range(nsub):
        y_ref[0, rows[s], :] = o_intra[s] + _dot_nt(qd[s], st.astype(BF16))
        st = st * eb[s] + kv_t[s]
    st_ref[...] = st


def gla_scan(slab, small, ga2, gb, *, t_ctx, nsub=4):
    nseq, t, _ = slab.shape
    kw, vw = GLA_KW, GLA_VW
    blk = GLA_CHUNK * nsub
    assert t % blk == 0 and t_ctx % blk == 0
    shared, per_dir, _ = _scan_specs(nseq, blk, t_ctx // blk, t // blk,
                                     [(kw, 0), (kw, 1), (vw, 2 * kw // vw), LANES], [vw])
    rank = ga2.shape[1]
    ga_pad = jnp.zeros((2, LANES, kw), F32).at[:, SSM_HEADS:SSM_HEADS + rank].set(ga2)
    return pl.pallas_call(
        functools.partial(_gla_kernel, nsub=nsub, nseq=nseq),
        grid=(2 * nseq, t // blk),
        in_specs=shared + [pl.BlockSpec((1, LANES, kw), lambda i, c: (i // nseq, 0, 0)),
                           pl.BlockSpec((1, 1, kw), lambda i, c: (i // nseq, 0, 0))],
        out_specs=per_dir[0],
        out_shape=jax.ShapeDtypeStruct((2 * nseq, t, vw), F32),
        scratch_shapes=[pltpu.VMEM((GLA_VW, GLA_KW), F32), pltpu.VMEM((blk, kw), F32)],
        compiler_params=pltpu.CompilerParams(dimension_semantics=("parallel", "arbitrary"),
                                             vmem_limit_bytes=VMEM_LIMIT),
        name="gla_scan",
    )(slab, slab, slab, small, ga_pad, gb[:, None])


_HP = lax.Precision.HIGHEST


_IN_WIDTHS = (RW_COLS, SSM_W, SSM_CONV_CH, 2 * GLA_KW + 2 * GLA_VW, SMALL_W)


def _arrange_w_in(w_in):
    rw, ssm, gla = jnp.split(w_in, [RW_COLS, RW_COLS + SSM_COLS], -1)
    z, xbc, dt = jnp.split(ssm, [SSM_W, SSM_W + SSM_CONV_CH], -1)
    qkv, gl, r = jnp.split(gla, [2 * GLA_KW + GLA_VW, 2 * GLA_KW + GLA_VW + GLA_GATE_RANK], -1)
    pad = jnp.zeros((w_in.shape[0], SMALL_W - SSM_HEADS - GLA_GATE_RANK), w_in.dtype)
    return jnp.concatenate([rw, z, xbc, qkv, r, dt, gl, pad], -1).astype(BF16)


def _pad_lanes(w, reps=1):
    w = jnp.repeat(w, reps, axis=-1) if reps > 1 else w
    return jnp.pad(w, [(0, 0)] * (w.ndim - 1) + [(0, LANES - w.shape[-1])])


def kernel(x, c, ctx, c_ctx, ada_w, ada_b, norm1_g, norm2_g, w_in, w_out, rw_mu_prev, rw_mu_next, rw_w0, rw_w2, rw_a0, rw_a2, rw_g2, rw_k_k, rw_k_a, rw_r_k, rw_gn_g, rw_gn_b, ssm_conv_w, ssm_conv_b, ssm_dt_bias, ssm_a_log, ssm_d, ssm_norm_g, gla_ga2, gla_gb, gla_norm_g, moe_rg_w, moe_rg_b, moe_re_w, moe_re_b, moe_w1, moe_w3, moe_w2, final_g):
    depth = ada_w.shape[0]
    bsz, t_lat, _ = x.shape
    t_ctx = ctx.shape[1]
    assert t_lat % GRID_W == 0
    cond_l = jax.nn.silu(c)
    cond_c = jax.nn.silu(c_ctx)[None]
    tm = PROJ_TM
    for l in range(depth):
        ctx_out = l < depth - 1
        mod_l = jnp.split(jnp.dot(cond_l, ada_w[l], precision=_HP) + ada_b[l], 6, -1)
        mod_c = jnp.split(jnp.dot(cond_c, ada_w[l], precision=_HP) + ada_b[l], 6, -1)
        w_in_l = _arrange_w_in(w_in[l])
        u_rw, z, xbc, u_gla, small = in_proj(ctx, x, norm1_g[l], mod_c[0], mod_c[1], mod_l[0], mod_l[1], w_in_l,
                                             _IN_WIDTHS, tm=tm)
        r, k, v, kk, bonus, rgate, lw, a_sig = rwkv_prep(u_rw, rw_mu_prev[l], rw_mu_next[l], rw_w0[l], rw_w2[l],
                                                         rw_a0[l], rw_a2[l], rw_g2[l], rw_k_k[l], rw_r_k[l],
                                                         t_ctx=t_ctx, tm=tm)
        merge = lambda a: a.reshape((2 * bsz,) + a.shape[2:])
        split = lambda a: a.reshape((2, bsz) + a.shape[1:])
        y_rw = split(rwkv_scan(r, k, v, kk, rw_k_a[l], merge(lw), merge(a_sig), t_ctx=t_ctx))
        conv, dd = ssm_prep(xbc, small, ssm_conv_w[l], ssm_conv_b[l], ssm_dt_bias[l], ssm_a_log[l], t_ctx=t_ctx, tm=tm)
        y_ssd = split(ssd_scan(conv, merge(dd), t_ctx=t_ctx))
        y_gla = split(gla_scan(u_gla, small, gla_ga2[l], gla_gb[l], t_ctx=t_ctx))
        mixed = (y_rw, bonus, rgate, y_ssd, conv, z, y_gla, u_gla,
                 rw_gn_g[l], rw_gn_b[l], ssm_d[l], ssm_norm_g[l], gla_norm_g[l])
        w_out_l = w_out[l].astype(BF16)
        epg = EXPERTS_PER_GROUP
        rw = _pad_lanes(jnp.concatenate([jnp.repeat(moe_rg_w[l], epg, -1), moe_re_w[l]], -1))
        rb = _pad_lanes(jnp.concatenate([jnp.repeat(moe_rg_b[l], epg, -1), moe_re_b[l]], -1)[None])
        by_group = lambda w: w.astype(BF16).reshape((N_GROUPS, epg) + w.shape[1:])
        w1, w3, w2 = by_group(moe_w1[l]), by_group(moe_w3[l]), by_group(moe_w2[l])
        x = out_proj(*mixed, w_out_l, x, mod_l[2], tm=tm, t_off=t_ctx)
        x = moe_block(x, norm2_g[l], mod_l[3], mod_l[4], mod_l[5], rw, rb, final_g, w1, w3, w2, tm=MOE_TM,
                      final_norm=not ctx_out)
        if ctx_out:
            ctx = out_proj(*mixed, w_out_l, ctx, mod_c[2], tm=tm, t_off=0)
            ctx = moe_block(ctx, norm2_g[l], mod_c[3], mod_c[4], mod_c[5], rw, rb, final_g, w1, w3, w2, tm=MOE_TM,
                            final_norm=False)
    return x
```

```python
import functools

import jax
import jax.numpy as jnp
from jax import lax
from jax.experimental import pallas as pl
from jax.experimental.pallas import tpu as pltpu

F32 = jnp.float32
BF16 = jnp.bfloat16

D_MODEL = 1024
GRID_W = 64
NORM_EPS = 1e-6

RW_HEADS = 4
RW_HD = 64
RW_W = RW_HEADS * RW_HD
RW_DECAY_RANK = 64
RW_ICLR_RANK = 64
RW_GATE_RANK = 128
RW_GN_EPS = 64e-5
RW_COLS = 3 * RW_W + RW_DECAY_RANK + RW_ICLR_RANK + RW_GATE_RANK

SSM_HEADS = 8
SSM_HD = 64
SSM_W = SSM_HEADS * SSM_HD
SSM_GROUPS = 2
SSM_STATE = 64
SSM_CONV_CH = SSM_W + 2 * SSM_GROUPS * SSM_STATE
SSM_COLS = SSM_W + SSM_CONV_CH + SSM_HEADS

GLA_HEADS = 4
GLA_DK = 32
GLA_DV = 64
GLA_KW = GLA_HEADS * GLA_DK
GLA_VW = GLA_HEADS * GLA_DV
GLA_GATE_RANK = 16
GLA_TAU = 16.0
GLA_COLS = 2 * GLA_KW + GLA_VW + GLA_GATE_RANK + GLA_VW

N_GROUPS = 4
EXPERTS_PER_GROUP = 4
N_EXPERTS = N_GROUPS * EXPERTS_PER_GROUP
D_EXPERT = 512

LANES = 128
SMALL_W = LANES
VMEM_LIMIT = 56 * 1024 * 1024

PROJ_TM = 256
MOE_TM = 1024
MOE_SUB = 128
MOE_SCATTER_COLS = 256
CONV_ROWS = 64

RW_CHUNK = 64
SSD_CHUNK = 128
GLA_CHUNK = 64

_NT = (((1,), (1,)), ((), ()))
_TN = (((0,), (0,)), ((), ()))


def _dot(a, b):
    return jnp.dot(a, b, preferred_element_type=F32)


def _dot_nt(a, b):
    return lax.dot_general(a, b, _NT, preferred_element_type=F32)


def _dot_tn(a, b):
    return lax.dot_general(a, b, _TN, preferred_element_type=F32)


def _split2(x):
    hi = x.astype(BF16)
    mid = (x - hi.astype(F32)).astype(BF16)
    return hi, mid


def _cumsum_rows(tri, x):
    hi, mid = _split2(x)
    return _dot(tri, hi) + _dot(tri, mid)


def _norm_mod(x, g, shift, scale):
    h = x * lax.rsqrt(jnp.mean(x * x, -1, keepdims=True) + NORM_EPS) * g
    return h * (1.0 + scale) + shift


def _in_proj_kernel(ctx_ref, x_ref, g_ref, sh_ref, sc_ref, w_ref, *out_refs, widths, ctx_blk):
    xin = jnp.where(pl.program_id(1) < ctx_blk, ctx_ref[0], x_ref[0])
    h = _norm_mod(xin, g_ref[...], sh_ref[0], sc_ref[0]).astype(BF16)
    off = 0
    for o_ref, wd in zip(out_refs, widths):
        o_ref[0] = _dot(h, w_ref[:, off:off + wd])
        off += wd


def _mod_spec(mod, tiles_per_batch):
    d = mod.shape[-1]
    if mod.shape[0] == 1:
        return pl.BlockSpec((1, 1, d), lambda i: (0, 0, 0))
    return pl.BlockSpec((1, 1, d), lambda i: (i // tiles_per_batch, 0, 0))


def in_proj(ctx, x, g, shift_c, scale_c, shift_l, scale_l, w, widths, *, tm):
    bsz, t, d = x.shape
    t_ctx = ctx.shape[1]
    assert t % tm == 0 and t_ctx % tm == 0
    ctx_blk = t_ctx // tm
    n_blk = ctx_blk + t // tm
    shift = jnp.concatenate([shift_l, shift_c], 0)[:, None]
    scale = jnp.concatenate([scale_l, scale_c], 0)[:, None]
    mod_spec = pl.BlockSpec((1, 1, d), lambda b, j: (jnp.where(j < ctx_blk, bsz, b), 0, 0))
    return pl.pallas_call(
        functools.partial(_in_proj_kernel, widths=widths, ctx_blk=ctx_blk),
        grid=(bsz, n_blk),
        in_specs=[
            pl.BlockSpec((1, tm, d), lambda b, j: (b, jnp.minimum(j, ctx_blk - 1), 0)),
            pl.BlockSpec((1, tm, d), lambda b, j: (b, jnp.maximum(j - ctx_blk, 0), 0)),
            pl.BlockSpec((1, d), lambda b, j: (0, 0)),
            mod_spec, mod_spec,
            pl.BlockSpec(w.shape, lambda b, j: (0, 0)),
        ],
        out_specs=[pl.BlockSpec((1, tm, wd), lambda b, j: (b, j, 0)) for wd in widths],
        out_shape=[jax.ShapeDtypeStruct((bsz, t_ctx + t, wd), F32) for wd in widths],
        compiler_params=pltpu.CompilerParams(dimension_semantics=("parallel", "parallel"),
                                             vmem_limit_bytes=VMEM_LIMIT),
        name="in_proj",
    )(ctx, x, g[None], shift, scale, w)


def _dot3(a, b):
    a_hi, a_mid = _split2(a)
    b_hi, b_mid = _split2(b)
    return _dot(a_hi, b_hi) + _dot(a_hi, b_mid) + _dot(a_mid, b_hi)


def _head_ones(width, head):
    r = lax.broadcasted_iota(jnp.int32, (width, width), 0)
    c = lax.broadcasted_iota(jnp.int32, (width, width), 1)
    return jnp.where(r // head == c // head, 1.0, 0.0).astype(BF16)


def _head_sum(x, ones):
    return sum(_dot(p, ones) for p in _split2(x))


def _silu(x):
    return x * jax.nn.sigmoid(x)


def _out_proj_kernel(yr_ref, bonus_ref, rgate_ref, ys_ref, xs_ref, z_ref, yg_ref, gr_ref,
                     gng_ref, gnb_ref, dsk_ref, sng_ref, gng2_ref, w_ref, x_ref, gate_ref, o_ref):
    ones_rw = _head_ones(RW_W, RW_HD)
    yr = yr_ref[0, 0] + yr_ref[1, 0]
    mu = _head_sum(yr, ones_rw) * (1.0 / RW_HD)
    yc = yr - mu
    var = _head_sum(yc * yc, ones_rw) * (1.0 / RW_HD)
    a_out = (yc * lax.rsqrt(var + RW_GN_EPS) * gng_ref[...] + gnb_ref[...] + bonus_ref[0]) * rgate_ref[0]
    ysd = ys_ref[0, 0] + ys_ref[1, 0] + dsk_ref[...] * xs_ref[0]
    t = ysd * _silu(z_ref[0])
    b_out = t * lax.rsqrt(jnp.mean(t * t, -1, keepdims=True) + NORM_EPS) * sng_ref[...]
    yg = yg_ref[0, 0] + yg_ref[1, 0]
    ms = _head_sum(yg * yg, _head_ones(GLA_VW, GLA_DV)) * (1.0 / GLA_DV)
    g_out = yg * lax.rsqrt(ms + NORM_EPS) * gng2_ref[...] * _silu(gr_ref[0])
    m = jnp.concatenate([a_out, b_out, g_out], 1).astype(BF16)
    o_ref[...] = x_ref[...] + gate_ref[0] * _dot(m, w_ref[...])


def out_proj(y_rw, bonus, rgate, y_ssd, conv, z, y_gla, gla_slab, gn_g, gn_b, d_skip, ssm_norm_g, gla_norm_g,
             w, x, gate, *, tm, t_off):
    bsz, t, d = x.shape
    n = bsz * t
    assert t % tm == 0 and t_off % tm == 0
    tpb = t // tm
    ob = t_off // tm
    row = lambda i: (i // tpb, ob + i % tpb)
    both = lambda wd: pl.BlockSpec((2, 1, tm, wd), lambda i: (0,) + row(i) + (0,))
    one = lambda wd, cb=0: pl.BlockSpec((1, tm, wd), lambda i: row(i) + (cb,))
    par = lambda wd: pl.BlockSpec((1, wd), lambda i: (0, 0))
    r_blk = (2 * GLA_KW + GLA_VW) // GLA_VW
    out = pl.pallas_call(
        _out_proj_kernel,
        grid=(n // tm,),
        in_specs=[
            both(RW_W), one(RW_W), one(RW_W),
            both(SSM_W), one(SSM_W), one(SSM_W),
            both(GLA_VW), one(GLA_VW, r_blk),
            par(RW_W), par(RW_W), par(SSM_W), par(SSM_W), par(GLA_VW),
            pl.BlockSpec(w.shape, lambda i: (0, 0)),
            pl.BlockSpec((tm, d), lambda i: (i, 0)),
            _mod_spec(gate[:, None], tpb),
        ],
        out_specs=pl.BlockSpec((tm, d), lambda i: (i, 0)),
        out_shape=jax.ShapeDtypeStruct((n, d), F32),
        compiler_params=pltpu.CompilerParams(dimension_semantics=("parallel",), vmem_limit_bytes=VMEM_LIMIT),
        name="out_proj",
    )(y_rw, bonus, rgate, y_ssd, conv, z, y_gla, gla_slab,
      gn_g[None], gn_b[None], jnp.repeat(d_skip, SSM_HD)[None], ssm_norm_g[None], gla_norm_g[None],
      w, x.reshape(n, d), gate[:, None])
    return out.reshape(bsz, t, d)


def _softplus(x):
    return jnp.maximum(x, 0.0) + jnp.log1p(jnp.exp(-jnp.abs(x)))


def _rwkv_prep_kernel(u_ref, up_ref, un_ref, mup_ref, mun_ref, w2_ref, a2_ref, w0_ref, a0_ref, g2_ref, kk_ref, rk_ref,
                      r_o, k_o, v_o, kn_o, bonus_o, gate_o, lw_o, as_o, *, ctx_blk, n_blk):
    j = pl.program_id(1)
    u = u_ref[0]
    tm = u.shape[0]
    seg_start = jnp.logical_or(j == 0, j == ctx_blk)
    seg_end = jnp.logical_or(j == ctx_blk - 1, j == n_blk - 1)
    prev_row = jnp.where(seg_start, 0.0, up_ref[0, 7:8, :])
    next_row = jnp.where(seg_end, 0.0, un_ref[0, 0:1, :])
    row = lax.broadcasted_iota(jnp.int32, u.shape, 0)
    prev = jnp.where(row == 0, prev_row, pltpu.roll(u, 1, 0))
    nxt = jnp.where(row == tm - 1, next_row, pltpu.roll(u, tm - 1, 0))
    s = u + mup_ref[...] * (prev - u) + mun_ref[...] * (nxt - u)
    r, k, v = s[:, :RW_W], s[:, RW_W:2 * RW_W], s[:, 2 * RW_W:3 * RW_W]
    lora_in = s[:, 3 * RW_W:3 * RW_W + RW_DECAY_RANK + RW_ICLR_RANK]
    gl = s[:, 3 * RW_W + RW_DECAY_RANK + RW_ICLR_RANK:]
    ones = _head_ones(RW_W, RW_HD)
    kr = k * kk_ref[...]
    r_o[0] = r
    k_o[0] = k
    v_o[0] = v
    kn_o[0] = kr * lax.rsqrt(_head_sum(kr * kr, ones) + 1e-12)
    bonus_o[0] = _head_sum(r * k * rk_ref[...], ones) * v
    gate_o[0] = _dot3(jax.nn.sigmoid(gl), g2_ref[...])
    lora_t = jnp.tanh(lora_in)
    for d in range(2):
        w_log = -_softplus(-(w0_ref[d] + _dot3(lora_t, w2_ref[d]))) - 0.5
        lw_o[d, 0] = -jnp.exp(w_log)
        as_o[d, 0] = jax.nn.sigmoid(a0_ref[d] + _dot3(lora_in, a2_ref[d]))


def rwkv_prep(u, mu_prev, mu_next, w0, w2, a0, a2, g2, k_k, r_k, *, t_ctx, tm):
    bsz, t_all, wc = u.shape
    assert t_all % tm == 0 and t_ctx % tm == 0 and tm % 8 == 0
    n_blk = t_all // tm
    h8 = tm // 8
    last8 = t_all // 8 - 1
    zpad = jnp.zeros((2, RW_DECAY_RANK, RW_W), F32)
    w2p = jnp.concatenate([w2, zpad], 1)
    a2p = jnp.concatenate([zpad, a2], 1)
    par = lambda a: pl.BlockSpec(a.shape, lambda b, j: (0,) * a.ndim)
    params = [mu_prev[None], mu_next[None], w2p, a2p, w0[:, None], a0[:, None], g2, k_k[None], r_k[None]]
    o1 = pl.BlockSpec((1, tm, RW_W), lambda b, j: (b, j, 0))
    o2 = pl.BlockSpec((2, 1, tm, RW_W), lambda b, j: (0, b, j, 0))
    s1 = jax.ShapeDtypeStruct((bsz, t_all, RW_W), F32)
    s2 = jax.ShapeDtypeStruct((2, bsz, t_all, RW_W), F32)
    return pl.pallas_call(
        functools.partial(_rwkv_prep_kernel, ctx_blk=t_ctx // tm, n_blk=n_blk),
        grid=(bsz, n_blk),
        in_specs=[
            pl.BlockSpec((1, tm, wc), lambda b, j: (b, j, 0)),
            pl.BlockSpec((1, 8, wc), lambda b, j: (b, jnp.maximum(j * h8 - 1, 0), 0)),
            pl.BlockSpec((1, 8, wc), lambda b, j: (b, jnp.minimum((j + 1) * h8, last8), 0)),
        ] + [par(a) for a in params],
        out_specs=[o1] * 6 + [o2] * 2,
        out_shape=[s1] * 6 + [s2] * 2,
        compiler_params=pltpu.CompilerParams(dimension_semantics=("parallel", "parallel"),
                                             vmem_limit_bytes=VMEM_LIMIT),
        name="rwkv_prep",
    )(u, u, u, *params)


def _ssm_prep_kernel(x_ref, xp_ref, xn_ref, sm_ref, cw_ref, cb_ref, dtb_ref, alog_ref, o_ref, dd_ref, src_ref,
                     *, t_ctx, t_all, halo):
    j = pl.program_id(1)
    tm = x_ref.shape[1]
    t0 = j * tm
    is_ctx = t0 < t_ctx
    seg_lo = jnp.where(is_ctx, 0, t_ctx)
    seg_hi = jnp.where(is_ctx, t_ctx, t_all)
    ctx_i = jnp.where(is_ctx, 1, 0)
    lat_f = jnp.where(is_ctx, 0.0, 1.0)
    ch = x_ref.shape[2]
    reps = ch // LANES
    for piece, start in ((xp_ref, 0), (x_ref, halo), (xn_ref, halo + tm)):
        n = piece.shape[1]
        e_row = lax.broadcasted_iota(jnp.int32, (n, LANES), 0) + start
        t_src = e_row + (t0 - halo)
        in_seg = jnp.logical_and(t_src >= seg_lo, t_src < seg_hi)
        col = e_row % GRID_W
        keep = {0: in_seg,
                -1: jnp.logical_and(in_seg, (jnp.where(col != GRID_W - 1, 1, 0) | ctx_i) > 0),
                1: jnp.logical_and(in_seg, (jnp.where(col != 0, 1, 0) | ctx_i) > 0)}
        val = piece[0]
        for dc in (-1, 0, 1):
            src_ref[dc + 1, start:start + n, :] = jnp.where(jnp.concatenate([keep[dc]] * reps, 1), val, 0.0)
    taps = [(dr, dc) for dr in (-1, 0, 1) for dc in (-1, 0, 1)]
    for c0 in range(0, ch, LANES):
        cs = slice(c0, c0 + LANES)
        wts = []
        for dr, dc in taps:
            wt = cw_ref[(dr + 1) * 3 + dc + 1:(dr + 1) * 3 + dc + 2, cs]
            wts.append(wt * lat_f if dr != 0 else wt)
        for r0 in range(0, tm, CONV_ROWS):
            acc = jnp.zeros((CONV_ROWS, LANES), F32) + cb_ref[:, cs]
            for (dr, dc), wt in zip(taps, wts):
                off = halo + GRID_W * dr + dc + r0
                acc = acc + src_ref[dc + 1, off:off + CONV_ROWS, cs] * wt
            o_ref[0, r0:r0 + CONV_ROWS, cs] = _silu(acc)
    dt = sm_ref[0].T[:SSM_HEADS]
    for d in range(2):
        dtp = _softplus(dt + dtb_ref[d])
        da = -jnp.exp(alog_ref[d]) * dtp
        for c in range(tm // SSD_CHUNK):
            cs = slice(c * SSD_CHUNK, (c + 1) * SSD_CHUNK)
            dd_ref[d, 0, c, :SSM_HEADS, :] = da[:, cs]
            dd_ref[d, 0, c, SSM_HEADS:, :] = dtp[:, cs]


def ssm_prep(xbc, small, conv_w, conv_b, dt_bias, a_log, *, t_ctx, tm):
    bsz, t_all, ch = xbc.shape
    halo = 128
    assert t_all % tm == 0 and t_ctx % tm == 0 and tm % halo == 0 and halo > GRID_W and tm % SSD_CHUNK == 0
    n_blk = t_all // tm
    hb = tm // halo
    last = t_all // halo - 1
    par = lambda a: pl.BlockSpec(a.shape, lambda b, j: (0,) * a.ndim)
    params = [conv_w.reshape(9, ch), conv_b[None], dt_bias[:, :, None], a_log[:, :, None]]
    nc = tm // SSD_CHUNK
    return pl.pallas_call(
        functools.partial(_ssm_prep_kernel, t_ctx=t_ctx, t_all=t_all, halo=halo),
        grid=(bsz, n_blk),
        in_specs=[
            pl.BlockSpec((1, tm, ch), lambda b, j: (b, j, 0)),
            pl.BlockSpec((1, halo, ch), lambda b, j: (b, jnp.maximum(j * hb - 1, 0), 0)),
            pl.BlockSpec((1, halo, ch), lambda b, j: (b, jnp.minimum((j + 1) * hb, last), 0)),
            pl.BlockSpec((1, tm, LANES), lambda b, j: (b, j, 0)),
        ] + [par(a) for a in params],
        out_specs=[pl.BlockSpec((1, tm, ch), lambda b, j: (b, j, 0)),
                   pl.BlockSpec((2, 1, nc, 2 * SSM_HEADS, SSD_CHUNK), lambda b, j: (0, b, j, 0, 0))],
        out_shape=[jax.ShapeDtypeStruct((bsz, t_all, ch), F32),
                   jax.ShapeDtypeStruct((2, bsz, t_all // SSD_CHUNK, 2 * SSM_HEADS, SSD_CHUNK), F32)],
        scratch_shapes=[pltpu.VMEM((3, tm + 2 * halo, ch), F32)],
        compiler_params=pltpu.CompilerParams(dimension_semantics=("parallel", "parallel"),
                                             vmem_limit_bytes=VMEM_LIMIT),
        name="ssm_prep",
    )(xbc, xbc, xbc, small, *params)


def _route(logits):
    lane = lax.broadcasted_iota(jnp.int32, logits.shape, 1)
    valid = lane < N_EXPERTS
    neg = -jnp.inf
    big = jnp.int32(1 << 20)
    el = pltpu.roll(logits, LANES - N_EXPERTS, 1)
    glm = jnp.where(valid, logits, neg)
    gmax = jnp.max(glm, -1, keepdims=True)
    g_sel_lane = jnp.min(jnp.where(glm == gmax, lane, big), -1, keepdims=True)
    g_sel = g_sel_lane // EXPERTS_PER_GROUP
    in_group = (lane // EXPERTS_PER_GROUP) == g_sel
    gsum = jnp.sum(jnp.where(valid, jnp.exp(glm - gmax), 0.0), -1, keepdims=True) / EXPERTS_PER_GROUP
    p_group = 1.0 / gsum
    elm = jnp.where(in_group & valid, el, neg)
    m1 = jnp.max(elm, -1, keepdims=True)
    i1 = jnp.min(jnp.where(elm == m1, lane, big), -1, keepdims=True)
    elm2 = jnp.where(lane == i1, neg, elm)
    m2 = jnp.max(elm2, -1, keepdims=True)
    i2 = jnp.min(jnp.where(elm2 == m2, lane, big), -1, keepdims=True)
    p2 = jnp.exp(m2 - m1)
    wa = p_group / (1.0 + p2)
    wb = p_group * p2 / (1.0 + p2)
    return jnp.where(lane == i1, wa, 0.0) + jnp.where(lane == i2, wb, 0.0), g_sel


def _moe_kernel(x_ref, g_ref, sh_ref, sc_ref, gate_ref, rw_ref, rb_ref, fg_ref, w1_ref, w3_ref, w2_ref,
                o_ref, h_ref, comb_ref, code_ref, *, final_norm):
    grp = pl.program_id(1)
    tm, d = x_ref.shape

    @pl.when(grp == 0)
    def _():
        h = _norm_mod(x_ref[...], g_ref[...], sh_ref[0], sc_ref[0])
        h_ref[...] = h.astype(BF16)
        h_hi, h_mid = _split2(h)
        w_hi, w_mid = _split2(rw_ref[...])
        logits = _dot(h_hi, w_hi) + _dot(h_hi, w_mid) + _dot(h_mid, w_hi) + rb_ref[...]
        comb, g_sel = _route(logits)
        comb_ref[...] = comb
        lane = lax.broadcasted_iota(jnp.int32, (tm, LANES), 1)
        member = jnp.where(lane == g_sel, 1.0, 0.0)
        r_t = lax.broadcasted_iota(jnp.int32, (tm, tm), 0)
        c_t = lax.broadcasted_iota(jnp.int32, (tm, tm), 1)
        before = jnp.where(c_t < r_t, 1.0, 0.0).astype(BF16)
        rank = _dot(before, member.astype(BF16))
        code = jnp.where(member > 0.0, rank, -1.0)
        code_ref[...] = code.T[:code_ref.shape[0]]
        o_ref[...] = jnp.zeros_like(o_ref)

    crow = code_ref[pl.ds(grp, 1), :]
    count = jnp.max(crow).astype(jnp.int32) + 1
    comb_parts = _split2(comb_ref[...])

    sub_i = lax.broadcasted_iota(jnp.int32, (MOE_SUB, tm), 0).astype(F32)
    lane_c = lax.broadcasted_iota(jnp.int32, (MOE_SUB, LANES), 1)

    def body(j, carry):
        first = (j * MOE_SUB).astype(F32)
        sel = jnp.where(crow - first == sub_i, 1.0, 0.0).astype(BF16)
        hg = _dot(sel, h_ref[...]).astype(BF16)
        cg = sum(_dot(sel, p) for p in comb_parts)
        ysum = jnp.zeros((MOE_SUB, d), F32)
        for e in range(EXPERTS_PER_GROUP):
            a = _dot(hg, w1_ref[0, e])
            b = _dot(hg, w3_ref[0, e])
            hid = (a * jax.nn.sigmoid(a) * b).astype(BF16)
            col = jnp.sum(jnp.where(lane_c == grp * EXPERTS_PER_GROUP + e, cg, 0.0), -1, keepdims=True)
            ysum = ysum + col * _dot(hid, w2_ref[0, e])
        yb = ysum.astype(BF16)
        for c0 in range(0, d, MOE_SCATTER_COLS):
            cs = slice(c0, c0 + MOE_SCATTER_COLS)
            o_ref[:, cs] += _dot_tn(sel, yb[:, cs])
        return carry

    lax.fori_loop(0, (count + MOE_SUB - 1) // MOE_SUB, body, 0)

    @pl.when(grp == N_GROUPS - 1)
    def _():
        out = x_ref[...] + gate_ref[0] * o_ref[...]
        if final_norm:
            out = out * lax.rsqrt(jnp.mean(out * out, -1, keepdims=True) + NORM_EPS) * fg_ref[...]
        o_ref[...] = out


def moe_block(x, g, shift, scale, gate, rw, rb, final_g, w1, w3, w2, *, tm, final_norm):
    bsz, t, d = x.shape
    n = bsz * t
    tm = min(tm, t)
    assert t % tm == 0 and tm % MOE_SUB == 0 and d % MOE_SCATTER_COLS == 0
    tpb = t // tm
    epg = EXPERTS_PER_GROUP

    def mod_spec(mod):
        if mod.shape[0] == 1:
            return pl.BlockSpec((1, 1, d), lambda i, e: (0, 0, 0))
        return pl.BlockSpec((1, 1, d), lambda i, e: (i // tpb, 0, 0))

    const = lambda shape: pl.BlockSpec(shape, lambda i, e: (0,) * len(shape))
    out = pl.pallas_call(
        functools.partial(_moe_kernel, final_norm=final_norm),
        grid=(n // tm, N_GROUPS),
        in_specs=[
            pl.BlockSpec((tm, d), lambda i, e: (i, 0)),
            const((1, d)),
            mod_spec(shift[:, None]), mod_spec(scale[:, None]), mod_spec(gate[:, None]),
            const(rw.shape), const(rb.shape), const((1, d)),
            pl.BlockSpec((1, epg, d, D_EXPERT), lambda i, e: (e, 0, 0, 0)),
            pl.BlockSpec((1, epg, d, D_EXPERT), lambda i, e: (e, 0, 0, 0)),
            pl.BlockSpec((1, epg, D_EXPERT, d), lambda i, e: (e, 0, 0, 0)),
        ],
        out_specs=pl.BlockSpec((tm, d), lambda i, e: (i, 0)),
        out_shape=jax.ShapeDtypeStruct((n, d), F32),
        scratch_shapes=[pltpu.VMEM((tm, d), BF16), pltpu.VMEM((tm, LANES), F32), pltpu.VMEM((8, tm), F32)],
        compiler_params=pltpu.CompilerParams(dimension_semantics=("parallel", "arbitrary"),
                                             vmem_limit_bytes=VMEM_LIMIT),
        name="moe",
    )(x.reshape(n, d), g[None], shift[:, None], scale[:, None], gate[:, None], rw, rb, final_g[None], w1, w3, w2)
    return out.reshape(bsz, t, d)


def _scan_masks(n, rev):
    row = lax.broadcasted_iota(jnp.int32, (n, n), 0)
    col = lax.broadcasted_iota(jnp.int32, (n, n), 1)
    d = (row - col) * jnp.where(rev, -1, 1)
    return d >= 0, d > 0


def _chunk_rows(s, nsub, size, rev):
    return pl.ds(pl.multiple_of(jnp.where(rev, (nsub - 1 - s) * size, s * size), size), size)


def _last_row(x, rev):
    n = x.shape[0]
    return jnp.where(rev, x[0:1], x[n - 1:n])


def _scan_specs(nseq, blk, n_ctx_blk, n_blk, widths_shared, widths_dir):
    def tblock(i, c):
        back = jnp.where(c < n_ctx_blk, n_ctx_blk - 1 - c, n_ctx_blk + n_blk - 1 - c)
        return jnp.where(i >= nseq, back, c)
    def shared_spec(w, col_blk=0):
        return pl.BlockSpec((1, blk, w), lambda i, c: (i % nseq, tblock(i, c), col_blk))
    shared = [shared_spec(*w) if isinstance(w, tuple) else shared_spec(w) for w in widths_shared]
    per_dir = [pl.BlockSpec((1, blk, w), lambda i, c: (i, tblock(i, c), 0)) for w in widths_dir]
    return shared, per_dir, tblock


def _rwkv_kernel(r_ref, k_ref, v_ref, kk_ref, ka_ref, lw_ref, as_ref, y_ref, st_ref, *, nsub, nseq):
    L = RW_CHUNK
    hd = RW_HD
    rev = pl.program_id(0) >= nseq

    @pl.when(pl.program_id(1) == 0)
    def _():
        st_ref[...] = jnp.zeros_like(st_ref)

    incl, strict = _scan_masks(L, rev)
    tri = jnp.where(incl, 1.0, 0.0).astype(BF16)
    zeros = jnp.zeros((L, hd), BF16)
    chains = [(s, h) for s in range(nsub) for h in range(RW_HEADS)]
    rows = [_chunk_rows(s, nsub, L, rev) for s in range(nsub)]

    prep = []
    for s in range(nsub):
        lw = lw_ref[0, rows[s], :]
        c = _cumsum_rows(tri, lw)
        c_end = _last_row(c, rev)
        e_inv = jnp.exp(-c)
        e_end = jnp.exp(c_end - c)
        a_sig = as_ref[0, rows[s], :]
        kk = kk_ref[0, rows[s], :]
        k = k_ref[0, rows[s], :]
        kmod = k * (1.0 + (a_sig - 1.0) * ka_ref[...])
        bv = kk * a_sig
        prep.append(dict(
            at=-kk * jnp.exp(c - lw), rt=r_ref[0, rows[s], :] * jnp.exp(c),
            bt=(bv * e_inv).astype(BF16), kt=(kmod * e_inv).astype(BF16),
            bh=(bv * e_end).astype(BF16), kh=(kmod * e_end).astype(BF16),
            p_end=jnp.exp(c_end), v=v_ref[0, rows[s], :].astype(BF16)))

    def head(name, s, h):
        return prep[s][name][:, h * hd:(h + 1) * hd]

    sc = {}
    for s, h in chains:
        lhs = jnp.concatenate([head("at", s, h), head("rt", s, h)], 0).astype(BF16)
        rhs = jnp.concatenate([head("bt", s, h), head("kt", s, h)], 0)
        sc[s, h] = _dot_nt(lhs, rhs)
    nmat, x, mr = {}, {}, {}
    for s, h in chains:
        m = sc[s, h]
        nmat[s, h] = jnp.where(strict, m[:L, :L], 0.0)
        mak = jnp.where(strict, m[:L, L:], 0.0).astype(BF16)
        mr[s, h] = jnp.concatenate([jnp.where(incl, m[L:, :L], 0.0), jnp.where(incl, m[L:, L:], 0.0)], 1).astype(BF16)
        x[s, h] = jnp.concatenate([head("at", s, h), _dot(mak, head("v", s, h))], 1)
    steps = L.bit_length() - 1
    for i in range(steps):
        for s, h in chains:
            nb = nmat[s, h].astype(BF16)
            xb = x[s, h].astype(BF16)
            if i + 1 < steps:
                prod = _dot(nb, jnp.concatenate([xb, nb], 1))
                x[s, h] = x[s, h] + prod[:, :2 * hd]
                nmat[s, h] = prod[:, 2 * hd:]
            else:
                x[s, h] = x[s, h] + _dot(nb, xb)
    ftop, bhw_t, gt = {}, {}, {}
    for s, h in chains:
        xb = x[s, h].astype(BF16)
        vb = head("v", s, h)
        z = jnp.concatenate([xb, jnp.concatenate([zeros, vb], 1)], 0)
        ftop[s, h] = _dot(mr[s, h], z)
        t1 = _dot_tn(xb, head("bh", s, h))
        bhw_t[s, h] = t1[:hd].astype(BF16)
        gt[s, h] = t1[hd:] + _dot_tn(vb, head("kh", s, h))
    st = [st_ref[h] for h in range(RW_HEADS)]
    for s in range(nsub):
        ys = []
        for h in range(RW_HEADS):
            stb = st[h].astype(BF16)
            q = (head("rt", s, h) + ftop[s, h][:, :hd]).astype(BF16)
            ys.append(_dot_nt(q, stb) + ftop[s, h][:, hd:])
            st[h] = st[h] * head("p_end", s, h) + _dot(stb, bhw_t[s, h]) + gt[s, h]
        y_ref[0, rows[s], :] = jnp.concatenate(ys, 1)
    for h in range(RW_HEADS):
        st_ref[h] = st[h]


def rwkv_scan(r, k, v, kk, k_a, lw, a_sig, *, t_ctx, nsub=4):
    nseq, t, w = r.shape
    blk = RW_CHUNK * nsub
    assert t % blk == 0 and t_ctx % blk == 0
    shared, per_dir, _ = _scan_specs(nseq, blk, t_ctx // blk, t // blk, [w] * 4, [w] * 3)
    return pl.pallas_call(
        functools.partial(_rwkv_kernel, nsub=nsub, nseq=nseq),
        grid=(2 * nseq, t // blk),
        in_specs=shared + [pl.BlockSpec((1, w), lambda i, c: (0, 0))] + per_dir[:2],
        out_specs=per_dir[2],
        out_shape=jax.ShapeDtypeStruct((2 * nseq, t, w), F32),
        scratch_shapes=[pltpu.VMEM((RW_HEADS, RW_HD, RW_HD), F32)],
        compiler_params=pltpu.CompilerParams(dimension_semantics=("parallel", "arbitrary"),
                                             vmem_limit_bytes=VMEM_LIMIT),
        name="rwkv_scan",
    )(r, k, v, kk, k_a[None], lw, a_sig)


def _expand_cols(x, e):
    return sum(_dot(p, e) for p in _split2(x))


def _ssd_kernel(x_ref, b_ref, c_ref, dd_ref, y_ref, st_ref, *, nsub, nseq):
    L = SSD_CHUNK
    nh = SSM_HEADS
    pw = 2 * SSM_HD
    npairs = nh // 2
    pairs_per_group = npairs // SSM_GROUPS
    rev = pl.program_id(0) >= nseq

    @pl.when(pl.program_id(1) == 0)
    def _():
        st_ref[...] = jnp.zeros_like(st_ref)

    incl, _ = _scan_masks(L, rev)
    tri = jnp.where(incl, 1.0, 0.0).astype(BF16)
    h_i = lax.broadcasted_iota(jnp.int32, (nh, nh * SSM_HD), 0)
    c_i = lax.broadcasted_iota(jnp.int32, (nh, nh * SSM_HD), 1)
    e_head = jnp.where(c_i // SSM_HD == h_i, 1.0, 0.0).astype(BF16)
    h_j = lax.broadcasted_iota(jnp.int32, (nh, nh * L), 0)
    c_j = lax.broadcasted_iota(jnp.int32, (nh, nh * L), 1)
    e_wide = jnp.where(c_j // L == h_j, 1.0, 0.0).astype(BF16)
    lane_p = lax.broadcasted_iota(jnp.int32, (L, pw), 1)
    first_half = lane_p < SSM_HD
    r_bd = lax.broadcasted_iota(jnp.int32, (pw, pw), 0)
    c_bd = lax.broadcasted_iota(jnp.int32, (pw, pw), 1)
    bd_mask = (r_bd < SSM_HD) == (c_bd < SSM_HD)

    rows = [_chunk_rows(s, nsub, L, rev) for s in range(nsub)]
    work = []
    for s in range(nsub):
        dd = dd_ref[0, jnp.where(rev, nsub - 1 - s, s)]
        da_parts = _split2(dd[:nh])
        acs_row = sum(_dot_nt(p, tri) for p in da_parts)
        acs_col = sum(_dot_nt(tri, p) for p in da_parts)
        colx = _expand_cols(acs_col, e_wide)
        dtx = sum(_dot_tn(p, e_head) for p in _split2(dd[nh:]))
        xdt = (x_ref[0, rows[s], :] * dtx).astype(BF16)
        bm = b_ref[0, rows[s], :]
        cm = c_ref[0, rows[s], :]
        bm_sw = pltpu.roll(bm, SSM_STATE, 1)
        cm_sw = pltpu.roll(cm, SSM_STATE, 1)
        b2 = [jnp.where(first_half, bm, bm_sw), jnp.where(first_half, bm_sw, bm)]
        c2 = [jnp.where(first_half, cm, cm_sw), jnp.where(first_half, cm_sw, cm)]
        cb = [_dot_nt(cm[:, g * SSM_STATE:(g + 1) * SSM_STATE].astype(BF16),
                      bm[:, g * SSM_STATE:(g + 1) * SSM_STATE].astype(BF16)) for g in range(SSM_GROUPS)]
        work.append(dict(acs_row=acs_row, colx=colx, xdt=xdt, b2=b2, c2=c2, cb=cb))

    y_diag, ce, new, ea = {}, {}, {}, {}
    for s in range(nsub):
        w = work[s]
        for p in range(npairs):
            g = p // pairs_per_group
            h0, h1 = 2 * p, 2 * p + 1
            cx0 = w["colx"][:, h0 * L:(h0 + 1) * L]
            cx1 = w["colx"][:, h1 * L:(h1 + 1) * L]
            s0 = w["cb"][g] * jnp.exp(jnp.where(incl, cx0 - w["acs_row"][h0:h0 + 1, :], -jnp.inf))
            s1 = w["cb"][g] * jnp.exp(jnp.where(incl, cx1 - w["acs_row"][h1:h1 + 1, :], -jnp.inf))
            xp = w["xdt"][:, p * pw:(p + 1) * pw]
            zero = jnp.zeros_like(xp)
            x_bd = jnp.concatenate([jnp.where(first_half, xp, zero), jnp.where(first_half, zero, xp)], 0)
            y_diag[s, p] = _dot(jnp.concatenate([s0, s1], 1).astype(BF16), x_bd)
            col = jnp.where(first_half, cx0[:, :pw], cx1[:, :pw])
            a_end = _last_row(col, rev)
            ce[s, p] = (w["c2"][g] * jnp.exp(col)).astype(BF16)
            be = (w["b2"][g] * jnp.exp(a_end - col)).astype(BF16)
            new[s, p] = jnp.where(bd_mask, _dot_tn(be, xp), 0.0)
            ea[s, p] = jnp.exp(a_end)
    st = [st_ref[p] for p in range(npairs)]
    for s in range(nsub):
        ys = []
        for p in range(npairs):
            ys.append(y_diag[s, p] + _dot(ce[s, p], st[p].astype(BF16)))
            st[p] = st[p] * ea[s, p] + new[s, p]
        y_ref[0, rows[s], :] = jnp.concatenate(ys, 1)
    for p in range(npairs):
        st_ref[p] = st[p]


def ssd_scan(conv, dd, *, t_ctx, nsub=2):
    nseq, t, _ = conv.shape
    w = SSM_W
    blk = SSD_CHUNK * nsub
    assert t % blk == 0 and t_ctx % blk == 0
    assert SSM_STATE == SSM_HD and SSD_CHUNK == 2 * SSM_HD
    gw = SSM_GROUPS * SSM_STATE
    shared, per_dir, tblock = _scan_specs(nseq, blk, t_ctx // blk, t // blk,
                                          [(w, 0), (gw, w // gw), (gw, w // gw + 1)], [w])
    dd_spec = pl.BlockSpec((1, nsub, 2 * SSM_HEADS, SSD_CHUNK), lambda i, c: (i, tblock(i, c), 0, 0))
    return pl.pallas_call(
        functools.partial(_ssd_kernel, nsub=nsub, nseq=nseq),
        grid=(2 * nseq, t // blk),
        in_specs=shared + [dd_spec],
        out_specs=per_dir[0],
        out_shape=jax.ShapeDtypeStruct((2 * nseq, t, w), F32),
        scratch_shapes=[pltpu.VMEM((SSM_HEADS // 2, 2 * SSM_STATE, 2 * SSM_HD), F32)],
        compiler_params=pltpu.CompilerParams(dimension_semantics=("parallel", "arbitrary"),
                                             vmem_limit_bytes=VMEM_LIMIT),
        name="ssd_scan",
    )(conv, conv, conv, dd)


def _gla_kernel(q_ref, k_ref, v_ref, sm_ref, ga_ref, gb_ref, y_ref, st_ref, lg_ref, *, nsub, nseq):
    L = GLA_CHUNK
    rev = pl.program_id(0) >= nseq

    @pl.when(pl.program_id(1) == 0)
    def _():
        st_ref[...] = jnp.zeros_like(st_ref)

    nh = GLA_HEADS
    incl, _ = _scan_masks(L, rev)
    tri = jnp.where(incl, 1.0, 0.0).astype(BF16)
    incl_h = jnp.concatenate([incl] * nh, 0)
    r_q = lax.broadcasted_iota(jnp.int32, (nh * L, GLA_KW), 0)
    c_q = lax.broadcasted_iota(jnp.int32, (nh * L, GLA_KW), 1)
    q_mask = r_q // L == c_q // GLA_DK
    r_s = lax.broadcasted_iota(jnp.int32, (GLA_VW, GLA_KW), 0)
    c_s = lax.broadcasted_iota(jnp.int32, (GLA_VW, GLA_KW), 1)
    st_mask = r_s // GLA_DV == c_s // GLA_DK
    lane_v = lax.broadcasted_iota(jnp.int32, (L, GLA_VW), 1) // GLA_DV
    rows = [_chunk_rows(s, nsub, L, rev) for s in range(nsub)]

    gate_logit = _dot3(sm_ref[0], ga_ref[0]) + gb_ref[0]
    lg_ref[...] = -_softplus(-gate_logit) * (1.0 / GLA_TAU)
    bcs = [_cumsum_rows(tri, lg_ref[rows[s], :]) for s in range(nsub)]
    vb, ke, q_st, qd, kd, eb = [], [], [], [], [], []
    for s in range(nsub):
        mid = jnp.where(rev, bcs[s][L - 1 - L // 2:L - L // 2], bcs[s][L // 2:L // 2 + 1])
        bend = _last_row(bcs[s], rev)
        q = q_ref[0, rows[s], :] * GLA_DK ** -0.5
        k = k_ref[0, rows[s], :]
        vb.append(v_ref[0, rows[s], :].astype(BF16))
        qe = (q * jnp.exp(bcs[s] - mid)).astype(BF16)
        ke.append((k * jnp.exp(mid - bcs[s])).astype(BF16))
        q_st.append(jnp.where(q_mask, jnp.concatenate([qe] * nh, 0), jnp.zeros((), BF16)))
        qd.append((q * jnp.exp(bcs[s])).astype(BF16))
        kd.append((k * jnp.exp(bend - bcs[s])).astype(BF16))
        eb.append(jnp.exp(bend))
    att = [jnp.where(incl_h, _dot_nt(q_st[s], ke[s]), 0.0).astype(BF16) for s in range(nsub)]
    full = [_dot(att[s], vb[s]) for s in range(nsub)]
    o_intra = [sum(jnp.where(lane_v == h, full[s][h * L:(h + 1) * L], 0.0) for h in range(nh)) for s in range(nsub)]
    kv_t = [jnp.where(st_mask, _dot_tn(vb[s], kd[s]), 0.0) for s in range(nsub)]
    st = st_ref[...]
    for s in range(nsub):
        y_ref[0, rows[s], :] = o_intra[s] + _dot_nt(qd[s], st.astype(BF16))
        st = st * eb[s] + kv_t[s]
    st_ref[...] = st


def gla_scan(slab, small, ga2, gb, *, t_ctx, nsub=4):
    nseq, t, _ = slab.shape
    kw, vw = GLA_KW, GLA_VW
    blk = GLA_CHUNK * nsub
    assert t % blk == 0 and t_ctx % blk == 0
    shared, per_dir, _ = _scan_specs(nseq, blk, t_ctx // blk, t // blk,
                                     [(kw, 0), (kw, 1), (vw, 2 * kw // vw), LANES], [vw])
    rank = ga2.shape[1]
    ga_pad = jnp.zeros((2, LANES, kw), F32).at[:, SSM_HEADS:SSM_HEADS + rank].set(ga2)
    return pl.pallas_call(
        functools.partial(_gla_kernel, nsub=nsub, nseq=nseq),
        grid=(2 * nseq, t // blk),
        in_specs=shared + [pl.BlockSpec((1, LANES, kw), lambda i, c: (i // nseq, 0, 0)),
                           pl.BlockSpec((1, 1, kw), lambda i, c: (i // nseq, 0, 0))],
        out_specs=per_dir[0],
        out_shape=jax.ShapeDtypeStruct((2 * nseq, t, vw), F32),
        scratch_shapes=[pltpu.VMEM((GLA_VW, GLA_KW), F32), pltpu.VMEM((blk, kw), F32)],
        compiler_params=pltpu.CompilerParams(dimension_semantics=("parallel", "arbitrary"),
                                             vmem_limit_bytes=VMEM_LIMIT),
        name="gla_scan",
    )(slab, slab, slab, small, ga_pad, gb[:, None])


_HP = lax.Precision.HIGHEST


_IN_WIDTHS = (RW_COLS, SSM_W, SSM_CONV_CH, 2 * GLA_KW + 2 * GLA_VW, SMALL_W)


def _arrange_w_in(w_in):
    rw, ssm, gla = jnp.split(w_in, [RW_COLS, RW_COLS + SSM_COLS], -1)
    z, xbc, dt = jnp.split(ssm, [SSM_W, SSM_W + SSM_CONV_CH], -1)
    qkv, gl, r = jnp.split(gla, [2 * GLA_KW + GLA_VW, 2 * GLA_KW + GLA_VW + GLA_GATE_RANK], -1)
    pad = jnp.zeros((w_in.shape[0], SMALL_W - SSM_HEADS - GLA_GATE_RANK), w_in.dtype)
    return jnp.concatenate([rw, z, xbc, qkv, r, dt, gl, pad], -1).astype(BF16)


def _pad_lanes(w, reps=1):
    w = jnp.repeat(w, reps, axis=-1) if reps > 1 else w
    return jnp.pad(w, [(0, 0)] * (w.ndim - 1) + [(0, LANES - w.shape[-1])])


def kernel(x, c, ctx, c_ctx, ada_w, ada_b, norm1_g, norm2_g, w_in, w_out, rw_mu_prev, rw_mu_next, rw_w0, rw_w2, rw_a0, rw_a2, rw_g2, rw_k_k, rw_k_a, rw_r_k, rw_gn_g, rw_gn_b, ssm_conv_w, ssm_conv_b, ssm_dt_bias, ssm_a_log, ssm_d, ssm_norm_g, gla_ga2, gla_gb, gla_norm_g, moe_rg_w, moe_rg_b, moe_re_w, moe_re_b, moe_w1, moe_w3, moe_w2, final_g):
    depth = ada_w.shape[0]
    bsz, t_lat, _ = x.shape
    t_ctx = ctx.shape[1]
    assert t_lat % GRID_W == 0
    cond_l = jax.nn.silu(c)
    cond_c = jax.nn.silu(c_ctx)[None]
    tm = PROJ_TM
    for l in range(depth):
        ctx_out = l < depth - 1
        mod_l = jnp.split(jnp.dot(cond_l, ada_w[l], precision=_HP) + ada_b[l], 6, -1)
        mod_c = jnp.split(jnp.dot(cond_c, ada_w[l], precision=_HP) + ada_b[l], 6, -1)
        w_in_l = _arrange_w_in(w_in[l])
        u_rw, z, xbc, u_gla, small = in_proj(ctx, x, norm1_g[l], mod_c[0], mod_c[1], mod_l[0], mod_l[1], w_in_l,
                                             _IN_WIDTHS, tm=tm)
        r, k, v, kk, bonus, rgate, lw, a_sig = rwkv_prep(u_rw, rw_mu_prev[l], rw_mu_next[l], rw_w0[l], rw_w2[l],
                                                         rw_a0[l], rw_a2[l], rw_g2[l], rw_k_k[l], rw_r_k[l],
                                                         t_ctx=t_ctx, tm=tm)
        merge = lambda a: a.reshape((2 * bsz,) + a.shape[2:])
        split = lambda a: a.reshape((2, bsz) + a.shape[1:])
        y_rw = split(rwkv_scan(r, k, v, kk, rw_k_a[l], merge(lw), merge(a_sig), t_ctx=t_ctx))
        conv, dd = ssm_prep(xbc, small, ssm_conv_w[l], ssm_conv_b[l], ssm_dt_bias[l], ssm_a_log[l], t_ctx=t_ctx, tm=tm)
        y_ssd = split(ssd_scan(conv, merge(dd), t_ctx=t_ctx))
        y_gla = split(gla_scan(u_gla, small, gla_ga2[l], gla_gb[l], t_ctx=t_ctx))
        mixed = (y_rw, bonus, rgate, y_ssd, conv, z, y_gla, u_gla,
                 rw_gn_g[l], rw_gn_b[l], ssm_d[l], ssm_norm_g[l], gla_norm_g[l])
        w_out_l = w_out[l].astype(BF16)
        epg = EXPERTS_PER_GROUP
        rw = _pad_lanes(jnp.concatenate([jnp.repeat(moe_rg_w[l], epg, -1), moe_re_w[l]], -1))
        rb = _pad_lanes(jnp.concatenate([jnp.repeat(moe_rg_b[l], epg, -1), moe_re_b[l]], -1)[None])
        by_group = lambda w: w.astype(BF16).reshape((N_GROUPS, epg) + w.shape[1:])
        w1, w3, w2 = by_group(moe_w1[l]), by_group(moe_w3[l]), by_group(moe_w2[l])
        x = out_proj(*mixed, w_out_l, x, mod_l[2], tm=tm, t_off=t_ctx)
        x = moe_block(x, norm2_g[l], mod_l[3], mod_l[4], mod_l[5], rw, rb, final_g, w1, w3, w2, tm=MOE_TM,
                      final_norm=not ctx_out)
        if ctx_out:
            ctx = out_proj(*mixed, w_out_l, ctx, mod_c[2], tm=tm, t_off=0)
            ctx = moe_block(ctx, norm2_g[l], mod_c[3], mod_c[4], mod_c[5], rw, rb, final_g, w1, w3, w2, tm=MOE_TM,
                            final_norm=False)
    return x
```

```python
import functools

import jax
import jax.numpy as jnp
from jax import lax
from jax.experimental import pallas as pl
from jax.experimental.pallas import tpu as pltpu

F32 = jnp.float32
BF16 = jnp.bfloat16

D_MODEL = 1024
GRID_W = 64
NORM_EPS = 1e-6

RW_HEADS = 4
RW_HD = 64
RW_W = RW_HEADS * RW_HD
RW_DECAY_RANK = 64
RW_ICLR_RANK = 64
RW_GATE_RANK = 128
RW_GN_EPS = 64e-5
RW_COLS = 3 * RW_W + RW_DECAY_RANK + RW_ICLR_RANK + RW_GATE_RANK

SSM_HEADS = 8
SSM_HD = 64
SSM_W = SSM_HEADS * SSM_HD
SSM_GROUPS = 2
SSM_STATE = 64
SSM_CONV_CH = SSM_W + 2 * SSM_GROUPS * SSM_STATE
SSM_COLS = SSM_W + SSM_CONV_CH + SSM_HEADS

GLA_HEADS = 4
GLA_DK = 32
GLA_DV = 64
GLA_KW = GLA_HEADS * GLA_DK
GLA_VW = GLA_HEADS * GLA_DV
GLA_GATE_RANK = 16
GLA_TAU = 16.0
GLA_COLS = 2 * GLA_KW + GLA_VW + GLA_GATE_RANK + GLA_VW

N_GROUPS = 4
EXPERTS_PER_GROUP = 4
N_EXPERTS = N_GROUPS * EXPERTS_PER_GROUP
D_EXPERT = 512

LANES = 128
SUBLANES = 8
SMALL_W = LANES
VMEM_LIMIT = 56 * 1024 * 1024
SCAN_OUT_DTYPE = BF16
CONV_HALO = 128

PROJ_TM = 256
MOE_TM = 1024
MOE_SUB = 128
MOE_SCATTER_COLS = 256
CONV_ROWS = 64

RW_CHUNK = 64
SSD_CHUNK = 128
GLA_CHUNK = 64

_NT = (((1,), (1,)), ((), ()))
_TN = (((0,), (0,)), ((), ()))


def _dot(a, b):
    return jnp.dot(a, b, preferred_element_type=F32)


def _dot_nt(a, b):
    return lax.dot_general(a, b, _NT, preferred_element_type=F32)


def _dot_tn(a, b):
    return lax.dot_general(a, b, _TN, preferred_element_type=F32)


def _split2(x):
    hi = x.astype(BF16)
    mid = (x - hi.astype(F32)).astype(BF16)
    return hi, mid


def _cumsum_rows(tri, x):
    hi, mid = _split2(x)
    return _dot(tri, hi) + _dot(tri, mid)


def _norm_mod(x, g, shift, scale):
    h = x * lax.rsqrt(jnp.mean(x * x, -1, keepdims=True) + NORM_EPS) * g
    return h * (1.0 + scale) + shift


def _in_proj_kernel(ctx_ref, x_ref, g_ref, sh_ref, sc_ref, w_ref, *out_refs, widths, ctx_blk):
    xin = jnp.where(pl.program_id(1) < ctx_blk, ctx_ref[0], x_ref[0])
    h = _norm_mod(xin, g_ref[...], sh_ref[0], sc_ref[0]).astype(BF16)
    off = 0
    for o_ref, wd in zip(out_refs, widths):
        o_ref[0] = _dot(h, w_ref[:, off:off + wd])
        off += wd


def _mod_spec(mod, tiles_per_batch):
    d = mod.shape[-1]
    if mod.shape[0] == 1:
        return pl.BlockSpec((1, 1, d), lambda i: (0, 0, 0))
    return pl.BlockSpec((1, 1, d), lambda i: (i // tiles_per_batch, 0, 0))


def in_proj(ctx, x, g, shift_c, scale_c, shift_l, scale_l, w, widths, *, tm):
    bsz, t, d = x.shape
    t_ctx = ctx.shape[1]
    assert t % tm == 0 and t_ctx % tm == 0
    ctx_blk = t_ctx // tm
    n_blk = ctx_blk + t // tm
    shift = jnp.concatenate([shift_l, shift_c], 0)[:, None]
    scale = jnp.concatenate([scale_l, scale_c], 0)[:, None]
    mod_spec = pl.BlockSpec((1, 1, d), lambda b, j: (jnp.where(j < ctx_blk, bsz, b), 0, 0))
    return pl.pallas_call(
        functools.partial(_in_proj_kernel, widths=widths, ctx_blk=ctx_blk),
        grid=(bsz, n_blk),
        in_specs=[
            pl.BlockSpec((1, tm, d), lambda b, j: (b, jnp.minimum(j, ctx_blk - 1), 0)),
            pl.BlockSpec((1, tm, d), lambda b, j: (b, jnp.maximum(j - ctx_blk, 0), 0)),
            pl.BlockSpec((1, d), lambda b, j: (0, 0)),
            mod_spec, mod_spec,
            pl.BlockSpec(w.shape, lambda b, j: (0, 0)),
        ],
        out_specs=[pl.BlockSpec((1, tm, wd), lambda b, j: (b, j, 0)) for wd in widths],
        out_shape=[jax.ShapeDtypeStruct((bsz, t_ctx + t, wd), F32) for wd in widths],
        compiler_params=pltpu.CompilerParams(dimension_semantics=("parallel", "parallel"),
                                             vmem_limit_bytes=VMEM_LIMIT),
        name="in_proj",
    )(ctx, x, g[None], shift, scale, w)


def _dot3(a, b):
    a_hi, a_mid = _split2(a)
    b_hi, b_mid = _split2(b)
    return _dot(a_hi, b_hi) + _dot(a_hi, b_mid) + _dot(a_mid, b_hi)


def _head_ones(width, head):
    r = lax.broadcasted_iota(jnp.int32, (width, width), 0)
    c = lax.broadcasted_iota(jnp.int32, (width, width), 1)
    return jnp.where(r // head == c // head, 1.0, 0.0).astype(BF16)


def _head_sum(x, ones):
    return sum(_dot(p, ones) for p in _split2(x))


def _silu(x):
    return x * jax.nn.sigmoid(x)


def _out_proj_kernel(yr_ref, bonus_ref, rgate_ref, ys_ref, xs_ref, z_ref, yg_ref, gr_ref,
                     gng_ref, gnb_ref, dsk_ref, sng_ref, gng2_ref, w_ref, x_ref, gate_ref, o_ref):
    ones_rw = _head_ones(RW_W, RW_HD)
    both = lambda ref: ref[0, 0].astype(F32) + ref[1, 0].astype(F32)
    yr = both(yr_ref)
    mu = _head_sum(yr, ones_rw) * (1.0 / RW_HD)
    yc = yr - mu
    var = _head_sum(yc * yc, ones_rw) * (1.0 / RW_HD)
    a_out = (yc * lax.rsqrt(var + RW_GN_EPS) * gng_ref[...] + gnb_ref[...] + bonus_ref[0]) * rgate_ref[0]
    ysd = both(ys_ref) + dsk_ref[...] * xs_ref[0]
    t = ysd * _silu(z_ref[0])
    b_out = t * lax.rsqrt(jnp.mean(t * t, -1, keepdims=True) + NORM_EPS) * sng_ref[...]
    yg = both(yg_ref)
    ms = _head_sum(yg * yg, _head_ones(GLA_VW, GLA_DV)) * (1.0 / GLA_DV)
    g_out = yg * lax.rsqrt(ms + NORM_EPS) * gng2_ref[...] * _silu(gr_ref[0])
    m = jnp.concatenate([a_out, b_out, g_out], 1).astype(BF16)
    o_ref[...] = x_ref[...] + gate_ref[0] * _dot(m, w_ref[...])


def out_proj(y_rw, bonus, rgate, y_ssd, conv, z, y_gla, gla_slab, gn_g, gn_b, d_skip, ssm_norm_g, gla_norm_g,
             w, x, gate, *, tm, t_off):
    bsz, t, d = x.shape
    n = bsz * t
    assert t % tm == 0 and t_off % tm == 0
    tpb = t // tm
    ob = t_off // tm
    row = lambda i: (i // tpb, ob + i % tpb)
    both = lambda wd: pl.BlockSpec((2, 1, tm, wd), lambda i: (0,) + row(i) + (0,))
    one = lambda wd, cb=0: pl.BlockSpec((1, tm, wd), lambda i: row(i) + (cb,))
    par = lambda wd: pl.BlockSpec((1, wd), lambda i: (0, 0))
    r_blk = (2 * GLA_KW + GLA_VW) // GLA_VW
    out = pl.pallas_call(
        _out_proj_kernel,
        grid=(n // tm,),
        in_specs=[
            both(RW_W), one(RW_W), one(RW_W),
            both(SSM_W), one(SSM_W), one(SSM_W),
            both(GLA_VW), one(GLA_VW, r_blk),
            par(RW_W), par(RW_W), par(SSM_W), par(SSM_W), par(GLA_VW),
            pl.BlockSpec(w.shape, lambda i: (0, 0)),
            pl.BlockSpec((tm, d), lambda i: (i, 0)),
            _mod_spec(gate[:, None], tpb),
        ],
        out_specs=pl.BlockSpec((tm, d), lambda i: (i, 0)),
        out_shape=jax.ShapeDtypeStruct((n, d), F32),
        compiler_params=pltpu.CompilerParams(dimension_semantics=("parallel",), vmem_limit_bytes=VMEM_LIMIT),
        name="out_proj",
    )(y_rw, bonus, rgate, y_ssd, conv, z, y_gla, gla_slab,
      gn_g[None], gn_b[None], jnp.repeat(d_skip, SSM_HD)[None], ssm_norm_g[None], gla_norm_g[None],
      w, x.reshape(n, d), gate[:, None])
    return out.reshape(bsz, t, d)


def _softplus(x):
    return jnp.maximum(x, 0.0) + jnp.log1p(jnp.exp(-jnp.abs(x)))


def _rwkv_prep_kernel(u_ref, up_ref, un_ref, mup_ref, mun_ref, w2_ref, a2_ref, w0_ref, a0_ref, g2_ref, kk_ref, rk_ref,
                      r_o, k_o, v_o, kn_o, bonus_o, gate_o, lw_o, as_o, *, ctx_blk, n_blk):
    j = pl.program_id(1)
    u = u_ref[0]
    tm = u.shape[0]
    seg_start = jnp.logical_or(j == 0, j == ctx_blk)
    seg_end = jnp.logical_or(j == ctx_blk - 1, j == n_blk - 1)
    prev_row = jnp.where(seg_start, 0.0, up_ref[0, SUBLANES - 1:SUBLANES, :])
    next_row = jnp.where(seg_end, 0.0, un_ref[0, 0:1, :])
    row = lax.broadcasted_iota(jnp.int32, u.shape, 0)
    prev = jnp.where(row == 0, prev_row, pltpu.roll(u, 1, 0))
    nxt = jnp.where(row == tm - 1, next_row, pltpu.roll(u, tm - 1, 0))
    s = u + mup_ref[...] * (prev - u) + mun_ref[...] * (nxt - u)
    r, k, v = s[:, :RW_W], s[:, RW_W:2 * RW_W], s[:, 2 * RW_W:3 * RW_W]
    lora_in = s[:, 3 * RW_W:3 * RW_W + RW_DECAY_RANK + RW_ICLR_RANK]
    gl = s[:, 3 * RW_W + RW_DECAY_RANK + RW_ICLR_RANK:]
    ones = _head_ones(RW_W, RW_HD)
    kr = k * kk_ref[...]
    r_o[0] = r
    k_o[0] = k
    v_o[0] = v
    kn_o[0] = kr * lax.rsqrt(_head_sum(kr * kr, ones) + 1e-12)
    bonus_o[0] = _head_sum(r * k * rk_ref[...], ones) * v
    gate_o[0] = _dot3(jax.nn.sigmoid(gl), g2_ref[...])
    lora_t = jnp.tanh(lora_in)
    for d in range(2):
        w_log = -_softplus(-(w0_ref[d] + _dot3(lora_t, w2_ref[d]))) - 0.5
        lw_o[d, 0] = -jnp.exp(w_log)
        as_o[d, 0] = jax.nn.sigmoid(a0_ref[d] + _dot3(lora_in, a2_ref[d]))


def rwkv_prep(u, mu_prev, mu_next, w0, w2, a0, a2, g2, k_k, r_k, *, t_ctx, tm):
    bsz, t_all, wc = u.shape
    assert t_all % tm == 0 and t_ctx % tm == 0 and tm % SUBLANES == 0
    n_blk = t_all // tm
    h8 = tm // SUBLANES
    last8 = t_all // SUBLANES - 1
    zpad = jnp.zeros((2, RW_DECAY_RANK, RW_W), F32)
    w2p = jnp.concatenate([w2, zpad], 1)
    a2p = jnp.concatenate([zpad, a2], 1)
    par = lambda a: pl.BlockSpec(a.shape, lambda b, j: (0,) * a.ndim)
    params = [mu_prev[None], mu_next[None], w2p, a2p, w0[:, None], a0[:, None], g2, k_k[None], r_k[None]]
    o1 = pl.BlockSpec((1, tm, RW_W), lambda b, j: (b, j, 0))
    o2 = pl.BlockSpec((2, 1, tm, RW_W), lambda b, j: (0, b, j, 0))
    s1 = jax.ShapeDtypeStruct((bsz, t_all, RW_W), F32)
    s2 = jax.ShapeDtypeStruct((2, bsz, t_all, RW_W), F32)
    return pl.pallas_call(
        functools.partial(_rwkv_prep_kernel, ctx_blk=t_ctx // tm, n_blk=n_blk),
        grid=(bsz, n_blk),
        in_specs=[
            pl.BlockSpec((1, tm, wc), lambda b, j: (b, j, 0)),
            pl.BlockSpec((1, SUBLANES, wc), lambda b, j: (b, jnp.maximum(j * h8 - 1, 0), 0)),
            pl.BlockSpec((1, SUBLANES, wc), lambda b, j: (b, jnp.minimum((j + 1) * h8, last8), 0)),
        ] + [par(a) for a in params],
        out_specs=[o1] * 6 + [o2] * 2,
        out_shape=[s1] * 6 + [s2] * 2,
        compiler_params=pltpu.CompilerParams(dimension_semantics=("parallel", "parallel"),
                                             vmem_limit_bytes=VMEM_LIMIT),
        name="rwkv_prep",
    )(u, u, u, *params)


def _ssm_prep_kernel(x_ref, xp_ref, xn_ref, sm_ref, cw_ref, cb_ref, dtb_ref, alog_ref, o_ref, dd_ref, src_ref,
                     *, t_ctx, t_all, halo):
    j = pl.program_id(1)
    tm = x_ref.shape[1]
    t0 = j * tm
    is_ctx = t0 < t_ctx
    seg_lo = jnp.where(is_ctx, 0, t_ctx)
    seg_hi = jnp.where(is_ctx, t_ctx, t_all)
    ctx_i = jnp.where(is_ctx, 1, 0)
    lat_f = jnp.where(is_ctx, 0.0, 1.0)
    ch = x_ref.shape[2]
    reps = ch // LANES
    for piece, start in ((xp_ref, 0), (x_ref, halo), (xn_ref, halo + tm)):
        n = piece.shape[1]
        e_row = lax.broadcasted_iota(jnp.int32, (n, LANES), 0) + start
        t_src = e_row + (t0 - halo)
        in_seg = jnp.logical_and(t_src >= seg_lo, t_src < seg_hi)
        col = e_row % GRID_W
        keep = {0: in_seg,
                -1: jnp.logical_and(in_seg, (jnp.where(col != GRID_W - 1, 1, 0) | ctx_i) > 0),
                1: jnp.logical_and(in_seg, (jnp.where(col != 0, 1, 0) | ctx_i) > 0)}
        val = piece[0]
        for dc in (-1, 0, 1):
            src_ref[dc + 1, start:start + n, :] = jnp.where(jnp.concatenate([keep[dc]] * reps, 1), val, 0.0)
    taps = [(dr, dc) for dr in (-1, 0, 1) for dc in (-1, 0, 1)]
    for c0 in range(0, ch, LANES):
        cs = slice(c0, c0 + LANES)
        wts = []
        for dr, dc in taps:
            wt = cw_ref[(dr + 1) * 3 + dc + 1:(dr + 1) * 3 + dc + 2, cs]
            wts.append(wt * lat_f if dr != 0 else wt)
        for r0 in range(0, tm, CONV_ROWS):
            acc = jnp.zeros((CONV_ROWS, LANES), F32) + cb_ref[:, cs]
            for (dr, dc), wt in zip(taps, wts):
                off = halo + GRID_W * dr + dc + r0
                acc = acc + src_ref[dc + 1, off:off + CONV_ROWS, cs] * wt
            o_ref[0, r0:r0 + CONV_ROWS, cs] = _silu(acc)
    dt = sm_ref[0].T[:SSM_HEADS]
    for d in range(2):
        dtp = _softplus(dt + dtb_ref[d])
        da = -jnp.exp(alog_ref[d]) * dtp
        for c in range(tm // SSD_CHUNK):
            cs = slice(c * SSD_CHUNK, (c + 1) * SSD_CHUNK)
            dd_ref[d, 0, c, :SSM_HEADS, :] = da[:, cs]
            dd_ref[d, 0, c, SSM_HEADS:, :] = dtp[:, cs]


def ssm_prep(xbc, small, conv_w, conv_b, dt_bias, a_log, *, t_ctx, tm):
    bsz, t_all, ch = xbc.shape
    halo = CONV_HALO
    assert t_all % tm == 0 and t_ctx % tm == 0 and tm % halo == 0 and halo > GRID_W + 1 and tm % SSD_CHUNK == 0
    assert tm % CONV_ROWS == 0 and ch % LANES == 0
    n_blk = t_all // tm
    hb = tm // halo
    last = t_all // halo - 1
    par = lambda a: pl.BlockSpec(a.shape, lambda b, j: (0,) * a.ndim)
    params = [conv_w.reshape(9, ch), conv_b[None], dt_bias[:, :, None], a_log[:, :, None]]
    nc = tm // SSD_CHUNK
    return pl.pallas_call(
        functools.partial(_ssm_prep_kernel, t_ctx=t_ctx, t_all=t_all, halo=halo),
        grid=(bsz, n_blk),
        in_specs=[
            pl.BlockSpec((1, tm, ch), lambda b, j: (b, j, 0)),
            pl.BlockSpec((1, halo, ch), lambda b, j: (b, jnp.maximum(j * hb - 1, 0), 0)),
            pl.BlockSpec((1, halo, ch), lambda b, j: (b, jnp.minimum((j + 1) * hb, last), 0)),
            pl.BlockSpec((1, tm, LANES), lambda b, j: (b, j, 0)),
        ] + [par(a) for a in params],
        out_specs=[pl.BlockSpec((1, tm, ch), lambda b, j: (b, j, 0)),
                   pl.BlockSpec((2, 1, nc, 2 * SSM_HEADS, SSD_CHUNK), lambda b, j: (0, b, j, 0, 0))],
        out_shape=[jax.ShapeDtypeStruct((bsz, t_all, ch), F32),
                   jax.ShapeDtypeStruct((2, bsz, t_all // SSD_CHUNK, 2 * SSM_HEADS, SSD_CHUNK), F32)],
        scratch_shapes=[pltpu.VMEM((3, tm + 2 * halo, ch), F32)],
        compiler_params=pltpu.CompilerParams(dimension_semantics=("parallel", "parallel"),
                                             vmem_limit_bytes=VMEM_LIMIT),
        name="ssm_prep",
    )(xbc, xbc, xbc, small, *params)


def _route(logits):
    lane = lax.broadcasted_iota(jnp.int32, logits.shape, 1)
    valid = lane < N_EXPERTS
    neg = -jnp.inf
    big = jnp.int32(1 << 20)
    el = pltpu.roll(logits, LANES - N_EXPERTS, 1)
    glm = jnp.where(valid, logits, neg)
    gmax = jnp.max(glm, -1, keepdims=True)
    g_sel_lane = jnp.min(jnp.where(glm == gmax, lane, big), -1, keepdims=True)
    g_sel = g_sel_lane // EXPERTS_PER_GROUP
    in_group = (lane // EXPERTS_PER_GROUP) == g_sel
    gsum = jnp.sum(jnp.where(valid, jnp.exp(glm - gmax), 0.0), -1, keepdims=True) / EXPERTS_PER_GROUP
    p_group = 1.0 / gsum
    elm = jnp.where(in_group & valid, el, neg)
    m1 = jnp.max(elm, -1, keepdims=True)
    i1 = jnp.min(jnp.where(elm == m1, lane, big), -1, keepdims=True)
    elm2 = jnp.where(lane == i1, neg, elm)
    m2 = jnp.max(elm2, -1, keepdims=True)
    i2 = jnp.min(jnp.where(elm2 == m2, lane, big), -1, keepdims=True)
    p2 = jnp.exp(m2 - m1)
    wa = p_group / (1.0 + p2)
    wb = p_group * p2 / (1.0 + p2)
    return jnp.where(lane == i1, wa, 0.0) + jnp.where(lane == i2, wb, 0.0), g_sel


def _moe_kernel(x_ref, g_ref, sh_ref, sc_ref, gate_ref, rw_ref, rb_ref, fg_ref, w1_ref, w3_ref, w2_ref,
                o_ref, h_ref, comb_ref, code_ref, *, final_norm):
    grp = pl.program_id(1)
    tm, d = x_ref.shape

    @pl.when(grp == 0)
    def _():
        h = _norm_mod(x_ref[...], g_ref[...], sh_ref[0], sc_ref[0])
        h_ref[...] = h.astype(BF16)
        h_hi, h_mid = _split2(h)
        w_hi, w_mid = _split2(rw_ref[...])
        logits = _dot(h_hi, w_hi) + _dot(h_hi, w_mid) + _dot(h_mid, w_hi) + rb_ref[...]
        comb, g_sel = _route(logits)
        comb_ref[...] = comb
        lane = lax.broadcasted_iota(jnp.int32, (tm, LANES), 1)
        member = jnp.where(lane == g_sel, 1.0, 0.0)
        r_t = lax.broadcasted_iota(jnp.int32, (tm, tm), 0)
        c_t = lax.broadcasted_iota(jnp.int32, (tm, tm), 1)
        before = jnp.where(c_t < r_t, 1.0, 0.0).astype(BF16)
        rank = _dot(before, member.astype(BF16))
        code = jnp.where(member > 0.0, rank, -1.0)
        code_ref[...] = code.T[:code_ref.shape[0]]
        o_ref[...] = jnp.zeros_like(o_ref)

    crow = code_ref[pl.ds(grp, 1), :]
    count = jnp.max(crow).astype(jnp.int32) + 1
    comb_parts = _split2(comb_ref[...])

    sub_i = lax.broadcasted_iota(jnp.int32, (MOE_SUB, tm), 0).astype(F32)
    lane_c = lax.broadcasted_iota(jnp.int32, (MOE_SUB, LANES), 1)

    def body(j, carry):
        first = (j * MOE_SUB).astype(F32)
        sel = jnp.where(crow - first == sub_i, 1.0, 0.0).astype(BF16)
        hg = _dot(sel, h_ref[...]).astype(BF16)
        cg = sum(_dot(sel, p) for p in comb_parts)
        ysum = jnp.zeros((MOE_SUB, d), F32)
        for e in range(EXPERTS_PER_GROUP):
            a = _dot(hg, w1_ref[0, e])
            b = _dot(hg, w3_ref[0, e])
            hid = (a * jax.nn.sigmoid(a) * b).astype(BF16)
            col = jnp.sum(jnp.where(lane_c == grp * EXPERTS_PER_GROUP + e, cg, 0.0), -1, keepdims=True)
            ysum = ysum + col * _dot(hid, w2_ref[0, e])
        yb = ysum.astype(BF16)
        for c0 in range(0, d, MOE_SCATTER_COLS):
            cs = slice(c0, c0 + MOE_SCATTER_COLS)
            o_ref[:, cs] += _dot_tn(sel, yb[:, cs])
        return carry

    lax.fori_loop(0, (count + MOE_SUB - 1) // MOE_SUB, body, 0)

    @pl.when(grp == N_GROUPS - 1)
    def _():
        out = x_ref[...] + gate_ref[0] * o_ref[...]
        if final_norm:
            out = out * lax.rsqrt(jnp.mean(out * out, -1, keepdims=True) + NORM_EPS) * fg_ref[...]
        o_ref[...] = out


def moe_block(x, g, shift, scale, gate, rw, rb, final_g, w1, w3, w2, *, tm, final_norm):
    bsz, t, d = x.shape
    n = bsz * t
    tm = min(tm, t)
    assert t % tm == 0 and tm % MOE_SUB == 0 and d % MOE_SCATTER_COLS == 0
    tpb = t // tm
    epg = EXPERTS_PER_GROUP

    def mod_spec(mod):
        if mod.shape[0] == 1:
            return pl.BlockSpec((1, 1, d), lambda i, e: (0, 0, 0))
        return pl.BlockSpec((1, 1, d), lambda i, e: (i // tpb, 0, 0))

    const = lambda shape: pl.BlockSpec(shape, lambda i, e: (0,) * len(shape))
    out = pl.pallas_call(
        functools.partial(_moe_kernel, final_norm=final_norm),
        grid=(n // tm, N_GROUPS),
        in_specs=[
            pl.BlockSpec((tm, d), lambda i, e: (i, 0)),
            const((1, d)),
            mod_spec(shift[:, None]), mod_spec(scale[:, None]), mod_spec(gate[:, None]),
            const(rw.shape), const(rb.shape), const((1, d)),
            pl.BlockSpec((1, epg, d, D_EXPERT), lambda i, e: (e, 0, 0, 0)),
            pl.BlockSpec((1, epg, d, D_EXPERT), lambda i, e: (e, 0, 0, 0)),
            pl.BlockSpec((1, epg, D_EXPERT, d), lambda i, e: (e, 0, 0, 0)),
        ],
        out_specs=pl.BlockSpec((tm, d), lambda i, e: (i, 0)),
        out_shape=jax.ShapeDtypeStruct((n, d), F32),
        scratch_shapes=[pltpu.VMEM((tm, d), BF16), pltpu.VMEM((tm, LANES), F32), pltpu.VMEM((8, tm), F32)],
        compiler_params=pltpu.CompilerParams(dimension_semantics=("parallel", "arbitrary"),
                                             vmem_limit_bytes=VMEM_LIMIT),
        name="moe",
    )(x.reshape(n, d), g[None], shift[:, None], scale[:, None], gate[:, None], rw, rb, final_g[None], w1, w3, w2)
    return out.reshape(bsz, t, d)


def _scan_masks(n, rev):
    row = lax.broadcasted_iota(jnp.int32, (n, n), 0)
    col = lax.broadcasted_iota(jnp.int32, (n, n), 1)
    d = (row - col) * jnp.where(rev, -1, 1)
    return d >= 0, d > 0


def _chunk_rows(s, nsub, size, rev):
    return pl.ds(pl.multiple_of(jnp.where(rev, (nsub - 1 - s) * size, s * size), size), size)


def _last_row(x, rev):
    n = x.shape[0]
    return jnp.where(rev, x[0:1], x[n - 1:n])


def _scan_specs(nseq, blk, n_ctx_blk, n_blk, widths_shared, widths_dir):
    def tblock(i, c):
        back = jnp.where(c < n_ctx_blk, n_ctx_blk - 1 - c, n_ctx_blk + n_blk - 1 - c)
        return jnp.where(i >= nseq, back, c)
    def shared_spec(w, col_blk=0):
        return pl.BlockSpec((1, blk, w), lambda i, c: (i % nseq, tblock(i, c), col_blk))
    shared = [shared_spec(*w) if isinstance(w, tuple) else shared_spec(w) for w in widths_shared]
    per_dir = [pl.BlockSpec((1, blk, w), lambda i, c: (i, tblock(i, c), 0)) for w in widths_dir]
    return shared, per_dir, tblock


def _rwkv_kernel(r_ref, k_ref, v_ref, kk_ref, ka_ref, lw_ref, as_ref, y_ref, st_ref, *, nsub, nseq):
    L = RW_CHUNK
    hd = RW_HD
    rev = pl.program_id(0) >= nseq

    @pl.when(pl.program_id(1) == 0)
    def _():
        st_ref[...] = jnp.zeros_like(st_ref)

    incl, strict = _scan_masks(L, rev)
    tri = jnp.where(incl, 1.0, 0.0).astype(BF16)
    zeros = jnp.zeros((L, hd), BF16)
    chains = [(s, h) for s in range(nsub) for h in range(RW_HEADS)]
    rows = [_chunk_rows(s, nsub, L, rev) for s in range(nsub)]

    prep = []
    for s in range(nsub):
        lw = lw_ref[0, rows[s], :]
        c = _cumsum_rows(tri, lw)
        c_end = _last_row(c, rev)
        e_inv = jnp.exp(-c)
        e_end = jnp.exp(c_end - c)
        a_sig = as_ref[0, rows[s], :]
        kk = kk_ref[0, rows[s], :]
        k = k_ref[0, rows[s], :]
        kmod = k * (1.0 + (a_sig - 1.0) * ka_ref[...])
        bv = kk * a_sig
        prep.append(dict(
            at=-kk * jnp.exp(c - lw), rt=r_ref[0, rows[s], :] * jnp.exp(c),
            bt=(bv * e_inv).astype(BF16), kt=(kmod * e_inv).astype(BF16),
            bh=(bv * e_end).astype(BF16), kh=(kmod * e_end).astype(BF16),
            p_end=jnp.exp(c_end), v=v_ref[0, rows[s], :].astype(BF16)))

    def head(name, s, h):
        return prep[s][name][:, h * hd:(h + 1) * hd]

    sc = {}
    for s, h in chains:
        lhs = jnp.concatenate([head("at", s, h), head("rt", s, h)], 0).astype(BF16)
        rhs = jnp.concatenate([head("bt", s, h), head("kt", s, h)], 0)
        sc[s, h] = _dot_nt(lhs, rhs)
    nmat, x, mr = {}, {}, {}
    for s, h in chains:
        m = sc[s, h]
        nmat[s, h] = jnp.where(strict, m[:L, :L], 0.0)
        mak = jnp.where(strict, m[:L, L:], 0.0).astype(BF16)
        mr[s, h] = jnp.concatenate([jnp.where(incl, m[L:, :L], 0.0), jnp.where(incl, m[L:, L:], 0.0)], 1).astype(BF16)
        x[s, h] = jnp.concatenate([head("at", s, h), _dot(mak, head("v", s, h))], 1)
    steps = L.bit_length() - 1
    for i in range(steps):
        for s, h in chains:
            nb = nmat[s, h].astype(BF16)
            xb = x[s, h].astype(BF16)
            if i + 1 < steps:
                prod = _dot(nb, jnp.concatenate([xb, nb], 1))
                x[s, h] = x[s, h] + prod[:, :2 * hd]
                nmat[s, h] = prod[:, 2 * hd:]
            else:
                x[s, h] = x[s, h] + _dot(nb, xb)
    ftop, bhw_t, gt = {}, {}, {}
    for s, h in chains:
        xb = x[s, h].astype(BF16)
        vb = head("v", s, h)
        z = jnp.concatenate([xb, jnp.concatenate([zeros, vb], 1)], 0)
        ftop[s, h] = _dot(mr[s, h], z)
        t1 = _dot_tn(xb, head("bh", s, h))
        bhw_t[s, h] = t1[:hd].astype(BF16)
        gt[s, h] = t1[hd:] + _dot_tn(vb, head("kh", s, h))
    st = [st_ref[h] for h in range(RW_HEADS)]
    for s in range(nsub):
        ys = []
        for h in range(RW_HEADS):
            stb = st[h].astype(BF16)
            q = (head("rt", s, h) + ftop[s, h][:, :hd]).astype(BF16)
            ys.append(_dot_nt(q, stb) + ftop[s, h][:, hd:])
            st[h] = st[h] * head("p_end", s, h) + _dot(stb, bhw_t[s, h]) + gt[s, h]
        y_ref[0, rows[s], :] = jnp.concatenate(ys, 1).astype(y_ref.dtype)
    for h in range(RW_HEADS):
        st_ref[h] = st[h]


def rwkv_scan(r, k, v, kk, k_a, lw, a_sig, *, t_ctx, nsub=4):
    nseq, t, w = r.shape
    blk = RW_CHUNK * nsub
    assert t % blk == 0 and t_ctx % blk == 0
    shared, per_dir, _ = _scan_specs(nseq, blk, t_ctx // blk, t // blk, [w] * 4, [w] * 3)
    return pl.pallas_call(
        functools.partial(_rwkv_kernel, nsub=nsub, nseq=nseq),
        grid=(2 * nseq, t // blk),
        in_specs=shared + [pl.BlockSpec((1, w), lambda i, c: (0, 0))] + per_dir[:2],
        out_specs=per_dir[2],
        out_shape=jax.ShapeDtypeStruct((2 * nseq, t, w), SCAN_OUT_DTYPE),
        scratch_shapes=[pltpu.VMEM((RW_HEADS, RW_HD, RW_HD), F32)],
        compiler_params=pltpu.CompilerParams(dimension_semantics=("parallel", "arbitrary"),
                                             vmem_limit_bytes=VMEM_LIMIT),
        name="rwkv_scan",
    )(r, k, v, kk, k_a[None], lw, a_sig)


def _expand_cols(x, e):
    return sum(_dot(p, e) for p in _split2(x))


def _ssd_kernel(x_ref, b_ref, c_ref, dd_ref, y_ref, st_ref, *, nsub, nseq):
    L = SSD_CHUNK
    nh = SSM_HEADS
    pw = 2 * SSM_HD
    npairs = nh // 2
    pairs_per_group = npairs // SSM_GROUPS
    rev = pl.program_id(0) >= nseq

    @pl.when(pl.program_id(1) == 0)
    def _():
        st_ref[...] = jnp.zeros_like(st_ref)

    incl, _ = _scan_masks(L, rev)
    tri = jnp.where(incl, 1.0, 0.0).astype(BF16)
    h_i = lax.broadcasted_iota(jnp.int32, (nh, nh * SSM_HD), 0)
    c_i = lax.broadcasted_iota(jnp.int32, (nh, nh * SSM_HD), 1)
    e_head = jnp.where(c_i // SSM_HD == h_i, 1.0, 0.0).astype(BF16)
    h_j = lax.broadcasted_iota(jnp.int32, (nh, nh * L), 0)
    c_j = lax.broadcasted_iota(jnp.int32, (nh, nh * L), 1)
    e_wide = jnp.where(c_j // L == h_j, 1.0, 0.0).astype(BF16)
    lane_p = lax.broadcasted_iota(jnp.int32, (L, pw), 1)
    first_half = lane_p < SSM_HD
    r_bd = lax.broadcasted_iota(jnp.int32, (pw, pw), 0)
    c_bd = lax.broadcasted_iota(jnp.int32, (pw, pw), 1)
    bd_mask = (r_bd < SSM_HD) == (c_bd < SSM_HD)

    rows = [_chunk_rows(s, nsub, L, rev) for s in range(nsub)]
    work = []
    for s in range(nsub):
        dd = dd_ref[0, jnp.where(rev, nsub - 1 - s, s)]
        da_parts = _split2(dd[:nh])
        acs_row = sum(_dot_nt(p, tri) for p in da_parts)
        acs_col = sum(_dot_nt(tri, p) for p in da_parts)
        colx = _expand_cols(acs_col, e_wide)
        dtx = sum(_dot_tn(p, e_head) for p in _split2(dd[nh:]))
        xdt = (x_ref[0, rows[s], :] * dtx).astype(BF16)
        bm = b_ref[0, rows[s], :]
        cm = c_ref[0, rows[s], :]
        bm_sw = pltpu.roll(bm, SSM_STATE, 1)
        cm_sw = pltpu.roll(cm, SSM_STATE, 1)
        b2 = [jnp.where(first_half, bm, bm_sw), jnp.where(first_half, bm_sw, bm)]
        c2 = [jnp.where(first_half, cm, cm_sw), jnp.where(first_half, cm_sw, cm)]
        cb = [_dot_nt(cm[:, g * SSM_STATE:(g + 1) * SSM_STATE].astype(BF16),
                      bm[:, g * SSM_STATE:(g + 1) * SSM_STATE].astype(BF16)) for g in range(SSM_GROUPS)]
        work.append(dict(acs_row=acs_row, colx=colx, xdt=xdt, b2=b2, c2=c2, cb=cb))

    y_diag, ce, new, ea = {}, {}, {}, {}
    for s in range(nsub):
        w = work[s]
        for p in range(npairs):
            g = p // pairs_per_group
            h0, h1 = 2 * p, 2 * p + 1
            cx0 = w["colx"][:, h0 * L:(h0 + 1) * L]
            cx1 = w["colx"][:, h1 * L:(h1 + 1) * L]
            s0 = w["cb"][g] * jnp.exp(jnp.where(incl, cx0 - w["acs_row"][h0:h0 + 1, :], -jnp.inf))
            s1 = w["cb"][g] * jnp.exp(jnp.where(incl, cx1 - w["acs_row"][h1:h1 + 1, :], -jnp.inf))
            xp = w["xdt"][:, p * pw:(p + 1) * pw]
            zero = jnp.zeros_like(xp)
            x_bd = jnp.concatenate([jnp.where(first_half, xp, zero), jnp.where(first_half, zero, xp)], 0)
            y_diag[s, p] = _dot(jnp.concatenate([s0, s1], 1).astype(BF16), x_bd)
            col = jnp.where(first_half, cx0[:, :pw], cx1[:, :pw])
            a_end = _last_row(col, rev)
            ce[s, p] = (w["c2"][g] * jnp.exp(col)).astype(BF16)
            be = (w["b2"][g] * jnp.exp(a_end - col)).astype(BF16)
            new[s, p] = jnp.where(bd_mask, _dot_tn(be, xp), 0.0)
            ea[s, p] = jnp.exp(a_end)
    st = [st_ref[p] for p in range(npairs)]
    for s in range(nsub):
        ys = []
        for p in range(npairs):
            ys.append(y_diag[s, p] + _dot(ce[s, p], st[p].astype(BF16)))
            st[p] = st[p] * ea[s, p] + new[s, p]
        y_ref[0, rows[s], :] = jnp.concatenate(ys, 1).astype(y_ref.dtype)
    for p in range(npairs):
        st_ref[p] = st[p]


def ssd_scan(conv, dd, *, t_ctx, nsub=2):
    nseq, t, _ = conv.shape
    w = SSM_W
    blk = SSD_CHUNK * nsub
    assert t % blk == 0 and t_ctx % blk == 0
    assert SSM_STATE == SSM_HD and SSD_CHUNK == 2 * SSM_HD
    gw = SSM_GROUPS * SSM_STATE
    shared, per_dir, tblock = _scan_specs(nseq, blk, t_ctx // blk, t // blk,
                                          [(w, 0), (gw, w // gw), (gw, w // gw + 1)], [w])
    dd_spec = pl.BlockSpec((1, nsub, 2 * SSM_HEADS, SSD_CHUNK), lambda i, c: (i, tblock(i, c), 0, 0))
    return pl.pallas_call(
        functools.partial(_ssd_kernel, nsub=nsub, nseq=nseq),
        grid=(2 * nseq, t // blk),
        in_specs=shared + [dd_spec],
        out_specs=per_dir[0],
        out_shape=jax.ShapeDtypeStruct((2 * nseq, t, w), SCAN_OUT_DTYPE),
        scratch_shapes=[pltpu.VMEM((SSM_HEADS // 2, 2 * SSM_STATE, 2 * SSM_HD), F32)],
        compiler_params=pltpu.CompilerParams(dimension_semantics=("parallel", "arbitrary"),
                                             vmem_limit_bytes=VMEM_LIMIT),
        name="ssd_scan",
    )(conv, conv, conv, dd)


def _gla_kernel(q_ref, k_ref, v_ref, sm_ref, ga_ref, gb_ref, y_ref, st_ref, lg_ref, *, nsub, nseq):
    L = GLA_CHUNK
    rev = pl.program_id(0) >= nseq

    @pl.when(pl.program_id(1) == 0)
    def _():
        st_ref[...] = jnp.zeros_like(st_ref)

    nh = GLA_HEADS
    incl, _ = _scan_masks(L, rev)
    tri = jnp.where(incl, 1.0, 0.0).astype(BF16)
    incl_h = jnp.concatenate([incl] * nh, 0)
    r_q = lax.broadcasted_iota(jnp.int32, (nh * L, GLA_KW), 0)
    c_q = lax.broadcasted_iota(jnp.int32, (nh * L, GLA_KW), 1)
    q_mask = r_q // L == c_q // GLA_DK
    r_s = lax.broadcasted_iota(jnp.int32, (GLA_VW, GLA_KW), 0)
    c_s = lax.broadcasted_iota(jnp.int32, (GLA_VW, GLA_KW), 1)
    st_mask = r_s // GLA_DV == c_s // GLA_DK
    lane_v = lax.broadcasted_iota(jnp.int32, (L, GLA_VW), 1) // GLA_DV
    rows = [_chunk_rows(s, nsub, L, rev) for s in range(nsub)]

    gate_logit = _dot3(sm_ref[0], ga_ref[0]) + gb_ref[0]
    lg_ref[...] = -_softplus(-gate_logit) * (1.0 / GLA_TAU)
    bcs = [_cumsum_rows(tri, lg_ref[rows[s], :]) for s in range(nsub)]
    vb, ke, q_st, qd, kd, eb = [], [], [], [], [], []
    for s in range(nsub):
        mid = jnp.where(rev, bcs[s][L - 1 - L // 2:L - L // 2], bcs[s][L // 2:L // 2 + 1])
        bend = _last_row(bcs[s], rev)
        q = q_ref[0, rows[s], :] * GLA_DK ** -0.5
        k = k_ref[0, rows[s], :]
        vb.append(v_ref[0, rows[s], :].astype(BF16))
        qe = (q * jnp.exp(bcs[s] - mid)).astype(BF16)
        ke.append((k * jnp.exp(mid - bcs[s])).astype(BF16))
        q_st.append(jnp.where(q_mask, jnp.concatenate([qe] * nh, 0), jnp.zeros((), BF16)))
        qd.append((q * jnp.exp(bcs[s])).astype(BF16))
        kd.append((k * jnp.exp(bend - bcs[s])).astype(BF16))
        eb.append(jnp.exp(bend))
    att = [jnp.where(incl_h, _dot_nt(q_st[s], ke[s]), 0.0).astype(BF16) for s in range(nsub)]
    full = [_dot(att[s], vb[s]) for s in range(nsub)]
    o_intra = [sum(jnp.where(lane_v == h, full[s][h * L:(h + 1) * L], 0.0) for h in range(nh)) for s in range(nsub)]
    kv_t = [jnp.where(st_mask, _dot_tn(vb[s], kd[s]), 0.0) for s in range(nsub)]
    st = st_ref[...]
    for s in range(nsub):
        y_ref[0, rows[s], :] = (o_intra[s] + _dot_nt(qd[s], st.astype(BF16))).astype(y_ref.dtype)
        st = st * eb[s] + kv_t[s]
    st_ref[...] = st


def gla_scan(slab, small, ga2, gb, *, t_ctx, nsub=4):
    nseq, t, _ = slab.shape
    kw, vw = GLA_KW, GLA_VW
    blk = GLA_CHUNK * nsub
    assert t % blk == 0 and t_ctx % blk == 0
    shared, per_dir, _ = _scan_specs(nseq, blk, t_ctx // blk, t // blk,
                                     [(kw, 0), (kw, 1), (vw, 2 * kw // vw), LANES], [vw])
    rank = ga2.shape[1]
    ga_pad = jnp.zeros((2, LANES, kw), F32).at[:, SSM_HEADS:SSM_HEADS + rank].set(ga2)
    return pl.pallas_call(
        functools.partial(_gla_kernel, nsub=nsub, nseq=nseq),
        grid=(2 * nseq, t // blk),
        in_specs=shared + [pl.BlockSpec((1, LANES, kw), lambda i, c: (i // nseq, 0, 0)),
                           pl.BlockSpec((1, 1, kw), lambda i, c: (i // nseq, 0, 0))],
        out_specs=per_dir[0],
        out_shape=jax.ShapeDtypeStruct((2 * nseq, t, vw), SCAN_OUT_DTYPE),
        scratch_shapes=[pltpu.VMEM((GLA_VW, GLA_KW), F32), pltpu.VMEM((blk, kw), F32)],
        compiler_params=pltpu.CompilerParams(dimension_semantics=("parallel", "arbitrary"),
                                             vmem_limit_bytes=VMEM_LIMIT),
        name="gla_scan",
    )(slab, slab, slab, small, ga_pad, gb[:, None])


_HP = lax.Precision.HIGHEST


_IN_WIDTHS = (RW_COLS, SSM_W, SSM_CONV_CH, 2 * GLA_KW + 2 * GLA_VW, SMALL_W)


def _arrange_w_in(w_in):
    rw, ssm, gla = jnp.split(w_in, [RW_COLS, RW_COLS + SSM_COLS], -1)
    z, xbc, dt = jnp.split(ssm, [SSM_W, SSM_W + SSM_CONV_CH], -1)
    qkv, gl, r = jnp.split(gla, [2 * GLA_KW + GLA_VW, 2 * GLA_KW + GLA_VW + GLA_GATE_RANK], -1)
    pad = jnp.zeros((w_in.shape[0], SMALL_W - SSM_HEADS - GLA_GATE_RANK), w_in.dtype)
    return jnp.concatenate([rw, z, xbc, qkv, r, dt, gl, pad], -1).astype(BF16)


def _pad_lanes(w, reps=1):
    w = jnp.repeat(w, reps, axis=-1) if reps > 1 else w
    return jnp.pad(w, [(0, 0)] * (w.ndim - 1) + [(0, LANES - w.shape[-1])])


def kernel(x, c, ctx, c_ctx, ada_w, ada_b, norm1_g, norm2_g, w_in, w_out, rw_mu_prev, rw_mu_next, rw_w0, rw_w2, rw_a0, rw_a2, rw_g2, rw_k_k, rw_k_a, rw_r_k, rw_gn_g, rw_gn_b, ssm_conv_w, ssm_conv_b, ssm_dt_bias, ssm_a_log, ssm_d, ssm_norm_g, gla_ga2, gla_gb, gla_norm_g, moe_rg_w, moe_rg_b, moe_re_w, moe_re_b, moe_w1, moe_w3, moe_w2, final_g):
    depth = ada_w.shape[0]
    bsz, t_lat, _ = x.shape
    t_ctx = ctx.shape[1]
    assert t_lat % GRID_W == 0
    cond_l = jax.nn.silu(c)
    cond_c = jax.nn.silu(c_ctx)[None]
    tm = PROJ_TM
    for l in range(depth):
        ctx_out = l < depth - 1
        mod_l = jnp.split(jnp.dot(cond_l, ada_w[l], precision=_HP) + ada_b[l], 6, -1)
        mod_c = jnp.split(jnp.dot(cond_c, ada_w[l], precision=_HP) + ada_b[l], 6, -1)
        w_in_l = _arrange_w_in(w_in[l])
        u_rw, z, xbc, u_gla, small = in_proj(ctx, x, norm1_g[l], mod_c[0], mod_c[1], mod_l[0], mod_l[1], w_in_l,
                                             _IN_WIDTHS, tm=tm)
        r, k, v, kk, bonus, rgate, lw, a_sig = rwkv_prep(u_rw, rw_mu_prev[l], rw_mu_next[l], rw_w0[l], rw_w2[l],
                                                         rw_a0[l], rw_a2[l], rw_g2[l], rw_k_k[l], rw_r_k[l],
                                                         t_ctx=t_ctx, tm=tm)
        merge = lambda a: a.reshape((2 * bsz,) + a.shape[2:])
        split = lambda a: a.reshape((2, bsz) + a.shape[1:])
        y_rw = split(rwkv_scan(r, k, v, kk, rw_k_a[l], merge(lw), merge(a_sig), t_ctx=t_ctx))
        conv, dd = ssm_prep(xbc, small, ssm_conv_w[l], ssm_conv_b[l], ssm_dt_bias[l], ssm_a_log[l], t_ctx=t_ctx, tm=tm)
        y_ssd = split(ssd_scan(conv, merge(dd), t_ctx=t_ctx))
        y_gla = split(gla_scan(u_gla, small, gla_ga2[l], gla_gb[l], t_ctx=t_ctx))
        mixed = (y_rw, bonus, rgate, y_ssd, conv, z, y_gla, u_gla,
                 rw_gn_g[l], rw_gn_b[l], ssm_d[l], ssm_norm_g[l], gla_norm_g[l])
        w_out_l = w_out[l].astype(BF16)
        epg = EXPERTS_PER_GROUP
        rw = _pad_lanes(jnp.concatenate([jnp.repeat(moe_rg_w[l], epg, -1), moe_re_w[l]], -1))
        rb = _pad_lanes(jnp.concatenate([jnp.repeat(moe_rg_b[l], epg, -1), moe_re_b[l]], -1)[None])
        by_group = lambda w: w.astype(BF16).reshape((N_GROUPS, epg) + w.shape[1:])
        w1, w3, w2 = by_group(moe_w1[l]), by_group(moe_w3[l]), by_group(moe_w2[l])
        x = out_proj(*mixed, w_out_l, x, mod_l[2], tm=tm, t_off=t_ctx)
        x = moe_block(x, norm2_g[l], mod_l[3], mod_l[4], mod_l[5], rw, rb, final_g, w1, w3, w2, tm=MOE_TM,
                      final_norm=not ctx_out)
        if ctx_out:
            ctx = out_proj(*mixed, w_out_l, ctx, mod_c[2], tm=tm, t_off=0)
            ctx = moe_block(ctx, norm2_g[l], mod_c[3], mod_c[4], mod_c[5], rw, rb, final_g, w1, w3, w2, tm=MOE_TM,
                            final_norm=False)
    return x
```

```python
import functools

import jax
import jax.numpy as jnp
from jax import lax
from jax.experimental import pallas as pl
from jax.experimental.pallas import tpu as pltpu

F32 = jnp.float32
BF16 = jnp.bfloat16

D_MODEL = 1024
GRID_W = 64
NORM_EPS = 1e-6

RW_HEADS = 4
RW_HD = 64
RW_W = RW_HEADS * RW_HD
RW_DECAY_RANK = 64
RW_ICLR_RANK = 64
RW_GATE_RANK = 128
RW_GN_EPS = 64e-5
RW_COLS = 3 * RW_W + RW_DECAY_RANK + RW_ICLR_RANK + RW_GATE_RANK

SSM_HEADS = 8
SSM_HD = 64
SSM_W = SSM_HEADS * SSM_HD
SSM_GROUPS = 2
SSM_STATE = 64
SSM_CONV_CH = SSM_W + 2 * SSM_GROUPS * SSM_STATE
SSM_COLS = SSM_W + SSM_CONV_CH + SSM_HEADS

GLA_HEADS = 4
GLA_DK = 32
GLA_DV = 64
GLA_KW = GLA_HEADS * GLA_DK
GLA_VW = GLA_HEADS * GLA_DV
GLA_GATE_RANK = 16
GLA_TAU = 16.0
GLA_COLS = 2 * GLA_KW + GLA_VW + GLA_GATE_RANK + GLA_VW

N_GROUPS = 4
EXPERTS_PER_GROUP = 4
N_EXPERTS = N_GROUPS * EXPERTS_PER_GROUP
D_EXPERT = 512

LANES = 128
SUBLANES = 8
SMALL_W = LANES
VMEM_LIMIT = 56 * 1024 * 1024
SCAN_OUT_DTYPE = BF16
CONV_HALO = 128

PROJ_TM = 256
MOE_TM = 1024
MOE_SUB = 256
MOE_SCATTER_COLS = 256
CONV_ROWS = 64

RW_CHUNK = 64
SSD_CHUNK = 128
GLA_CHUNK = 64

_NT = (((1,), (1,)), ((), ()))
_TN = (((0,), (0,)), ((), ()))


def _dot(a, b):
    return jnp.dot(a, b, preferred_element_type=F32)


def _dot_nt(a, b):
    return lax.dot_general(a, b, _NT, preferred_element_type=F32)


def _dot_tn(a, b):
    return lax.dot_general(a, b, _TN, preferred_element_type=F32)


def _split2(x):
    hi = x.astype(BF16)
    mid = (x - hi.astype(F32)).astype(BF16)
    return hi, mid


def _cumsum_rows(tri, x):
    hi, mid = _split2(x)
    return _dot(tri, hi) + _dot(tri, mid)


def _norm_mod(x, g, shift, scale):
    h = x * lax.rsqrt(jnp.mean(x * x, -1, keepdims=True) + NORM_EPS) * g
    return h * (1.0 + scale) + shift


def _in_proj_kernel(ctx_ref, x_ref, g_ref, sh_ref, sc_ref, w_ref, *out_refs, widths, ctx_blk):
    xin = jnp.where(pl.program_id(1) < ctx_blk, ctx_ref[0], x_ref[0])
    h = _norm_mod(xin, g_ref[...], sh_ref[0], sc_ref[0]).astype(BF16)
    off = 0
    for o_ref, wd in zip(out_refs, widths):
        o_ref[0] = _dot(h, w_ref[:, off:off + wd])
        off += wd


def _mod_spec(mod, tiles_per_batch):
    d = mod.shape[-1]
    if mod.shape[0] == 1:
        return pl.BlockSpec((1, 1, d), lambda i: (0, 0, 0))
    return pl.BlockSpec((1, 1, d), lambda i: (i // tiles_per_batch, 0, 0))


def in_proj(ctx, x, g, shift_c, scale_c, shift_l, scale_l, w, widths, *, tm):
    bsz, t, d = x.shape
    t_ctx = ctx.shape[1]
    assert t % tm == 0 and t_ctx % tm == 0
    ctx_blk = t_ctx // tm
    n_blk = ctx_blk + t // tm
    shift = jnp.concatenate([shift_l, shift_c], 0)[:, None]
    scale = jnp.concatenate([scale_l, scale_c], 0)[:, None]
    mod_spec = pl.BlockSpec((1, 1, d), lambda b, j: (jnp.where(j < ctx_blk, bsz, b), 0, 0))
    return pl.pallas_call(
        functools.partial(_in_proj_kernel, widths=widths, ctx_blk=ctx_blk),
        grid=(bsz, n_blk),
        in_specs=[
            pl.BlockSpec((1, tm, d), lambda b, j: (b, jnp.minimum(j, ctx_blk - 1), 0)),
            pl.BlockSpec((1, tm, d), lambda b, j: (b, jnp.maximum(j - ctx_blk, 0), 0)),
            pl.BlockSpec((1, d), lambda b, j: (0, 0)),
            mod_spec, mod_spec,
            pl.BlockSpec(w.shape, lambda b, j: (0, 0)),
        ],
        out_specs=[pl.BlockSpec((1, tm, wd), lambda b, j: (b, j, 0)) for wd in widths],
        out_shape=[jax.ShapeDtypeStruct((bsz, t_ctx + t, wd), F32) for wd in widths],
        compiler_params=pltpu.CompilerParams(dimension_semantics=("parallel", "parallel"),
                                             vmem_limit_bytes=VMEM_LIMIT),
        name="in_proj",
    )(ctx, x, g[None], shift, scale, w)


def _dot3(a, b):
    a_hi, a_mid = _split2(a)
    b_hi, b_mid = _split2(b)
    return _dot(a_hi, b_hi) + _dot(a_hi, b_mid) + _dot(a_mid, b_hi)


def _head_ones(width, head):
    r = lax.broadcasted_iota(jnp.int32, (width, width), 0)
    c = lax.broadcasted_iota(jnp.int32, (width, width), 1)
    return jnp.where(r // head == c // head, 1.0, 0.0).astype(BF16)


def _head_sum(x, ones):
    return sum(_dot(p, ones) for p in _split2(x))


def _silu(x):
    return x * jax.nn.sigmoid(x)


def _out_proj_kernel(yr_ref, bonus_ref, rgate_ref, ys_ref, xs_ref, z_ref, yg_ref, gr_ref,
                     gng_ref, gnb_ref, dsk_ref, sng_ref, gng2_ref, w_ref, x_ref, gate_ref, o_ref):
    ones_rw = _head_ones(RW_W, RW_HD)
    both = lambda ref: ref[0, 0].astype(F32) + ref[1, 0].astype(F32)
    yr = both(yr_ref)
    mu = _head_sum(yr, ones_rw) * (1.0 / RW_HD)
    yc = yr - mu
    var = _head_sum(yc * yc, ones_rw) * (1.0 / RW_HD)
    a_out = (yc * lax.rsqrt(var + RW_GN_EPS) * gng_ref[...] + gnb_ref[...] + bonus_ref[0]) * rgate_ref[0]
    ysd = both(ys_ref) + dsk_ref[...] * xs_ref[0]
    t = ysd * _silu(z_ref[0])
    b_out = t * lax.rsqrt(jnp.mean(t * t, -1, keepdims=True) + NORM_EPS) * sng_ref[...]
    yg = both(yg_ref)
    ms = _head_sum(yg * yg, _head_ones(GLA_VW, GLA_DV)) * (1.0 / GLA_DV)
    g_out = yg * lax.rsqrt(ms + NORM_EPS) * gng2_ref[...] * _silu(gr_ref[0])
    m = jnp.concatenate([a_out, b_out, g_out], 1).astype(BF16)
    o_ref[...] = x_ref[...] + gate_ref[0] * _dot(m, w_ref[...])


def out_proj(y_rw, bonus, rgate, y_ssd, conv, z, y_gla, gla_slab, gn_g, gn_b, d_skip, ssm_norm_g, gla_norm_g,
             w, x, gate, *, tm, t_off):
    bsz, t, d = x.shape
    n = bsz * t
    assert t % tm == 0 and t_off % tm == 0
    tpb = t // tm
    ob = t_off // tm
    row = lambda i: (i // tpb, ob + i % tpb)
    both = lambda wd: pl.BlockSpec((2, 1, tm, wd), lambda i: (0,) + row(i) + (0,))
    one = lambda wd, cb=0: pl.BlockSpec((1, tm, wd), lambda i: row(i) + (cb,))
    par = lambda wd: pl.BlockSpec((1, wd), lambda i: (0, 0))
    r_blk = (2 * GLA_KW + GLA_VW) // GLA_VW
    out = pl.pallas_call(
        _out_proj_kernel,
        grid=(n // tm,),
        in_specs=[
            both(RW_W), one(RW_W), one(RW_W),
            both(SSM_W), one(SSM_W), one(SSM_W),
            both(GLA_VW), one(GLA_VW, r_blk),
            par(RW_W), par(RW_W), par(SSM_W), par(SSM_W), par(GLA_VW),
            pl.BlockSpec(w.shape, lambda i: (0, 0)),
            pl.BlockSpec((tm, d), lambda i: (i, 0)),
            _mod_spec(gate[:, None], tpb),
        ],
        out_specs=pl.BlockSpec((tm, d), lambda i: (i, 0)),
        out_shape=jax.ShapeDtypeStruct((n, d), F32),
        compiler_params=pltpu.CompilerParams(dimension_semantics=("parallel",), vmem_limit_bytes=VMEM_LIMIT),
        name="out_proj",
    )(y_rw, bonus, rgate, y_ssd, conv, z, y_gla, gla_slab,
      gn_g[None], gn_b[None], jnp.repeat(d_skip, SSM_HD)[None], ssm_norm_g[None], gla_norm_g[None],
      w, x.reshape(n, d), gate[:, None])
    return out.reshape(bsz, t, d)


def _softplus(x):
    return jnp.maximum(x, 0.0) + jnp.log1p(jnp.exp(-jnp.abs(x)))


def _rwkv_prep_kernel(u_ref, up_ref, un_ref, mup_ref, mun_ref, w2_ref, a2_ref, w0_ref, a0_ref, g2_ref, kk_ref, rk_ref,
                      r_o, k_o, v_o, kn_o, bonus_o, gate_o, lw_o, as_o, *, ctx_blk, n_blk):
    j = pl.program_id(1)
    u = u_ref[0]
    tm = u.shape[0]
    seg_start = jnp.logical_or(j == 0, j == ctx_blk)
    seg_end = jnp.logical_or(j == ctx_blk - 1, j == n_blk - 1)
    prev_row = jnp.where(seg_start, 0.0, up_ref[0, SUBLANES - 1:SUBLANES, :])
    next_row = jnp.where(seg_end, 0.0, un_ref[0, 0:1, :])
    row = lax.broadcasted_iota(jnp.int32, u.shape, 0)
    prev = jnp.where(row == 0, prev_row, pltpu.roll(u, 1, 0))
    nxt = jnp.where(row == tm - 1, next_row, pltpu.roll(u, tm - 1, 0))
    s = u + mup_ref[...] * (prev - u) + mun_ref[...] * (nxt - u)
    r, k, v = s[:, :RW_W], s[:, RW_W:2 * RW_W], s[:, 2 * RW_W:3 * RW_W]
    lora_in = s[:, 3 * RW_W:3 * RW_W + RW_DECAY_RANK + RW_ICLR_RANK]
    gl = s[:, 3 * RW_W + RW_DECAY_RANK + RW_ICLR_RANK:]
    ones = _head_ones(RW_W, RW_HD)
    kr = k * kk_ref[...]
    r_o[0] = r
    k_o[0] = k
    v_o[0] = v
    kn_o[0] = kr * lax.rsqrt(_head_sum(kr * kr, ones) + 1e-12)
    bonus_o[0] = _head_sum(r * k * rk_ref[...], ones) * v
    gate_o[0] = _dot3(jax.nn.sigmoid(gl), g2_ref[...])
    lora_t = jnp.tanh(lora_in)
    for d in range(2):
        w_log = -_softplus(-(w0_ref[d] + _dot3(lora_t, w2_ref[d]))) - 0.5
        lw_o[d, 0] = -jnp.exp(w_log)
        as_o[d, 0] = jax.nn.sigmoid(a0_ref[d] + _dot3(lora_in, a2_ref[d]))


def rwkv_prep(u, mu_prev, mu_next, w0, w2, a0, a2, g2, k_k, r_k, *, t_ctx, tm):
    bsz, t_all, wc = u.shape
    assert t_all % tm == 0 and t_ctx % tm == 0 and tm % SUBLANES == 0
    n_blk = t_all // tm
    h8 = tm // SUBLANES
    last8 = t_all // SUBLANES - 1
    zpad = jnp.zeros((2, RW_DECAY_RANK, RW_W), F32)
    w2p = jnp.concatenate([w2, zpad], 1)
    a2p = jnp.concatenate([zpad, a2], 1)
    par = lambda a: pl.BlockSpec(a.shape, lambda b, j: (0,) * a.ndim)
    params = [mu_prev[None], mu_next[None], w2p, a2p, w0[:, None], a0[:, None], g2, k_k[None], r_k[None]]
    o1 = pl.BlockSpec((1, tm, RW_W), lambda b, j: (b, j, 0))
    o2 = pl.BlockSpec((2, 1, tm, RW_W), lambda b, j: (0, b, j, 0))
    s1 = jax.ShapeDtypeStruct((bsz, t_all, RW_W), F32)
    s2 = jax.ShapeDtypeStruct((2, bsz, t_all, RW_W), F32)
    return pl.pallas_call(
        functools.partial(_rwkv_prep_kernel, ctx_blk=t_ctx // tm, n_blk=n_blk),
        grid=(bsz, n_blk),
        in_specs=[
            pl.BlockSpec((1, tm, wc), lambda b, j: (b, j, 0)),
            pl.BlockSpec((1, SUBLANES, wc), lambda b, j: (b, jnp.maximum(j * h8 - 1, 0), 0)),
            pl.BlockSpec((1, SUBLANES, wc), lambda b, j: (b, jnp.minimum((j + 1) * h8, last8), 0)),
        ] + [par(a) for a in params],
        out_specs=[o1] * 6 + [o2] * 2,
        out_shape=[s1] * 6 + [s2] * 2,
        compiler_params=pltpu.CompilerParams(dimension_semantics=("parallel", "parallel"),
                                             vmem_limit_bytes=VMEM_LIMIT),
        name="rwkv_prep",
    )(u, u, u, *params)


def _ssm_prep_kernel(x_ref, xp_ref, xn_ref, sm_ref, cw_ref, cb_ref, dtb_ref, alog_ref, o_ref, dd_ref, src_ref,
                     *, t_ctx, t_all, halo):
    j = pl.program_id(1)
    tm = x_ref.shape[1]
    t0 = j * tm
    is_ctx = t0 < t_ctx
    seg_lo = jnp.where(is_ctx, 0, t_ctx)
    seg_hi = jnp.where(is_ctx, t_ctx, t_all)
    ctx_i = jnp.where(is_ctx, 1, 0)
    lat_f = jnp.where(is_ctx, 0.0, 1.0)
    ch = x_ref.shape[2]
    reps = ch // LANES
    for piece, start in ((xp_ref, 0), (x_ref, halo), (xn_ref, halo + tm)):
        n = piece.shape[1]
        e_row = lax.broadcasted_iota(jnp.int32, (n, LANES), 0) + start
        t_src = e_row + (t0 - halo)
        in_seg = jnp.logical_and(t_src >= seg_lo, t_src < seg_hi)
        col = e_row % GRID_W
        keep = {0: in_seg,
                -1: jnp.logical_and(in_seg, (jnp.where(col != GRID_W - 1, 1, 0) | ctx_i) > 0),
                1: jnp.logical_and(in_seg, (jnp.where(col != 0, 1, 0) | ctx_i) > 0)}
        val = piece[0]
        for dc in (-1, 0, 1):
            src_ref[dc + 1, start:start + n, :] = jnp.where(jnp.concatenate([keep[dc]] * reps, 1), val, 0.0)
    taps = [(dr, dc) for dr in (-1, 0, 1) for dc in (-1, 0, 1)]
    for c0 in range(0, ch, LANES):
        cs = slice(c0, c0 + LANES)
        wts = []
        for dr, dc in taps:
            wt = cw_ref[(dr + 1) * 3 + dc + 1:(dr + 1) * 3 + dc + 2, cs]
            wts.append(wt * lat_f if dr != 0 else wt)
        for r0 in range(0, tm, CONV_ROWS):
            acc = jnp.zeros((CONV_ROWS, LANES), F32) + cb_ref[:, cs]
            for (dr, dc), wt in zip(taps, wts):
                off = halo + GRID_W * dr + dc + r0
                acc = acc + src_ref[dc + 1, off:off + CONV_ROWS, cs] * wt
            o_ref[0, r0:r0 + CONV_ROWS, cs] = _silu(acc)
    dt = sm_ref[0].T[:SSM_HEADS]
    for d in range(2):
        dtp = _softplus(dt + dtb_ref[d])
        da = -jnp.exp(alog_ref[d]) * dtp
        for c in range(tm // SSD_CHUNK):
            cs = slice(c * SSD_CHUNK, (c + 1) * SSD_CHUNK)
            dd_ref[d, 0, c, :SSM_HEADS, :] = da[:, cs]
            dd_ref[d, 0, c, SSM_HEADS:, :] = dtp[:, cs]


def ssm_prep(xbc, small, conv_w, conv_b, dt_bias, a_log, *, t_ctx, tm):
    bsz, t_all, ch = xbc.shape
    halo = CONV_HALO
    assert t_all % tm == 0 and t_ctx % tm == 0 and tm % halo == 0 and halo > GRID_W + 1 and tm % SSD_CHUNK == 0
    assert tm % CONV_ROWS == 0 and ch % LANES == 0
    n_blk = t_all // tm
    hb = tm // halo
    last = t_all // halo - 1
    par = lambda a: pl.BlockSpec(a.shape, lambda b, j: (0,) * a.ndim)
    params = [conv_w.reshape(9, ch), conv_b[None], dt_bias[:, :, None], a_log[:, :, None]]
    nc = tm // SSD_CHUNK
    return pl.pallas_call(
        functools.partial(_ssm_prep_kernel, t_ctx=t_ctx, t_all=t_all, halo=halo),
        grid=(bsz, n_blk),
        in_specs=[
            pl.BlockSpec((1, tm, ch), lambda b, j: (b, j, 0)),
            pl.BlockSpec((1, halo, ch), lambda b, j: (b, jnp.maximum(j * hb - 1, 0), 0)),
            pl.BlockSpec((1, halo, ch), lambda b, j: (b, jnp.minimum((j + 1) * hb, last), 0)),
            pl.BlockSpec((1, tm, LANES), lambda b, j: (b, j, 0)),
        ] + [par(a) for a in params],
        out_specs=[pl.BlockSpec((1, tm, ch), lambda b, j: (b, j, 0)),
                   pl.BlockSpec((2, 1, nc, 2 * SSM_HEADS, SSD_CHUNK), lambda b, j: (0, b, j, 0, 0))],
        out_shape=[jax.ShapeDtypeStruct((bsz, t_all, ch), F32),
                   jax.ShapeDtypeStruct((2, bsz, t_all // SSD_CHUNK, 2 * SSM_HEADS, SSD_CHUNK), F32)],
        scratch_shapes=[pltpu.VMEM((3, tm + 2 * halo, ch), F32)],
        compiler_params=pltpu.CompilerParams(dimension_semantics=("parallel", "parallel"),
                                             vmem_limit_bytes=VMEM_LIMIT),
        name="ssm_prep",
    )(xbc, xbc, xbc, small, *params)


def _route(logits):
    lane = lax.broadcasted_iota(jnp.int32, logits.shape, 1)
    valid = lane < N_EXPERTS
    neg = -jnp.inf
    big = jnp.int32(1 << 20)
    el = pltpu.roll(logits, LANES - N_EXPERTS, 1)
    glm = jnp.where(valid, logits, neg)
    gmax = jnp.max(glm, -1, keepdims=True)
    g_sel_lane = jnp.min(jnp.where(glm == gmax, lane, big), -1, keepdims=True)
    g_sel = g_sel_lane // EXPERTS_PER_GROUP
    in_group = (lane // EXPERTS_PER_GROUP) == g_sel
    gsum = jnp.sum(jnp.where(valid, jnp.exp(glm - gmax), 0.0), -1, keepdims=True) / EXPERTS_PER_GROUP
    p_group = 1.0 / gsum
    elm = jnp.where(in_group & valid, el, neg)
    m1 = jnp.max(elm, -1, keepdims=True)
    i1 = jnp.min(jnp.where(elm == m1, lane, big), -1, keepdims=True)
    elm2 = jnp.where(lane == i1, neg, elm)
    m2 = jnp.max(elm2, -1, keepdims=True)
    i2 = jnp.min(jnp.where(elm2 == m2, lane, big), -1, keepdims=True)
    p2 = jnp.exp(m2 - m1)
    wa = p_group / (1.0 + p2)
    wb = p_group * p2 / (1.0 + p2)
    return jnp.where(lane == i1, wa, 0.0) + jnp.where(lane == i2, wb, 0.0), g_sel


def _moe_kernel(x_ref, g_ref, sh_ref, sc_ref, gate_ref, rw_ref, rb_ref, fg_ref, w1_ref, w3_ref, w2_ref,
                o_ref, h_ref, comb_ref, code_ref, *, final_norm):
    grp = pl.program_id(1)
    tm, d = x_ref.shape

    @pl.when(grp == 0)
    def _():
        h = _norm_mod(x_ref[...], g_ref[...], sh_ref[0], sc_ref[0])
        h_ref[...] = h.astype(BF16)
        h_hi, h_mid = _split2(h)
        w_hi, w_mid = _split2(rw_ref[...])
        logits = _dot(h_hi, w_hi) + _dot(h_hi, w_mid) + _dot(h_mid, w_hi) + rb_ref[...]
        comb, g_sel = _route(logits)
        comb_ref[...] = comb
        lane = lax.broadcasted_iota(jnp.int32, (tm, LANES), 1)
        member = jnp.where(lane == g_sel, 1.0, 0.0)
        r_t = lax.broadcasted_iota(jnp.int32, (tm, tm), 0)
        c_t = lax.broadcasted_iota(jnp.int32, (tm, tm), 1)
        before = jnp.where(c_t < r_t, 1.0, 0.0).astype(BF16)
        rank = _dot(before, member.astype(BF16))
        code = jnp.where(member > 0.0, rank, -1.0)
        code_ref[...] = code.T[:code_ref.shape[0]]
        o_ref[...] = jnp.zeros_like(o_ref)

    crow = code_ref[pl.ds(grp, 1), :]
    count = jnp.max(crow).astype(jnp.int32) + 1
    comb_parts = _split2(comb_ref[...])

    def sub_block(first, sub):
        sub_i = lax.broadcasted_iota(jnp.int32, (sub, tm), 0).astype(F32)
        lane_c = lax.broadcasted_iota(jnp.int32, (sub, LANES), 1)
        sel = jnp.where(crow - first.astype(F32) == sub_i, 1.0, 0.0).astype(BF16)
        hg = _dot(sel, h_ref[...]).astype(BF16)
        cg = sum(_dot(sel, p) for p in comb_parts)
        ysum = jnp.zeros((sub, d), F32)
        for e in range(EXPERTS_PER_GROUP):
            a = _dot(hg, w1_ref[0, e])
            b = _dot(hg, w3_ref[0, e])
            hid = (a * jax.nn.sigmoid(a) * b).astype(BF16)
            col = jnp.sum(jnp.where(lane_c == grp * EXPERTS_PER_GROUP + e, cg, 0.0), -1, keepdims=True)
            ysum = ysum + col * _dot(hid, w2_ref[0, e])
        yb = ysum.astype(BF16)
        for c0 in range(0, d, MOE_SCATTER_COLS):
            cs = slice(c0, c0 + MOE_SCATTER_COLS)
            o_ref[:, cs] += _dot_tn(sel, yb[:, cs])

    def body(j, carry):
        sub_block(j * MOE_SUB, MOE_SUB)
        return carry

    full = count // MOE_SUB
    rest = count - full * MOE_SUB
    lax.fori_loop(0, full, body, 0)

    @pl.when(rest > MOE_SUB // 2)
    def _():
        sub_block(full * MOE_SUB, MOE_SUB)

    @pl.when(jnp.logical_and(rest > 0, rest <= MOE_SUB // 2))
    def _():
        sub_block(full * MOE_SUB, MOE_SUB // 2)

    @pl.when(grp == N_GROUPS - 1)
    def _():
        out = x_ref[...] + gate_ref[0] * o_ref[...]
        if final_norm:
            out = out * lax.rsqrt(jnp.mean(out * out, -1, keepdims=True) + NORM_EPS) * fg_ref[...]
        o_ref[...] = out


def moe_block(x, g, shift, scale, gate, rw, rb, final_g, w1, w3, w2, *, tm, final_norm):
    bsz, t, d = x.shape
    n = bsz * t
    tm = min(tm, t)
    assert t % tm == 0 and tm % MOE_SUB == 0 and d % MOE_SCATTER_COLS == 0
    tpb = t // tm
    epg = EXPERTS_PER_GROUP

    def mod_spec(mod):
        if mod.shape[0] == 1:
            return pl.BlockSpec((1, 1, d), lambda i, e: (0, 0, 0))
        return pl.BlockSpec((1, 1, d), lambda i, e: (i // tpb, 0, 0))

    const = lambda shape: pl.BlockSpec(shape, lambda i, e: (0,) * len(shape))
    out = pl.pallas_call(
        functools.partial(_moe_kernel, final_norm=final_norm),
        grid=(n // tm, N_GROUPS),
        in_specs=[
            pl.BlockSpec((tm, d), lambda i, e: (i, 0)),
            const((1, d)),
            mod_spec(shift[:, None]), mod_spec(scale[:, None]), mod_spec(gate[:, None]),
            const(rw.shape), const(rb.shape), const((1, d)),
            pl.BlockSpec((1, epg, d, D_EXPERT), lambda i, e: (e, 0, 0, 0)),
            pl.BlockSpec((1, epg, d, D_EXPERT), lambda i, e: (e, 0, 0, 0)),
            pl.BlockSpec((1, epg, D_EXPERT, d), lambda i, e: (e, 0, 0, 0)),
        ],
        out_specs=pl.BlockSpec((tm, d), lambda i, e: (i, 0)),
        out_shape=jax.ShapeDtypeStruct((n, d), F32),
        scratch_shapes=[pltpu.VMEM((tm, d), BF16), pltpu.VMEM((tm, LANES), F32), pltpu.VMEM((8, tm), F32)],
        compiler_params=pltpu.CompilerParams(dimension_semantics=("parallel", "arbitrary"),
                                             vmem_limit_bytes=VMEM_LIMIT),
        name="moe",
    )(x.reshape(n, d), g[None], shift[:, None], scale[:, None], gate[:, None], rw, rb, final_g[None], w1, w3, w2)
    return out.reshape(bsz, t, d)


def _scan_masks(n, rev):
    row = lax.broadcasted_iota(jnp.int32, (n, n), 0)
    col = lax.broadcasted_iota(jnp.int32, (n, n), 1)
    d = (row - col) * jnp.where(rev, -1, 1)
    return d >= 0, d > 0


def _chunk_rows(s, nsub, size, rev):
    return pl.ds(pl.multiple_of(jnp.where(rev, (nsub - 1 - s) * size, s * size), size), size)


def _last_row(x, rev):
    n = x.shape[0]
    return jnp.where(rev, x[0:1], x[n - 1:n])


def _scan_specs(nseq, blk, n_ctx_blk, n_blk, widths_shared, widths_dir):
    def tblock(i, c):
        back = jnp.where(c < n_ctx_blk, n_ctx_blk - 1 - c, n_ctx_blk + n_blk - 1 - c)
        return jnp.where(i >= nseq, back, c)
    def shared_spec(w, col_blk=0):
        return pl.BlockSpec((1, blk, w), lambda i, c: (i % nseq, tblock(i, c), col_blk))
    shared = [shared_spec(*w) if isinstance(w, tuple) else shared_spec(w) for w in widths_shared]
    per_dir = [pl.BlockSpec((1, blk, w), lambda i, c: (i, tblock(i, c), 0)) for w in widths_dir]
    return shared, per_dir, tblock


def _rwkv_kernel(r_ref, k_ref, v_ref, kk_ref, ka_ref, lw_ref, as_ref, y_ref, st_ref, *, nsub, nseq):
    L = RW_CHUNK
    hd = RW_HD
    rev = pl.program_id(0) >= nseq

    @pl.when(pl.program_id(1) == 0)
    def _():
        st_ref[...] = jnp.zeros_like(st_ref)

    incl, strict = _scan_masks(L, rev)
    tri = jnp.where(incl, 1.0, 0.0).astype(BF16)
    zeros = jnp.zeros((L, hd), BF16)
    chains = [(s, h) for s in range(nsub) for h in range(RW_HEADS)]
    rows = [_chunk_rows(s, nsub, L, rev) for s in range(nsub)]

    prep = []
    for s in range(nsub):
        lw = lw_ref[0, rows[s], :]
        c = _cumsum_rows(tri, lw)
        c_end = _last_row(c, rev)
        e_inv = jnp.exp(-c)
        e_end = jnp.exp(c_end - c)
        a_sig = as_ref[0, rows[s], :]
        kk = kk_ref[0, rows[s], :]
        k = k_ref[0, rows[s], :]
        kmod = k * (1.0 + (a_sig - 1.0) * ka_ref[...])
        bv = kk * a_sig
        prep.append(dict(
            at=-kk * jnp.exp(c - lw), rt=r_ref[0, rows[s], :] * jnp.exp(c),
            bt=(bv * e_inv).astype(BF16), kt=(kmod * e_inv).astype(BF16),
            bh=(bv * e_end).astype(BF16), kh=(kmod * e_end).astype(BF16),
            p_end=jnp.exp(c_end), v=v_ref[0, rows[s], :].astype(BF16)))

    def head(name, s, h):
        return prep[s][name][:, h * hd:(h + 1) * hd]

    sc = {}
    for s, h in chains:
        lhs = jnp.concatenate([head("at", s, h), head("rt", s, h)], 0).astype(BF16)
        rhs = jnp.concatenate([head("bt", s, h), head("kt", s, h)], 0)
        sc[s, h] = _dot_nt(lhs, rhs)
    nmat, x, mr = {}, {}, {}
    for s, h in chains:
        m = sc[s, h]
        nmat[s, h] = jnp.where(strict, m[:L, :L], 0.0)
        mak = jnp.where(strict, m[:L, L:], 0.0).astype(BF16)
        mr[s, h] = jnp.concatenate([jnp.where(incl, m[L:, :L], 0.0), jnp.where(incl, m[L:, L:], 0.0)], 1).astype(BF16)
        x[s, h] = jnp.concatenate([head("at", s, h), _dot(mak, head("v", s, h))], 1)
    steps = L.bit_length() - 1
    for i in range(steps):
        for s, h in chains:
            nb = nmat[s, h].astype(BF16)
            xb = x[s, h].astype(BF16)
            if i + 1 < steps:
                prod = _dot(nb, jnp.concatenate([xb, nb], 1))
                x[s, h] = x[s, h] + prod[:, :2 * hd]
                nmat[s, h] = prod[:, 2 * hd:]
            else:
                x[s, h] = x[s, h] + _dot(nb, xb)
    ftop, bhw_t, gt = {}, {}, {}
    for s, h in chains:
        xb = x[s, h].astype(BF16)
        vb = head("v", s, h)
        z = jnp.concatenate([xb, jnp.concatenate([zeros, vb], 1)], 0)
        ftop[s, h] = _dot(mr[s, h], z)
        t1 = _dot_tn(xb, head("bh", s, h))
        bhw_t[s, h] = t1[:hd].astype(BF16)
        gt[s, h] = t1[hd:] + _dot_tn(vb, head("kh", s, h))
    st = [st_ref[h] for h in range(RW_HEADS)]
    for s in range(nsub):
        ys = []
        for h in range(RW_HEADS):
            stb = st[h].astype(BF16)
            q = (head("rt", s, h) + ftop[s, h][:, :hd]).astype(BF16)
            ys.append(_dot_nt(q, stb) + ftop[s, h][:, hd:])
            st[h] = st[h] * head("p_end", s, h) + _dot(stb, bhw_t[s, h]) + gt[s, h]
        y_ref[0, rows[s], :] = jnp.concatenate(ys, 1).astype(y_ref.dtype)
    for h in range(RW_HEADS):
        st_ref[h] = st[h]


def rwkv_scan(r, k, v, kk, k_a, lw, a_sig, *, t_ctx, nsub=4):
    nseq, t, w = r.shape
    blk = RW_CHUNK * nsub
    assert t % blk == 0 and t_ctx % blk == 0
    shared, per_dir, _ = _scan_specs(nseq, blk, t_ctx // blk, t // blk, [w] * 4, [w] * 3)
    return pl.pallas_call(
        functools.partial(_rwkv_kernel, nsub=nsub, nseq=nseq),
        grid=(2 * nseq, t // blk),
        in_specs=shared + [pl.BlockSpec((1, w), lambda i, c: (0, 0))] + per_dir[:2],
        out_specs=per_dir[2],
        out_shape=jax.ShapeDtypeStruct((2 * nseq, t, w), SCAN_OUT_DTYPE),
        scratch_shapes=[pltpu.VMEM((RW_HEADS, RW_HD, RW_HD), F32)],
        compiler_params=pltpu.CompilerParams(dimension_semantics=("parallel", "arbitrary"),
                                             vmem_limit_bytes=VMEM_LIMIT),
        name="rwkv_scan",
    )(r, k, v, kk, k_a[None], lw, a_sig)


def _expand_cols(x, e):
    return sum(_dot(p, e) for p in _split2(x))


def _ssd_kernel(x_ref, b_ref, c_ref, dd_ref, y_ref, st_ref, *, nsub, nseq):
    L = SSD_CHUNK
    nh = SSM_HEADS
    pw = 2 * SSM_HD
    npairs = nh // 2
    pairs_per_group = npairs // SSM_GROUPS
    rev = pl.program_id(0) >= nseq

    @pl.when(pl.program_id(1) == 0)
    def _():
        st_ref[...] = jnp.zeros_like(st_ref)

    incl, _ = _scan_masks(L, rev)
    tri = jnp.where(incl, 1.0, 0.0).astype(BF16)
    h_i = lax.broadcasted_iota(jnp.int32, (nh, nh * SSM_HD), 0)
    c_i = lax.broadcasted_iota(jnp.int32, (nh, nh * SSM_HD), 1)
    e_head = jnp.where(c_i // SSM_HD == h_i, 1.0, 0.0).astype(BF16)
    h_j = lax.broadcasted_iota(jnp.int32, (nh, nh * L), 0)
    c_j = lax.broadcasted_iota(jnp.int32, (nh, nh * L), 1)
    e_wide = jnp.where(c_j // L == h_j, 1.0, 0.0).astype(BF16)
    lane_p = lax.broadcasted_iota(jnp.int32, (L, pw), 1)
    first_half = lane_p < SSM_HD
    r_bd = lax.broadcasted_iota(jnp.int32, (pw, pw), 0)
    c_bd = lax.broadcasted_iota(jnp.int32, (pw, pw), 1)
    bd_mask = (r_bd < SSM_HD) == (c_bd < SSM_HD)

    rows = [_chunk_rows(s, nsub, L, rev) for s in range(nsub)]
    work = []
    for s in range(nsub):
        dd = dd_ref[0, jnp.where(rev, nsub - 1 - s, s)]
        da_parts = _split2(dd[:nh])
        acs_row = sum(_dot_nt(p, tri) for p in da_parts)
        acs_col = sum(_dot_nt(tri, p) for p in da_parts)
        colx = _expand_cols(acs_col, e_wide)
        dtx = sum(_dot_tn(p, e_head) for p in _split2(dd[nh:]))
        xdt = (x_ref[0, rows[s], :] * dtx).astype(BF16)
        bm = b_ref[0, rows[s], :]
        cm = c_ref[0, rows[s], :]
        bm_sw = pltpu.roll(bm, SSM_STATE, 1)
        cm_sw = pltpu.roll(cm, SSM_STATE, 1)
        b2 = [jnp.where(first_half, bm, bm_sw), jnp.where(first_half, bm_sw, bm)]
        c2 = [jnp.where(first_half, cm, cm_sw), jnp.where(first_half, cm_sw, cm)]
        cb = [_dot_nt(cm[:, g * SSM_STATE:(g + 1) * SSM_STATE].astype(BF16),
                      bm[:, g * SSM_STATE:(g + 1) * SSM_STATE].astype(BF16)) for g in range(SSM_GROUPS)]
        work.append(dict(acs_row=acs_row, colx=colx, xdt=xdt, b2=b2, c2=c2, cb=cb))

    y_diag, ce, new, ea = {}, {}, {}, {}
    for s in range(nsub):
        w = work[s]
        for p in range(npairs):
            g = p // pairs_per_group
            h0, h1 = 2 * p, 2 * p + 1
            cx0 = w["colx"][:, h0 * L:(h0 + 1) * L]
            cx1 = w["colx"][:, h1 * L:(h1 + 1) * L]
            s0 = w["cb"][g] * jnp.exp(jnp.where(incl, cx0 - w["acs_row"][h0:h0 + 1, :], -jnp.inf))
            s1 = w["cb"][g] * jnp.exp(jnp.where(incl, cx1 - w["acs_row"][h1:h1 + 1, :], -jnp.inf))
            xp = w["xdt"][:, p * pw:(p + 1) * pw]
            zero = jnp.zeros_like(xp)
            x_bd = jnp.concatenate([jnp.where(first_half, xp, zero), jnp.where(first_half, zero, xp)], 0)
            y_diag[s, p] = _dot(jnp.concatenate([s0, s1], 1).astype(BF16), x_bd)
            col = jnp.where(first_half, cx0[:, :pw], cx1[:, :pw])
            a_end = _last_row(col, rev)
            ce[s, p] = (w["c2"][g] * jnp.exp(col)).astype(BF16)
            be = (w["b2"][g] * jnp.exp(a_end - col)).astype(BF16)
            new[s, p] = jnp.where(bd_mask, _dot_tn(be, xp), 0.0)
            ea[s, p] = jnp.exp(a_end)
    st = [st_ref[p] for p in range(npairs)]
    for s in range(nsub):
        ys = []
        for p in range(npairs):
            ys.append(y_diag[s, p] + _dot(ce[s, p], st[p].astype(BF16)))
            st[p] = st[p] * ea[s, p] + new[s, p]
        y_ref[0, rows[s], :] = jnp.concatenate(ys, 1).astype(y_ref.dtype)
    for p in range(npairs):
        st_ref[p] = st[p]


def ssd_scan(conv, dd, *, t_ctx, nsub=2):
    nseq, t, _ = conv.shape
    w = SSM_W
    blk = SSD_CHUNK * nsub
    assert t % blk == 0 and t_ctx % blk == 0
    assert SSM_STATE == SSM_HD and SSD_CHUNK == 2 * SSM_HD
    gw = SSM_GROUPS * SSM_STATE
    shared, per_dir, tblock = _scan_specs(nseq, blk, t_ctx // blk, t // blk,
                                          [(w, 0), (gw, w // gw), (gw, w // gw + 1)], [w])
    dd_spec = pl.BlockSpec((1, nsub, 2 * SSM_HEADS, SSD_CHUNK), lambda i, c: (i, tblock(i, c), 0, 0))
    return pl.pallas_call(
        functools.partial(_ssd_kernel, nsub=nsub, nseq=nseq),
        grid=(2 * nseq, t // blk),
        in_specs=shared + [dd_spec],
        out_specs=per_dir[0],
        out_shape=jax.ShapeDtypeStruct((2 * nseq, t, w), SCAN_OUT_DTYPE),
        scratch_shapes=[pltpu.VMEM((SSM_HEADS // 2, 2 * SSM_STATE, 2 * SSM_HD), F32)],
        compiler_params=pltpu.CompilerParams(dimension_semantics=("parallel", "arbitrary"),
                                             vmem_limit_bytes=VMEM_LIMIT),
        name="ssd_scan",
    )(conv, conv, conv, dd)


def _gla_kernel(q_ref, k_ref, v_ref, sm_ref, ga_ref, gb_ref, y_ref, st_ref, lg_ref, *, nsub, nseq):
    L = GLA_CHUNK
    rev = pl.program_id(0) >= nseq

    @pl.when(pl.program_id(1) == 0)
    def _():
        st_ref[...] = jnp.zeros_like(st_ref)

    nh = GLA_HEADS
    incl, _ = _scan_masks(L, rev)
    tri = jnp.where(incl, 1.0, 0.0).astype(BF16)
    incl_h = jnp.concatenate([incl] * nh, 0)
    r_q = lax.broadcasted_iota(jnp.int32, (nh * L, GLA_KW), 0)
    c_q = lax.broadcasted_iota(jnp.int32, (nh * L, GLA_KW), 1)
    q_mask = r_q // L == c_q // GLA_DK
    r_s = lax.broadcasted_iota(jnp.int32, (GLA_VW, GLA_KW), 0)
    c_s = lax.broadcasted_iota(jnp.int32, (GLA_VW, GLA_KW), 1)
    st_mask = r_s // GLA_DV == c_s // GLA_DK
    lane_v = lax.broadcasted_iota(jnp.int32, (L, GLA_VW), 1) // GLA_DV
    rows = [_chunk_rows(s, nsub, L, rev) for s in range(nsub)]

    gate_logit = _dot3(sm_ref[0], ga_ref[0]) + gb_ref[0]
    lg_ref[...] = -_softplus(-gate_logit) * (1.0 / GLA_TAU)
    bcs = [_cumsum_rows(tri, lg_ref[rows[s], :]) for s in range(nsub)]
    vb, ke, q_st, qd, kd, eb = [], [], [], [], [], []
    for s in range(nsub):
        mid = jnp.where(rev, bcs[s][L - 1 - L // 2:L - L // 2], bcs[s][L // 2:L // 2 + 1])
        bend = _last_row(bcs[s], rev)
        q = q_ref[0, rows[s], :] * GLA_DK ** -0.5
        k = k_ref[0, rows[s], :]
        vb.append(v_ref[0, rows[s], :].astype(BF16))
        qe = (q * jnp.exp(bcs[s] - mid)).astype(BF16)
        ke.append((k * jnp.exp(mid - bcs[s])).astype(BF16))
        q_st.append(jnp.where(q_mask, jnp.concatenate([qe] * nh, 0), jnp.zeros((), BF16)))
        qd.append((q * jnp.exp(bcs[s])).astype(BF16))
        kd.append((k * jnp.exp(bend - bcs[s])).astype(BF16))
        eb.append(jnp.exp(bend))
    att = [jnp.where(incl_h, _dot_nt(q_st[s], ke[s]), 0.0).astype(BF16) for s in range(nsub)]
    full = [_dot(att[s], vb[s]) for s in range(nsub)]
    o_intra = [sum(jnp.where(lane_v == h, full[s][h * L:(h + 1) * L], 0.0) for h in range(nh)) for s in range(nsub)]
    kv_t = [jnp.where(st_mask, _dot_tn(vb[s], kd[s]), 0.0) for s in range(nsub)]
    st = st_ref[...]
    for s in range(nsub):
        y_ref[0, rows[s], :] = (o_intra[s] + _dot_nt(qd[s], st.astype(BF16))).astype(y_ref.dtype)
        st = st * eb[s] + kv_t[s]
    st_ref[...] = st


def gla_scan(slab, small, ga2, gb, *, t_ctx, nsub=4):
    nseq, t, _ = slab.shape
    kw, vw = GLA_KW, GLA_VW
    blk = GLA_CHUNK * nsub
    assert t % blk == 0 and t_ctx % blk == 0
    shared, per_dir, _ = _scan_specs(nseq, blk, t_ctx // blk, t // blk,
                                     [(kw, 0), (kw, 1), (vw, 2 * kw // vw), LANES], [vw])
    rank = ga2.shape[1]
    ga_pad = jnp.zeros((2, LANES, kw), F32).at[:, SSM_HEADS:SSM_HEADS + rank].set(ga2)
    return pl.pallas_call(
        functools.partial(_gla_kernel, nsub=nsub, nseq=nseq),
        grid=(2 * nseq, t // blk),
        in_specs=shared + [pl.BlockSpec((1, LANES, kw), lambda i, c: (i // nseq, 0, 0)),
                           pl.BlockSpec((1, 1, kw), lambda i, c: (i // nseq, 0, 0))],
        out_specs=per_dir[0],
        out_shape=jax.ShapeDtypeStruct((2 * nseq, t, vw), SCAN_OUT_DTYPE),
        scratch_shapes=[pltpu.VMEM((GLA_VW, GLA_KW), F32), pltpu.VMEM((blk, kw), F32)],
        compiler_params=pltpu.CompilerParams(dimension_semantics=("parallel", "arbitrary"),
                                             vmem_limit_bytes=VMEM_LIMIT),
        name="gla_scan",
    )(slab, slab, slab, small, ga_pad, gb[:, None])


_HP = lax.Precision.HIGHEST


_IN_WIDTHS = (RW_COLS, SSM_W, SSM_CONV_CH, 2 * GLA_KW + 2 * GLA_VW, SMALL_W)


def _arrange_w_in(w_in):
    rw, ssm, gla = jnp.split(w_in, [RW_COLS, RW_COLS + SSM_COLS], -1)
    z, xbc, dt = jnp.split(ssm, [SSM_W, SSM_W + SSM_CONV_CH], -1)
    qkv, gl, r = jnp.split(gla, [2 * GLA_KW + GLA_VW, 2 * GLA_KW + GLA_VW + GLA_GATE_RANK], -1)
    pad = jnp.zeros((w_in.shape[0], SMALL_W - SSM_HEADS - GLA_GATE_RANK), w_in.dtype)
    return jnp.concatenate([rw, z, xbc, qkv, r, dt, gl, pad], -1).astype(BF16)


def _pad_lanes(w, reps=1):
    w = jnp.repeat(w, reps, axis=-1) if reps > 1 else w
    return jnp.pad(w, [(0, 0)] * (w.ndim - 1) + [(0, LANES - w.shape[-1])])


def kernel(x, c, ctx, c_ctx, ada_w, ada_b, norm1_g, norm2_g, w_in, w_out, rw_mu_prev, rw_mu_next, rw_w0, rw_w2, rw_a0, rw_a2, rw_g2, rw_k_k, rw_k_a, rw_r_k, rw_gn_g, rw_gn_b, ssm_conv_w, ssm_conv_b, ssm_dt_bias, ssm_a_log, ssm_d, ssm_norm_g, gla_ga2, gla_gb, gla_norm_g, moe_rg_w, moe_rg_b, moe_re_w, moe_re_b, moe_w1, moe_w3, moe_w2, final_g):
    depth = ada_w.shape[0]
    bsz, t_lat, _ = x.shape
    t_ctx = ctx.shape[1]
    assert t_lat % GRID_W == 0
    cond_l = jax.nn.silu(c)
    cond_c = jax.nn.silu(c_ctx)[None]
    tm = PROJ_TM
    for l in range(depth):
        ctx_out = l < depth - 1
        mod_l = jnp.split(jnp.dot(cond_l, ada_w[l], precision=_HP) + ada_b[l], 6, -1)
        mod_c = jnp.split(jnp.dot(cond_c, ada_w[l], precision=_HP) + ada_b[l], 6, -1)
        w_in_l = _arrange_w_in(w_in[l])
        u_rw, z, xbc, u_gla, small = in_proj(ctx, x, norm1_g[l], mod_c[0], mod_c[1], mod_l[0], mod_l[1], w_in_l,
                                             _IN_WIDTHS, tm=tm)
        r, k, v, kk, bonus, rgate, lw, a_sig = rwkv_prep(u_rw, rw_mu_prev[l], rw_mu_next[l], rw_w0[l], rw_w2[l],
                                                         rw_a0[l], rw_a2[l], rw_g2[l], rw_k_k[l], rw_r_k[l],
                                                         t_ctx=t_ctx, tm=tm)
        merge = lambda a: a.reshape((2 * bsz,) + a.shape[2:])
        split = lambda a: a.reshape((2, bsz) + a.shape[1:])
        y_rw = split(rwkv_scan(r, k, v, kk, rw_k_a[l], merge(lw), merge(a_sig), t_ctx=t_ctx))
        conv, dd = ssm_prep(xbc, small, ssm_conv_w[l], ssm_conv_b[l], ssm_dt_bias[l], ssm_a_log[l], t_ctx=t_ctx, tm=tm)
        y_ssd = split(ssd_scan(conv, merge(dd), t_ctx=t_ctx))
        y_gla = split(gla_scan(u_gla, small, gla_ga2[l], gla_gb[l], t_ctx=t_ctx))
        mixed = (y_rw, bonus, rgate, y_ssd, conv, z, y_gla, u_gla,
                 rw_gn_g[l], rw_gn_b[l], ssm_d[l], ssm_norm_g[l], gla_norm_g[l])
        w_out_l = w_out[l].astype(BF16)
        epg = EXPERTS_PER_GROUP
        rw = _pad_lanes(jnp.concatenate([jnp.repeat(moe_rg_w[l], epg, -1), moe_re_w[l]], -1))
        rb = _pad_lanes(jnp.concatenate([jnp.repeat(moe_rg_b[l], epg, -1), moe_re_b[l]], -1)[None])
        by_group = lambda w: w.astype(BF16).reshape((N_GROUPS, epg) + w.shape[1:])
        w1, w3, w2 = by_group(moe_w1[l]), by_group(moe_w3[l]), by_group(moe_w2[l])
        x = out_proj(*mixed, w_out_l, x, mod_l[2], tm=tm, t_off=t_ctx)
        x = moe_block(x, norm2_g[l], mod_l[3], mod_l[4], mod_l[5], rw, rb, final_g, w1, w3, w2, tm=MOE_TM,
                      final_norm=not ctx_out)
        if ctx_out:
            ctx = out_proj(*mixed, w_out_l, ctx, mod_c[2], tm=tm, t_off=0)
            ctx = moe_block(ctx, norm2_g[l], mod_c[3], mod_c[4], mod_c[5], rw, rb, final_g, w1, w3, w2, tm=MOE_TM,
                            final_norm=False)
    return x
```
